```python
import math
import jax, jax.numpy as jnp
from jax import lax
import numpy as np

D_MODEL = 1024
BATCH = 8
SEQ = 4096
DEPTH = 2

CTX_LEN = 256
GRID_W = 64
D_MIX = D_MODEL
W_BR = D_MIX // 4
EPS = 1e-6

RW_HEAD = 64
RW_HEADS = W_BR // RW_HEAD
RW_DECAY_RANK = 64
RW_A_RANK = 64
RW_SHIFT = 3 * W_BR + 2 * RW_DECAY_RANK + 2 * RW_A_RANK
RW_COLS = RW_SHIFT + W_BR
RW_GN_EPS = 64e-5

S5_CH = 16
S5_GROUPS = W_BR // S5_CH
S5_P = 64
S5_COLS = 2 * W_BR

SSD_HEADDIM = 64
SSD_HEADS = W_BR // SSD_HEADDIM
SSD_NGROUPS = 2
SSD_N = 64
SSD_CONV = 5
SSD_CHUNK = 128
SSD_XBC = W_BR + 2 * SSD_NGROUPS * SSD_N
SSD_COLS = SSD_XBC + 2 * SSD_HEADS + W_BR

GLA_HEADS = 4
GLA_DK = (W_BR // 2) // GLA_HEADS
GLA_DV = W_BR // GLA_HEADS
GLA_RANK = 16
GLA_TAU = 16.0
GLA_CHUNK = 64
GLA_COLS = 2 * GLA_HEADS * GLA_DK + W_BR + 2 * GLA_RANK + W_BR

N_IN = RW_COLS + S5_COLS + SSD_COLS + GLA_COLS

kernel_name = 'hybrid_parallel_heads_rwkv7_s5_ssd_gla_dit'


def split_last(t, sizes):
    idx = np.cumsum(sizes)[:-1].tolist()
    return jnp.split(t, idx, axis=-1)


def flip(t):
    return jnp.flip(t, axis=1)


def rmsnorm(t, g):
    tf = t.astype(jnp.float32)
    tf = tf * lax.rsqrt(jnp.mean(tf * tf, axis=-1, keepdims=True) + EPS)
    return (tf * g.astype(jnp.float32)).astype(t.dtype)


def token_shift(f, grid):
    b, L, C = f.shape
    if grid:
        rows = L // GRID_W
        q = C // 4
        g = f.reshape(b, rows, GRID_W, C)
        left = jnp.pad(g[:, :, :-1, :q], ((0, 0), (0, 0), (1, 0), (0, 0)))
        right = jnp.pad(g[:, :, 1:, q:2 * q], ((0, 0), (0, 0), (0, 1), (0, 0)))
        up = jnp.pad(g[:, :-1, :, 2 * q:3 * q], ((0, 0), (1, 0), (0, 0), (0, 0)))
        down = jnp.pad(g[:, 1:, :, 3 * q:], ((0, 0), (0, 1), (0, 0), (0, 0)))
        return jnp.concatenate([left, right, up, down], axis=-1).reshape(b, L, C)
    half = C // 2
    prev = jnp.pad(f[:, :-1, :half], ((0, 0), (1, 0), (0, 0)))
    nxt = jnp.pad(f[:, 1:, half:], ((0, 0), (0, 1), (0, 0)))
    return jnp.concatenate([prev, nxt], axis=-1)


def dwconv_centred(t, w, bias):
    kw = w.shape[0]
    y = lax.conv_general_dilated(t, w.astype(t.dtype)[:, None, :], window_strides=(1,),
                                 padding=[(kw // 2, kw // 2)], dimension_numbers=('NWC', 'WIO', 'NWC'),
                                 feature_group_count=t.shape[-1])
    return y + bias


def rwkv7_scan(r, w, k, v, kk, a, s0, reverse, need_out):
    def step(S, inp):
        r_t, w_t, k_t, v_t, kk_t, a_t = inp
        sa = jnp.einsum('bhvk,bhk->bhv', S, -kk_t)
        S = S * w_t[:, :, None, :] + sa[..., None] * (kk_t * a_t)[:, :, None, :] + v_t[..., None] * k_t[:, :, None, :]
        y = jnp.einsum('bhvk,bhk->bhv', S, r_t) if need_out else None
        return S, y
    xs = tuple(jnp.moveaxis(t, 1, 0) for t in (r, w, k, v, kk, a))
    s_fin, ys = lax.scan(step, s0, xs, reverse=reverse)
    return (jnp.moveaxis(ys, 0, 1) if need_out else None), s_fin


def rwkv7_mixer(f, p, grid, init, need_out):
    b, L, _ = f.shape
    f = f.astype(jnp.float32)
    z, gate = f[..., :RW_SHIFT], f[..., RW_SHIFT:]
    z = z + p['rw_mu'] * (token_shift(z, grid) - z)
    r, k, v, wl, al = split_last(z, [W_BR, W_BR, W_BR, 2 * RW_DECAY_RANK, 2 * RW_A_RANK])
    heads = lambda t: t.reshape(b, L, RW_HEADS, RW_HEAD)
    r, k, v = heads(r), heads(k), heads(v)
    wl = wl.reshape(b, L, 2, RW_DECAY_RANK)
    al = al.reshape(b, L, 2, RW_A_RANK)
    w_pre = p['rw_w0'] + jnp.einsum('bldr,drc->bldc', jnp.tanh(wl), p['rw_w2'])
    decay = jnp.exp(-jnp.exp(-jax.nn.softplus(-w_pre) - 0.5))
    a = jax.nn.sigmoid(p['rw_a0'] + jnp.einsum('bldr,drc->bldc', al, p['rw_a2']))
    kk = k * p['rw_kk'].reshape(RW_HEADS, RW_HEAD)
    kk = kk * lax.rsqrt(jnp.sum(kk * kk, axis=-1, keepdims=True) + EPS)
    if init is None:
        zero = jnp.zeros((b, RW_HEADS, RW_HEAD, RW_HEAD), jnp.float32)
        init = (zero, zero)
    ys, bonus, finals = [], [], []
    for d, rev in enumerate((False, True)):
        w_d = heads(decay[:, :, d])
        a_d = heads(a[:, :, d])
        k_d = k * (1.0 + (a_d - 1.0) * p['rw_ka'].reshape(RW_HEADS, RW_HEAD))
        y_d, s_d = rwkv7_scan(r, w_d, k_d, v, kk, a_d, init[d], rev, need_out)
        finals.append(s_d)
        if need_out:
            ys.append(y_d)
            bonus.append(jnp.sum(r * k_d * p['rw_rk'], axis=-1, keepdims=True) * v)
    if not need_out:
        return None, (finals[0], finals[1])
    y = ys[0] + ys[1]
    mu = jnp.mean(y, axis=-1, keepdims=True)
    var = jnp.mean(jnp.square(y - mu), axis=-1, keepdims=True)
    y = ((y - mu) * lax.rsqrt(var + RW_GN_EPS)).reshape(b, L, W_BR) * p['rw_ln_w'] + p['rw_ln_b']
    y = y + (bonus[0] + bonus[1]).reshape(b, L, W_BR)
    return y * jax.nn.silu(gate), (finals[0], finals[1])


def s5_combine(e1, e2):
    a1, b1 = e1
    a2, b2 = e2
    return a1 * a2, a2 * b1 + b2


def s5_mixer(f, p, init, need_out):
    b, L, _ = f.shape
    u, gate = split_last(f.astype(jnp.float32), [W_BR, W_BR])
    ug = u.reshape(b, L, S5_GROUPS, S5_CH).astype(jnp.complex64)
    if init is None:
        zero = jnp.zeros((b, S5_GROUPS, S5_P), jnp.complex64)
        init = (zero, zero)
    f32 = jnp.float32
    outs, finals = [], []
    for d, rev in enumerate((False, True)):
        lam = lax.complex(p['s5_a_re'][d].astype(f32), p['s5_a_im'][d].astype(f32))
        dt = jnp.exp(p['s5_log_dt'][d].astype(f32))[:, None]
        lam_bar = jnp.exp(lam * dt)
        b_bar = ((lam_bar - 1.0) / lam)[..., None] * lax.complex(p['s5_b_re'][d].astype(f32), p['s5_b_im'][d].astype(f32))
        bu = jnp.einsum('gpc,blgc->blgp', b_bar, ug)
        edge = L - 1 if rev else 0
        bu = bu.at[:, edge].add(lam_bar * init[d])
        _, xs = lax.associative_scan(s5_combine, (jnp.broadcast_to(lam_bar, bu.shape), bu), reverse=rev, axis=1)
        finals.append(xs[:, 0] if rev else xs[:, L - 1])
        if need_out:
            c_c = lax.complex(p['s5_c_re'][d].astype(f32), p['s5_c_im'][d].astype(f32))
            outs.append(jnp.real(jnp.einsum('gcp,blgp->blgc', c_c, xs)))
    if not need_out:
        return None, (finals[0], finals[1])
    y = (outs[0] + outs[1]).reshape(b, L, W_BR) + p['s5_d'] * u
    y = jax.nn.gelu(y)
    y = y * jax.nn.sigmoid(y @ p['s5_glu_w'] + p['s5_glu_b'])
    return y * jax.nn.silu(gate), (finals[0], finals[1])


def chunk_state_pass(decay, contrib, s0):
    def step(S, inp):
        dec, ctb = inp
        return dec * S + ctb, S
    s_fin, prev = lax.scan(step, s0, (jnp.moveaxis(decay, 1, 0), jnp.moveaxis(contrib, 1, 0)))
    return jnp.moveaxis(prev, 0, 1), s_fin


def ssd_chunked(q, k, v, log_a, s0, need_out):
    b, L, H, _ = q.shape
    nc = L // SSD_CHUNK
    ch = lambda t: t.reshape((b, nc, SSD_CHUNK) + t.shape[2:])
    q, k, v, log_a = ch(q), ch(k), ch(v), ch(log_a)
    cs = jnp.cumsum(log_a, axis=2)
    cs_last = cs[:, :, -1:]
    contrib = jnp.einsum('bcqhn,bcqhp->bchnp', k * jnp.exp(cs_last - cs)[..., None], v)
    prev, s_fin = chunk_state_pass(jnp.exp(cs_last[:, :, 0])[..., None, None], contrib, s0)
    if not need_out:
        return None, s_fin
    cs_h = jnp.moveaxis(cs, 2, 3)
    within = jnp.tril(jnp.ones((SSD_CHUNK, SSD_CHUNK), bool))
    seg = jnp.exp(jnp.where(within, cs_h[..., :, None] - cs_h[..., None, :], -jnp.inf))
    scores = jnp.einsum('bcihn,bcjhn->bchij', q, k) * seg
    y = jnp.einsum('bchij,bcjhp->bcihp', scores, v) + jnp.einsum('bcihn,bchnp->bcihp', q * jnp.exp(cs)[..., None], prev)
    return y.reshape(b, L, H, -1), s_fin


def gla_chunked(q, k, v, log_a, s0, need_out):
    b, L, H, _ = q.shape
    nc = L // GLA_CHUNK
    ch = lambda t: t.reshape((b, nc, GLA_CHUNK) + t.shape[2:])
    q, k, v, log_a = ch(q), ch(k), ch(v), ch(log_a)
    bc = jnp.cumsum(log_a, axis=2)
    b_last = bc[:, :, -1:]
    contrib = jnp.einsum('bcqhk,bcqhv->bchkv', k * jnp.exp(b_last - bc), v)
    prev, s_fin = chunk_state_pass(jnp.exp(b_last[:, :, 0])[..., None], contrib, s0)
    if not need_out:
        return None, s_fin
    b_mid = bc[:, :, GLA_CHUNK // 2:GLA_CHUNK // 2 + 1]
    scores = jnp.einsum('bcihk,bcjhk->bchij', q * jnp.exp(bc - b_mid), k * jnp.exp(b_mid - bc))
    within = jnp.tril(jnp.ones((GLA_CHUNK, GLA_CHUNK), bool))
    scores = jnp.where(within, scores, 0.0)
    y = jnp.einsum('bchij,bcjhv->bcihv', scores, v) + jnp.einsum('bcihk,bchkv->bcihv', q * jnp.exp(bc), prev)
    return y.reshape(b, L, H, -1), s_fin


def ssd_mixer(f, p, init, need_out):
    b, L, _ = f.shape
    xbc, dt_raw, z = split_last(f.astype(jnp.float32), [SSD_XBC, 2 * SSD_HEADS, W_BR])
    xbc = jax.nn.silu(dwconv_centred(xbc, p['ssd_conv_w'], p['ssd_conv_b']))
    xs, bm, cm = split_last(xbc, [W_BR, SSD_NGROUPS * SSD_N, SSD_NGROUPS * SSD_N])
    xh = xs.reshape(b, L, SSD_HEADS, SSD_HEADDIM)
    rep = SSD_HEADS // SSD_NGROUPS
    bh = jnp.repeat(bm.reshape(b, L, SSD_NGROUPS, SSD_N), rep, axis=2)
    chh = jnp.repeat(cm.reshape(b, L, SSD_NGROUPS, SSD_N), rep, axis=2)
    dt = jax.nn.softplus(dt_raw.reshape(b, L, 2, SSD_HEADS) + p['ssd_dt_bias'])
    a_neg = -jnp.exp(p['ssd_a_log'].astype(jnp.float32))
    if init is None:
        zero = jnp.zeros((b, SSD_HEADS, SSD_N, SSD_HEADDIM), jnp.float32)
        init = (zero, zero)
    ys, finals = [], []
    for d, rev in enumerate((False, True)):
        dt_d = dt[:, :, d]
        args = (chh, bh, xh * dt_d[..., None], dt_d * a_neg[d])
        if rev:
            args = tuple(flip(t) for t in args)
        y_d, s_d = ssd_chunked(args[0], args[1], args[2], args[3], init[d], need_out)
        finals.append(s_d)
        if need_out:
            ys.append(flip(y_d) if rev else y_d)
    if not need_out:
        return None, (finals[0], finals[1])
    y = ys[0] + ys[1] + p['ssd_d'][:, None] * xh
    y = rmsnorm(y.reshape(b, L, W_BR) * jax.nn.silu(z), p['ssd_norm'])
    return y, (finals[0], finals[1])


def gla_mixer(f, p, init, need_out):
    b, L, _ = f.shape
    q, k, v, gl, gate = split_last(f.astype(jnp.float32), [GLA_HEADS * GLA_DK, GLA_HEADS * GLA_DK, W_BR, 2 * GLA_RANK, W_BR])
    q = q.reshape(b, L, GLA_HEADS, GLA_DK) * GLA_DK ** -0.5
    k = k.reshape(b, L, GLA_HEADS, GLA_DK)
    v = v.reshape(b, L, GLA_HEADS, GLA_DV)
    log_a = jax.nn.log_sigmoid(jnp.einsum('bldr,drc->bldc', gl.reshape(b, L, 2, GLA_RANK), p['gla_g2']) + p['gla_gb']) / GLA_TAU
    if init is None:
        zero = jnp.zeros((b, GLA_HEADS, GLA_DK, GLA_DV), jnp.float32)
        init = (zero, zero)
    ys, finals = [], []
    for d, rev in enumerate((False, True)):
        la = log_a[:, :, d].reshape(b, L, GLA_HEADS, GLA_DK)
        args = (q, k, v, la)
        if rev:
            args = tuple(flip(t) for t in args)
        y_d, s_d = gla_chunked(args[0], args[1], args[2], args[3], init[d], need_out)
        finals.append(s_d)
        if need_out:
            ys.append(flip(y_d) if rev else y_d)
    if not need_out:
        return None, (finals[0], finals[1])
    y = ys[0] + ys[1]
    y = y * lax.rsqrt(jnp.mean(y * y, axis=-1, keepdims=True) + EPS)
    y = y.reshape(b, L, W_BR) * p['gla_norm']
    return y * jax.nn.silu(gate), (finals[0], finals[1])


def mixer_layer(h, mod, p, grid, init, need_out):
    shift, scale, gate = jnp.split(mod, 3, axis=-1)
    hn = rmsnorm(h, p['norm_pre']) * (1.0 + scale) + shift
    proj = hn @ p['w_in']
    f_rw, f_s5, f_ssd, f_gla = split_last(proj, [RW_COLS, S5_COLS, SSD_COLS, GLA_COLS])
    ini = init if init is not None else (None, None, None, None)
    y_rw, st_rw = rwkv7_mixer(f_rw, p, grid, ini[0], need_out)
    y_s5, st_s5 = s5_mixer(f_s5, p, ini[1], need_out)
    y_ssd, st_ssd = ssd_mixer(f_ssd, p, ini[2], need_out)
    y_gla, st_gla = gla_mixer(f_gla, p, ini[3], need_out)
    states = (st_rw, st_s5, st_ssd, st_gla)
    if not need_out:
        return None, states
    y = jnp.concatenate([y_rw, y_s5, y_ssd, y_gla], axis=-1).astype(h.dtype) @ p['w_out']
    return h + gate * rmsnorm(y, p['norm_post']), states


def setup_inputs(seed: int = 0) -> dict:
    key = jax.random.key(seed)
    ks = iter(jax.random.split(key, 48))
    f32 = jnp.float32

    def nrm(shape, s):
        return jax.random.normal(next(ks), shape, f32) * s

    def unif(shape, lo, hi):
        return jax.random.uniform(next(ks), shape, f32, lo, hi)

    Ld = DEPTH
    x = nrm((BATCH, SEQ, D_MODEL), 1.0)
    c = nrm((BATCH, D_MODEL), 1.0)
    ctx = nrm((BATCH, CTX_LEN, D_MODEL), 1.0)
    c_ctx = nrm((D_MODEL,), 1.0)
    ada_w = nrm((Ld, D_MODEL, 3 * D_MODEL), 0.5 * D_MODEL ** -0.5)
    ada_b = nrm((Ld, 3 * D_MODEL), 0.01)
    norm_pre = 1.0 + nrm((Ld, D_MODEL), 0.05)
    norm_post = 1.0 + nrm((Ld, D_MODEL), 0.05)
    w_in = nrm((Ld, D_MODEL, N_IN), D_MODEL ** -0.5)
    w_out = nrm((Ld, D_MIX, D_MODEL), D_MIX ** -0.5)
    rw_mu = unif((Ld, RW_SHIFT), 0.0, 1.0)
    rw_w0 = unif((Ld, 2, W_BR), -6.0, -1.0)
    rw_w2 = nrm((Ld, 2, RW_DECAY_RANK, W_BR), 0.5 * RW_DECAY_RANK ** -0.5)
    rw_a0 = nrm((Ld, 2, W_BR), 0.1)
    rw_a2 = nrm((Ld, 2, RW_A_RANK, W_BR), 0.5 * RW_A_RANK ** -0.5)
    rw_kk = 0.85 + nrm((Ld, W_BR), 0.05)
    rw_ka = 1.0 + nrm((Ld, W_BR), 0.05)
    rw_rk = nrm((Ld, RW_HEADS, RW_HEAD), 0.1)
    rw_ln_w = 1.0 + nrm((Ld, W_BR), 0.05)
    rw_ln_b = nrm((Ld, W_BR), 0.01)
    s5_a_re = -0.5 + nrm((Ld, 2, S5_GROUPS, S5_P), 0.01)
    s5_a_im = math.pi * jnp.arange(S5_P, dtype=f32) + nrm((Ld, 2, S5_GROUPS, S5_P), 0.01)
    s5_log_dt = unif((Ld, 2, S5_GROUPS), math.log(1e-3), math.log(1e-1))
    s5_b_re = nrm((Ld, 2, S5_GROUPS, S5_P, S5_CH), (2 * S5_CH) ** -0.5)
    s5_b_im = nrm((Ld, 2, S5_GROUPS, S5_P, S5_CH), (2 * S5_CH) ** -0.5)
    s5_c_re = nrm((Ld, 2, S5_GROUPS, S5_CH, S5_P), (2 * S5_P) ** -0.5)
    s5_c_im = nrm((Ld, 2, S5_GROUPS, S5_CH, S5_P), (2 * S5_P) ** -0.5)
    s5_d = nrm((Ld, W_BR), 1.0)
    s5_glu_w = nrm((Ld, W_BR, W_BR), W_BR ** -0.5)
    s5_glu_b = nrm((Ld, W_BR), 0.01)
    ssd_conv_w = nrm((Ld, SSD_CONV, SSD_XBC), SSD_CONV ** -0.5)
    ssd_conv_b = nrm((Ld, SSD_XBC), 0.01)
    dt0 = jnp.exp(unif((Ld, 2, SSD_HEADS), math.log(1e-3), math.log(1e-1)))
    ssd_dt_bias = dt0 + jnp.log(-jnp.expm1(-dt0))
    ssd_a_log = jnp.log(unif((Ld, 2, SSD_HEADS), 1.0, 16.0))
    ssd_d = 1.0 + nrm((Ld, SSD_HEADS), 0.05)
    ssd_norm = 1.0 + nrm((Ld, W_BR), 0.05)
    gla_g2 = nrm((Ld, 2, GLA_RANK, GLA_HEADS * GLA_DK), GLA_RANK ** -0.5)
    gla_gb = nrm((Ld, 2, GLA_HEADS * GLA_DK), 0.5)
    gla_norm = 1.0 + nrm((Ld, W_BR), 0.05)
    return {'x': x, 'c': c, 'ctx': ctx, 'c_ctx': c_ctx, 'ada_w': ada_w, 'ada_b': ada_b,
            'norm_pre': norm_pre, 'norm_post': norm_post, 'w_in': w_in, 'w_out': w_out,
            'rw_mu': rw_mu, 'rw_w0': rw_w0, 'rw_w2': rw_w2, 'rw_a0': rw_a0, 'rw_a2': rw_a2,
            'rw_kk': rw_kk, 'rw_ka': rw_ka, 'rw_rk': rw_rk, 'rw_ln_w': rw_ln_w, 'rw_ln_b': rw_ln_b,
            's5_a_re': s5_a_re, 's5_a_im': s5_a_im, 's5_log_dt': s5_log_dt, 's5_b_re': s5_b_re,
            's5_b_im': s5_b_im, 's5_c_re': s5_c_re, 's5_c_im': s5_c_im, 's5_d': s5_d,
            's5_glu_w': s5_glu_w, 's5_glu_b': s5_glu_b, 'ssd_conv_w': ssd_conv_w, 'ssd_conv_b': ssd_conv_b,
            'ssd_dt_bias': ssd_dt_bias, 'ssd_a_log': ssd_a_log, 'ssd_d': ssd_d, 'ssd_norm': ssd_norm,
            'gla_g2': gla_g2, 'gla_gb': gla_gb, 'gla_norm': gla_norm}


def reference(x, c, ctx, c_ctx, ada_w, ada_b, norm_pre, norm_post, w_in, w_out,
              rw_mu, rw_w0, rw_w2, rw_a0, rw_a2, rw_kk, rw_ka, rw_rk, rw_ln_w, rw_ln_b,
              s5_a_re, s5_a_im, s5_log_dt, s5_b_re, s5_b_im, s5_c_re, s5_c_im, s5_d, s5_glu_w, s5_glu_b,
              ssd_conv_w, ssd_conv_b, ssd_dt_bias, ssd_a_log, ssd_d, ssd_norm,
              gla_g2, gla_gb, gla_norm):
    h, hc = x, ctx
    silu_c = jax.nn.silu(c)
    silu_cc = jax.nn.silu(c_ctx)
    for l in range(DEPTH):
        p = dict(norm_pre=norm_pre[l], norm_post=norm_post[l], w_in=w_in[l], w_out=w_out[l],
                 rw_mu=rw_mu[l], rw_w0=rw_w0[l], rw_w2=rw_w2[l], rw_a0=rw_a0[l], rw_a2=rw_a2[l],
                 rw_kk=rw_kk[l], rw_ka=rw_ka[l], rw_rk=rw_rk[l], rw_ln_w=rw_ln_w[l], rw_ln_b=rw_ln_b[l],
                 s5_a_re=s5_a_re[l], s5_a_im=s5_a_im[l], s5_log_dt=s5_log_dt[l], s5_b_re=s5_b_re[l],
                 s5_b_im=s5_b_im[l], s5_c_re=s5_c_re[l], s5_c_im=s5_c_im[l], s5_d=s5_d[l],
                 s5_glu_w=s5_glu_w[l], s5_glu_b=s5_glu_b[l], ssd_conv_w=ssd_conv_w[l], ssd_conv_b=ssd_conv_b[l],
                 ssd_dt_bias=ssd_dt_bias[l], ssd_a_log=ssd_a_log[l], ssd_d=ssd_d[l], ssd_norm=ssd_norm[l],
                 gla_g2=gla_g2[l], gla_gb=gla_gb[l], gla_norm=gla_norm[l])
        mod = (silu_c @ ada_w[l] + ada_b[l])[:, None, :]
        mod_c = (silu_cc @ ada_w[l] + ada_b[l])[None, None, :]
        last = l == DEPTH - 1
        hc_next, ctx_states = mixer_layer(hc, mod_c, p, False, None, not last)
        h, _ = mixer_layer(h, mod, p, True, ctx_states, True)
        hc = hc_next
    return h
```

```python
import functools
import math

import jax
import jax.numpy as jnp
from jax import lax
from jax.experimental import pallas as pl
from jax.experimental.pallas import tpu as pltpu

F32 = jnp.float32
BF16 = jnp.bfloat16

EPS = 1e-6
GRID_W = 64
W_BR = 256
CH = 64
HEAD = 64
RW_GN_EPS = 64e-5
RW_LORA = 64
S5_G, S5_CH, S5_P, S5_T = 16, 16, 64, 16
SSD_CONV = 5
GLA_RANK = 16
GLA_DK = 32
GLA_TAU = 16.0
W_RW, W_S5, W_SSD, W_GLA = 1280, 512, 1152, 896
N_PROJ = W_RW + W_S5 + W_SSD + W_GLA
VMEM_LIMIT = 48 * 1024 * 1024


def _dg(a, b, mode):
    ca, cb = {"nn": (1, 0), "nt": (1, 1), "tn": (0, 0)}[mode]
    return lax.dot_general(a, b, (((ca,), (cb,)), ((), ())), preferred_element_type=F32)


def _bf(a):
    return a.astype(BF16)


def _split2(a):
    hi = _bf(a)
    return hi, _bf(a - hi.astype(F32))


def _split3(a):
    hi = _bf(a)
    r = a - hi.astype(F32)
    mid = _bf(r)
    return hi, mid, _bf(r - mid.astype(F32))


def dot1(a, b, mode="nn"):
    return _dg(_bf(a), _bf(b), mode)


def dot3(a, b, mode="nn"):
    ah, al = _split2(a)
    bh, bl = _split2(b)
    return _dg(ah, bh, mode) + (_dg(ah, bl, mode) + _dg(al, bh, mode))


def dot_xl(a_exact, b, mode="nn"):
    b1, b2, b3 = _split3(b)
    return _dg(a_exact, b1, mode) + (_dg(a_exact, b2, mode) + _dg(a_exact, b3, mode))


def dot_xr(a, b_exact, mode="nn"):
    a1, a2, a3 = _split3(a)
    return _dg(a1, b_exact, mode) + (_dg(a2, b_exact, mode) + _dg(a3, b_exact, mode))


def _sigmoid(x):
    return 1.0 / (1.0 + jnp.exp(-x))


def _silu(x):
    return x * _sigmoid(x)


def _softplus(x):
    return jnp.maximum(x, 0.0) + jnp.log(1.0 + jnp.exp(-jnp.abs(x)))


def _iota(shape, dim):
    return lax.broadcasted_iota(jnp.int32, shape, dim)


def _tri_mask(n, rev, strict):
    r, c = _iota((n, n), 0), _iota((n, n), 1)
    if rev:
        return (r < c) if strict else (r <= c)
    return (r > c) if strict else (r >= c)


def _block_ones(n, blk):
    return (_iota((n, n), 0) // blk == _iota((n, n), 1) // blk).astype(BF16)


def _expand_blockdiag(compact, rows_per_head, n_heads):
    lane_head = _iota((1, compact.shape[1]), 1) // HEAD
    return jnp.concatenate([jnp.where(lane_head == h, compact, 0.0) for h in range(n_heads)], axis=0)


def _compact_blockdiag(blk, rows_per_head, n_heads):
    lane_head = _iota((1, blk.shape[1]), 1) // HEAD
    out = jnp.where(lane_head == 0, blk[0:rows_per_head], 0.0)
    for h in range(1, n_heads):
        out = out + jnp.where(lane_head == h, blk[h * rows_per_head:(h + 1) * rows_per_head], 0.0)
    return out


def _expand_heads(cols, first, n_heads=4):
    lane_head = _iota((1, n_heads * HEAD), 1) // HEAD
    out = jnp.where(lane_head == 0, cols[:, first:first + 1], 0.0)
    for h in range(1, n_heads):
        out = jnp.where(lane_head == h, cols[:, first + h:first + h + 1], out)
    return out


def _shift_rows(x, first_row, last_row, down):
    n = x.shape[0]
    rows = _iota((n, 1), 0)
    if down:
        return jnp.where(rows == 0, first_row, pltpu.roll(x, 1, 0))
    return jnp.where(rows == n - 1, last_row, pltpu.roll(x, n - 1, 0))


def _params(sem):
    return pltpu.CompilerParams(dimension_semantics=sem, vmem_limit_bytes=VMEM_LIMIT)


def _full(shape):
    nd = len(shape)
    return pl.BlockSpec(shape, lambda *_: (0,) * nd)


def _mod_kernel(c_ref, w_ref, b_ref, o_ref):
    s = _silu(c_ref[...])
    o_ref[0] = dot3(s, w_ref[0]) + b_ref[0]


def _modulation(cvec, ada_w, ada_b):
    depth, d, n = ada_w.shape
    rows = cvec.shape[0]
    tn = 1024
    return pl.pallas_call(
        _mod_kernel,
        grid=(depth, n // tn),
        in_specs=[pl.BlockSpec((rows, d), lambda l, j: (0, 0)),
                  pl.BlockSpec((1, d, tn), lambda l, j: (l, 0, j)),
                  pl.BlockSpec((1, 1, tn), lambda l, j: (l, 0, j))],
        out_specs=pl.BlockSpec((1, rows, tn), lambda l, j: (l, 0, j)),
        out_shape=jax.ShapeDtypeStruct((depth, rows, n), F32),
        compiler_params=_params(("parallel", "parallel")),
        name="adaln_mod",
    )(cvec, ada_w, ada_b.reshape(depth, 1, n))


def _inproj_kernel(x_ref, mod_ref, g_ref, w_ref, o_rw, o_s5, o_ssd, o_gla):
    x = x_ref[0]
    d = x.shape[1]
    hn = x * lax.rsqrt(jnp.mean(x * x, axis=-1, keepdims=True) + EPS) * g_ref[...]
    m = mod_ref[0]
    hn = hn * (1.0 + m[:, d:2 * d]) + m[:, 0:d]
    p = dot1(hn, w_ref[...])
    o_rw[0] = p[:, 0:W_RW]
    o_s5[0] = p[:, W_RW:W_RW + W_S5]
    o_ssd[0] = p[:, W_RW + W_S5:W_RW + W_S5 + W_SSD]
    o_gla[0] = p[:, W_RW + W_S5 + W_SSD:N_PROJ]


def _inproj(h, mod3, mod_row, norm_pre, w_in_p, tm):
    b, L, d = h.shape
    widths = (W_RW, W_S5, W_SSD, W_GLA)
    return pl.pallas_call(
        _inproj_kernel,
        grid=(b, L // tm),
        in_specs=[pl.BlockSpec((1, tm, d), lambda i, j: (i, j, 0)),
                  pl.BlockSpec((1, 1, 3 * d), lambda i, j: (mod_row(i), 0, 0)),
                  _full((1, d)),
                  _full(w_in_p.shape)],
        out_specs=[pl.BlockSpec((1, tm, w), lambda i, j: (i, j, 0)) for w in widths],
        out_shape=[jax.ShapeDtypeStruct((b, L, w), F32) for w in widths],
        compiler_params=_params(("parallel", "parallel")),
        name="in_proj",
    )(h, mod3, norm_pre.reshape(1, d), w_in_p)


def _tri_inv(n_mat, rev):
    r, c = _iota((CH, CH), 0), _iota((CH, CH), 1)
    blk16 = (r // 16) == (c // 16)
    blk32 = (r // 32) == (c // 32)
    nd = jnp.where(blk16, n_mat, 0.0)
    d = jnp.where(r == c, 1.0, 0.0) + nd
    p = dot3(nd, nd)
    d = d + dot3(d, p)
    p = dot3(p, p)
    d = d + dot3(d, p)
    p = dot3(p, p)
    d = d + dot3(d, p)
    o = jnp.where(blk32 & jnp.logical_not(blk16), n_mat, 0.0)
    d = d + dot3(d, dot3(o, d))
    o = jnp.where(blk32, 0.0, n_mat)
    d = d + dot3(d, dot3(o, d))
    return d


def _rw_local_kernel(is_grid, nc, cur_ref, prev_ref, next_ref, mu_ref, w0_ref, w2_ref, a0_ref, a2_ref,
                     kk_ref, ka_ref, rk_ref, rt_ref, y0_ref, bonus_ref, m_ref, c_ref):
    ci = pl.program_id(1)
    has_prev = (ci > 0).astype(F32)
    has_next = (ci < nc - 1).astype(F32)
    z = cur_ref[0, :, 0:1024]
    if is_grid:
        zero_row = jnp.zeros((1, W_BR), F32)
        left = _shift_rows(z[:, 0:256], zero_row, zero_row, True)
        right = _shift_rows(z[:, 256:512], zero_row, zero_row, False)
        up = prev_ref[0] * has_prev
        down = next_ref[0] * has_next
        sh = jnp.concatenate([left, right, up, down], axis=1)
    else:
        prow = prev_ref[0][7:8, :] * has_prev
        nrow = next_ref[0][0:1, :] * has_next
        sh = jnp.concatenate([_shift_rows(z[:, 0:512], prow, prow, True),
                              _shift_rows(z[:, 512:1024], nrow, nrow, False)], axis=1)
    zs = z + mu_ref[...] * (sh - z)
    r, k, v = zs[:, 0:256], zs[:, 256:512], zs[:, 512:768]
    wl, al = zs[:, 768:896], zs[:, 896:1024]

    bo = _block_ones(W_BR, HEAD)
    lane_head = _iota((1, W_BR), 1) // HEAD
    kk = k * kk_ref[...]
    kk = kk * lax.rsqrt(dot_xr(kk * kk, bo) + EPS)
    diag_sel = _iota((W_BR, W_BR), 0) == _iota((W_BR, W_BR), 1)
    blk_sel = (_iota((W_BR, W_BR), 0) // HEAD) == (_iota((W_BR, W_BR), 1) // HEAD)

    y0 = jnp.zeros((CH, W_BR), F32)
    bonus = jnp.zeros((CH, W_BR), F32)
    for d in range(2):
        rev = d == 1
        w_pre = w0_ref[d:d + 1, :] + dot3(jnp.tanh(wl[:, RW_LORA * d:RW_LORA * (d + 1)]), w2_ref[d])
        lw = (-math.exp(-0.5)) * _sigmoid(w_pre)
        a = _sigmoid(a0_ref[d:d + 1, :] + dot3(al[:, RW_LORA * d:RW_LORA * (d + 1)], a2_ref[d]))
        k_d = k * (1.0 + (a - 1.0) * ka_ref[...])
        bonus = bonus + dot_xr(r * k_d * rk_ref[...], bo) * v

        cs = dot_xl(_tri_mask(CH, rev, False).astype(BF16), lw)
        cs_ex = cs - lw
        cs_tot = cs[0:1] if rev else cs[CH - 1:CH]
        cs_mid = cs[CH // 2:CH // 2 + 1]
        kka = kk * a
        pa = -kk * jnp.exp(cs_ex)
        pr = r * jnp.exp(cs)
        pa_m = -kk * jnp.exp(cs_ex - cs_mid)
        pr_m = r * jnp.exp(cs - cs_mid)
        e_mid = jnp.exp(cs_mid - cs)
        nb, nk = kka * e_mid, k_d * e_mid
        e_tot = jnp.exp(cs_tot - cs)
        eb, ek = kka * e_tot, k_d * e_tot
        strict = _tri_mask(CH, rev, True)
        incl = _tri_mask(CH, rev, False)

        w_all = jnp.zeros((CH, W_BR), F32)
        u_all = jnp.zeros((CH, W_BR), F32)
        rt = pr
        for h in range(4):
            mh = lane_head == h
            nb_h = jnp.where(mh, nb, 0.0)
            nk_h = jnp.where(mh, nk, 0.0)
            n_mat = jnp.where(strict, dot3(pa_m, nb_h, "nt"), 0.0)
            a_ak = jnp.where(strict, dot1(pa_m, nk_h, "nt"), 0.0)
            g_b = jnp.where(incl, dot1(pr_m, nb_h, "nt"), 0.0)
            g_k = jnp.where(incl, dot1(pr_m, nk_h, "nt"), 0.0)
            t_inv = _tri_inv(n_mat, rev)
            w_h = dot3(t_inv, pa)
            u_h = dot3(t_inv, dot1(a_ak, v))
            rt = rt + jnp.where(mh, dot1(g_b, w_h), 0.0)
            y0 = y0 + jnp.where(mh, dot1(g_b, u_h) + dot1(g_k, v), 0.0)
            w_all = w_all + jnp.where(mh, w_h, 0.0)
            u_all = u_all + jnp.where(mh, u_h, 0.0)
        m_blk = jnp.where(diag_sel, jnp.exp(cs_tot), 0.0) + jnp.where(blk_sel, dot3(eb, w_all, "tn"), 0.0)
        c_blk = jnp.where(blk_sel, dot3(eb, u_all, "tn") + dot1(ek, v, "tn"), 0.0)
        rt_ref[0, 0, d] = rt
        m_ref[0, 0, d] = _compact_blockdiag(m_blk, HEAD, 4)
        c_ref[0, 0, d] = _compact_blockdiag(c_blk, HEAD, 4)
    y0_ref[0] = y0
    bonus_ref[0] = bonus


def _halo_specs(is_grid, nc, width_cols):
    if is_grid:
        prev = pl.BlockSpec((1, CH, W_BR), lambda b, c: (b, jnp.maximum(c - 1, 0), 2))
        nxt = pl.BlockSpec((1, CH, W_BR), lambda b, c: (b, jnp.minimum(c + 1, nc - 1), 3))
    else:
        rb = CH // 8
        prev = pl.BlockSpec((1, 8, width_cols), lambda b, c: (b, jnp.maximum(c * rb - 1, 0), 0))
        nxt = pl.BlockSpec((1, 8, width_cols), lambda b, c: (b, jnp.minimum((c + 1) * rb, nc * rb - 1), 1))
    return prev, nxt


def _rw_local(f_rw, p, is_grid):
    b, L, _ = f_rw.shape
    nc = L // CH
    prev, nxt = _halo_specs(is_grid, nc, 512)
    state_shape = jax.ShapeDtypeStruct((b, nc, 2, HEAD, W_BR), F32)
    state_spec = pl.BlockSpec((1, 1, 2, HEAD, W_BR), lambda i, c: (i, c, 0, 0, 0))
    row_spec = pl.BlockSpec((1, CH, W_BR), lambda i, c: (i, c, 0))
    row_shape = jax.ShapeDtypeStruct((b, L, W_BR), F32)
    return pl.pallas_call(
        functools.partial(_rw_local_kernel, is_grid, nc),
        grid=(b, nc),
        in_specs=[pl.BlockSpec((1, CH, W_RW), lambda i, c: (i, c, 0)), prev, nxt,
                  _full((1, 1024)), _full((2, W_BR)), _full((2, RW_LORA, W_BR)), _full((2, W_BR)),
                  _full((2, RW_LORA, W_BR)), _full((1, W_BR)), _full((1, W_BR)), _full((1, W_BR))],
        out_specs=[state_spec, row_spec, row_spec, state_spec, state_spec],
        out_shape=[state_shape, row_shape, row_shape, state_shape, state_shape],
        compiler_params=_params(("parallel", "parallel")),
        name="rwkv_local",
    )(f_rw, f_rw, f_rw, p["rw_mu"].reshape(1, 1024), p["rw_w0"], p["rw_w2"], p["rw_a0"], p["rw_a2"],
      p["rw_kk"].reshape(1, W_BR), p["rw_ka"].reshape(1, W_BR), p["rw_rk"].reshape(1, W_BR))


def _rw_scan_kernel(nc, mf_ref, mr_ref, cf_ref, cr_ref, init_ref, sf_ref, sr_ref, fin_ref, s_scr):
    ci = pl.program_id(1)

    @pl.when(ci == 0)
    def _():
        s_scr[0] = _expand_blockdiag(init_ref[0, 0], HEAD, 4)
        s_scr[1] = _expand_blockdiag(init_ref[0, 1], HEAD, 4)

    for d, (m_ref, c_ref, o_ref) in enumerate(((mf_ref, cf_ref, sf_ref), (mr_ref, cr_ref, sr_ref))):
        s = s_scr[d]
        o_ref[0, 0] = _compact_blockdiag(s, HEAD, 4)
        m_blk = _expand_blockdiag(m_ref[0, 0, 0], HEAD, 4)
        c_blk = _expand_blockdiag(c_ref[0, 0, 0], HEAD, 4)
        s_scr[d] = dot3(m_blk, s) + c_blk

    @pl.when(ci == nc - 1)
    def _():
        fin_ref[0, 0] = _compact_blockdiag(s_scr[0], HEAD, 4)
        fin_ref[0, 1] = _compact_blockdiag(s_scr[1], HEAD, 4)


def _rw_scan(m, c, init):
    b, nc = m.shape[0], m.shape[1]
    fwd = pl.BlockSpec((1, 1, 1, HEAD, W_BR), lambda i, j: (i, j, 0, 0, 0))
    rev = pl.BlockSpec((1, 1, 1, HEAD, W_BR), lambda i, j: (i, nc - 1 - j, 1, 0, 0))
    ent_shape = jax.ShapeDtypeStruct((b, nc, HEAD, W_BR), F32)
    return pl.pallas_call(
        functools.partial(_rw_scan_kernel, nc),
        grid=(b, nc),
        in_specs=[fwd, rev, fwd, rev, pl.BlockSpec((1, 2, HEAD, W_BR), lambda i, j: (i, 0, 0, 0))],
        out_specs=[pl.BlockSpec((1, 1, HEAD, W_BR), lambda i, j: (i, j, 0, 0)),
                   pl.BlockSpec((1, 1, HEAD, W_BR), lambda i, j: (i, nc - 1 - j, 0, 0)),
                   pl.BlockSpec((1, 2, HEAD, W_BR), lambda i, j: (i, 0, 0, 0))],
        out_shape=[ent_shape, ent_shape, jax.ShapeDtypeStruct((b, 2, HEAD, W_BR), F32)],
        scratch_shapes=[pltpu.VMEM((2, W_BR, W_BR), F32)],
        compiler_params=_params(("parallel", "arbitrary")),
        name="rwkv_scan",
    )(m, m, c, c, init)


def _rw_out_kernel(y0_ref, bonus_ref, rt_ref, sf_ref, sr_ref, gate_ref, lnw_ref, lnb_ref, o_ref):
    y = y0_ref[0]
    y = y + dot1(rt_ref[0, 0, 0], _expand_blockdiag(sf_ref[0, 0], HEAD, 4))
    y = y + dot1(rt_ref[0, 0, 1], _expand_blockdiag(sr_ref[0, 0], HEAD, 4))
    bo = _block_ones(W_BR, HEAD)
    mu = dot_xr(y, bo) * (1.0 / HEAD)
    yc = y - mu
    var = dot_xr(yc * yc, bo) * (1.0 / HEAD)
    yn = yc * lax.rsqrt(var + RW_GN_EPS) * lnw_ref[...] + lnb_ref[...]
    o_ref[0] = (yn + bonus_ref[0]) * _silu(gate_ref[0])


def _rw_out(y0, bonus, rt, sf, sr, f_rw, p):
    b, L, _ = y0.shape
    nc = L // CH
    row_spec = pl.BlockSpec((1, CH, W_BR), lambda i, c: (i, c, 0))
    ent_spec = pl.BlockSpec((1, 1, HEAD, W_BR), lambda i, c: (i, c, 0, 0))
    return pl.pallas_call(
        _rw_out_kernel,
        grid=(b, nc),
        in_specs=[row_spec, row_spec, pl.BlockSpec((1, 1, 2, HEAD, W_BR), lambda i, c: (i, c, 0, 0, 0)),
                  ent_spec, ent_spec, pl.BlockSpec((1, CH, W_BR), lambda i, c: (i, c, 4)),
                  _full((1, W_BR)), _full((1, W_BR))],
        out_specs=row_spec,
        out_shape=jax.ShapeDtypeStruct((b, L, W_BR), F32),
        compiler_params=_params(("parallel", "parallel")),
        name="rwkv_out",
    )(y0, bonus, rt, sf, sr, f_rw, p["rw_ln_w"].reshape(1, W_BR), p["rw_ln_b"].reshape(1, W_BR))


def _ew_scan_kernel(nc, cf_ref, cr_ref, df_ref, dr_ref, init_ref, sf_ref, sr_ref, fin_ref, s_scr):
    ci = pl.program_id(1)

    @pl.when(ci == 0)
    def _():
        s_scr[...] = init_ref[0]

    for d, (c_ref, d_ref, o_ref) in enumerate(((cf_ref, df_ref, sf_ref), (cr_ref, dr_ref, sr_ref))):
        s = s_scr[d]
        o_ref[0, 0] = s
        s_scr[d] = d_ref[0, 0, 0] * s + c_ref[0, 0, 0]

    @pl.when(ci == nc - 1)
    def _():
        fin_ref[0] = s_scr[...]


def _ew_scan(cst, dec, init, name):
    b, nc, _, rows, width = cst.shape
    fwd = pl.BlockSpec((1, 1, 1, rows, width), lambda i, j: (i, j, 0, 0, 0))
    rev = pl.BlockSpec((1, 1, 1, rows, width), lambda i, j: (i, nc - 1 - j, 1, 0, 0))
    ent_shape = jax.ShapeDtypeStruct((b, nc, rows, width), F32)
    return pl.pallas_call(
        functools.partial(_ew_scan_kernel, nc),
        grid=(b, nc),
        in_specs=[fwd, rev, fwd, rev, pl.BlockSpec((1, 2, rows, width), lambda i, j: (i, 0, 0, 0))],
        out_specs=[pl.BlockSpec((1, 1, rows, width), lambda i, j: (i, j, 0, 0)),
                   pl.BlockSpec((1, 1, rows, width), lambda i, j: (i, nc - 1 - j, 0, 0)),
                   pl.BlockSpec((1, 2, rows, width), lambda i, j: (i, 0, 0, 0))],
        out_shape=[ent_shape, ent_shape, jax.ShapeDtypeStruct((b, 2, rows, width), F32)],
        scratch_shapes=[pltpu.VMEM((2, rows, width), F32)],
        compiler_params=_params(("parallel", "arbitrary")),
        name=name,
    )(cst, cst, dec, dec, init)


def _gla_log_decay(gl, g2_ref, gb_ref, d):
    x = dot3(gl[:, GLA_RANK * d:GLA_RANK * (d + 1)], g2_ref[d]) + gb_ref[d:d + 1, :]
    return -_softplus(-x) * (1.0 / GLA_TAU)


def _gla_local_kernel(f_ref, g2_ref, gb_ref, cst_ref, dec_ref):
    k = f_ref[0, :, 128:256]
    v = f_ref[0, :, 256:512]
    gl = f_ref[0, :, 768:800]
    sel = (_iota((128, W_BR), 0) // GLA_DK) == (_iota((128, W_BR), 1) // HEAD)
    first_row = _iota((CH, 128), 0) == 0
    ones = jnp.ones((CH, W_BR), BF16)
    for d in range(2):
        rev = d == 1
        la = _gla_log_decay(gl, g2_ref, gb_ref, d)
        cs = dot_xl(_tri_mask(CH, rev, False).astype(BF16), la)
        cs_tot = cs[0:1] if rev else cs[CH - 1:CH]
        ke = k * jnp.exp(cs_tot - cs)
        c_blk = jnp.where(sel, dot1(ke, v, "tn"), 0.0)
        d_blk = dot_xr(jnp.where(first_row, jnp.exp(cs_tot), 0.0), ones, "tn")
        cst_ref[0, 0, d] = _compact_blockdiag(c_blk, GLA_DK, 4)
        dec_ref[0, 0, d] = _compact_blockdiag(d_blk, GLA_DK, 4)


def _gla_local(f_gla, p):
    b, L, _ = f_gla.shape
    nc = L // CH
    st_shape = jax.ShapeDtypeStruct((b, nc, 2, GLA_DK, W_BR), F32)
    st_spec = pl.BlockSpec((1, 1, 2, GLA_DK, W_BR), lambda i, c: (i, c, 0, 0, 0))
    return pl.pallas_call(
        _gla_local_kernel,
        grid=(b, nc),
        in_specs=[pl.BlockSpec((1, CH, W_GLA), lambda i, c: (i, c, 0)),
                  _full((2, GLA_RANK, 128)), _full((2, 128))],
        out_specs=[st_spec, st_spec],
        out_shape=[st_shape, st_shape],
        compiler_params=_params(("parallel", "parallel")),
        name="gla_local",
    )(f_gla, p["gla_g2"], p["gla_gb"])


def _gla_out_kernel(f_ref, sf_ref, sr_ref, g2_ref, gb_ref, norm_ref, o_ref):
    q = f_ref[0, :, 0:128] * (GLA_DK ** -0.5)
    k = f_ref[0, :, 128:256]
    v = f_ref[0, :, 256:512]
    gate = f_ref[0, :, 512:768]
    gl = f_ref[0, :, 768:800]
    khead = _iota((1, 128), 1) // GLA_DK
    vhead = _iota((1, W_BR), 1) // HEAD
    y = jnp.zeros((CH, W_BR), F32)
    scores = [None] * 4
    for d, s_ref in enumerate((sf_ref, sr_ref)):
        rev = d == 1
        la = _gla_log_decay(gl, g2_ref, gb_ref, d)
        cs = dot_xl(_tri_mask(CH, rev, False).astype(BF16), la)
        cs_mid = cs[CH // 2:CH // 2 + 1]
        qe = q * jnp.exp(cs - cs_mid)
        kn = k * jnp.exp(cs_mid - cs)
        incl = _tri_mask(CH, rev, False)
        for h in range(4):
            sc = jnp.where(incl, dot1(jnp.where(khead == h, qe, 0.0), kn, "nt"), 0.0)
            scores[h] = sc if scores[h] is None else scores[h] + sc
        y = y + dot1(q * jnp.exp(cs), _expand_blockdiag(s_ref[0, 0], GLA_DK, 4))
    for h in range(4):
        y = y + dot1(scores[h], jnp.where(vhead == h, v, 0.0))
    ms = dot_xr(y * y, _block_ones(W_BR, HEAD)) * (1.0 / HEAD)
    y = y * lax.rsqrt(ms + EPS) * norm_ref[...]
    o_ref[0] = y * _silu(gate)


def _gla_out(f_gla, sf, sr, p):
    b, L, _ = f_gla.shape
    nc = L // CH
    ent_spec = pl.BlockSpec((1, 1, GLA_DK, W_BR), lambda i, c: (i, c, 0, 0))
    return pl.pallas_call(
        _gla_out_kernel,
        grid=(b, nc),
        in_specs=[pl.BlockSpec((1, CH, W_GLA), lambda i, c: (i, c, 0)), ent_spec, ent_spec,
                  _full((2, GLA_RANK, 128)), _full((2, 128)), _full((1, W_BR))],
        out_specs=pl.BlockSpec((1, CH, W_BR), lambda i, c: (i, c, 0)),
        out_shape=jax.ShapeDtypeStruct((b, L, W_BR), F32),
        compiler_params=_params(("parallel", "parallel")),
        name="gla_out",
    )(f_gla, sf, sr, p["gla_g2"], p["gla_gb"], p["gla_norm"].reshape(1, W_BR))


def _ssd_prep(nc, cur_ref, prev_ref, next_ref, cw_ref, cb_ref, dtb_ref, aneg_ref):
    ci = pl.program_id(1)
    has_prev = (ci > 0).astype(F32)
    has_next = (ci < nc - 1).astype(F32)
    xbc = cur_ref[0, :, 0:768]
    ext = jnp.concatenate([prev_ref[0] * has_prev, xbc, next_ref[0] * has_next], axis=0)
    acc = cb_ref[...] + cw_ref[0:1, :] * ext[6:6 + CH]
    for j in range(1, SSD_CONV):
        acc = acc + cw_ref[j:j + 1, :] * ext[6 + j:6 + j + CH]
    xbc = _silu(acc)
    dt = _softplus(cur_ref[0, :, 1024:1032] + dtb_ref[...])
    la = dt * aneg_ref[...]
    lane = _iota((1, 8), 1)
    cs = jnp.where(lane < 4, dot_xl(_tri_mask(CH, False, False).astype(BF16), la),
                   dot_xl(_tri_mask(CH, True, False).astype(BF16), la))
    cs_tot = jnp.where(lane < 4, cs[CH - 1:CH], cs[0:1])
    return xbc[:, 0:256], xbc[:, 256:512], xbc[:, 512:768], dt, cs, cs_tot


def _ssd_local_kernel(nc, cur_ref, prev_ref, next_ref, cw_ref, cb_ref, dtb_ref, aneg_ref, cst_ref, dec_ref):
    x, b_rep, _, dt, cs, cs_tot = _ssd_prep(nc, cur_ref, prev_ref, next_ref, cw_ref, cb_ref, dtb_ref, aneg_ref)
    wgt = jnp.exp(cs_tot - cs) * dt
    sel = (_iota((W_BR, W_BR), 0) // HEAD) == (_iota((W_BR, W_BR), 1) // HEAD)
    e_tot = jnp.exp(cs_tot)
    for d in range(2):
        xs = x * _expand_heads(wgt, 4 * d)
        c_blk = jnp.where(sel, dot1(b_rep, xs, "tn"), 0.0)
        cst_ref[0, 0, d] = _compact_blockdiag(c_blk, HEAD, 4)
        dec_ref[0, 0, d] = jnp.broadcast_to(_expand_heads(e_tot, 4 * d), (HEAD, W_BR))


def _ssd_halo_specs(nc):
    rb = CH // 8
    prev = pl.BlockSpec((1, 8, 768), lambda b, c: (b, jnp.maximum(c * rb - 1, 0), 0))
    nxt = pl.BlockSpec((1, 8, 768), lambda b, c: (b, jnp.minimum((c + 1) * rb, nc * rb - 1), 0))
    return prev, nxt


def _ssd_param_args(p):
    cw = jnp.concatenate([p["conv_w_rep"], jnp.zeros((8 - SSD_CONV, 768), F32)], axis=0)
    return cw, p["conv_b_rep"].reshape(1, 768), p["ssd_dt_bias"].reshape(1, 8), p["ssd_a_neg"].reshape(1, 8)


def _ssd_local(f_ssd, p):
    b, L, _ = f_ssd.shape
    nc = L // CH
    prev, nxt = _ssd_halo_specs(nc)
    st_shape = jax.ShapeDtypeStruct((b, nc, 2, HEAD, W_BR), F32)
    st_spec = pl.BlockSpec((1, 1, 2, HEAD, W_BR), lambda i, c: (i, c, 0, 0, 0))
    return pl.pallas_call(
        functools.partial(_ssd_local_kernel, nc),
        grid=(b, nc),
        in_specs=[pl.BlockSpec((1, CH, W_SSD), lambda i, c: (i, c, 0)), prev, nxt,
                  _full((8, 768)), _full((1, 768)), _full((1, 8)), _full((1, 8))],
        out_specs=[st_spec, st_spec],
        out_shape=[st_shape, st_shape],
        compiler_params=_params(("parallel", "parallel")),
        name="ssd_local",
    )(f_ssd, f_ssd, f_ssd, *_ssd_param_args(p))


def _ssd_out_kernel(nc, cur_ref, prev_ref, next_ref, cw_ref, cb_ref, dtb_ref, aneg_ref, sf_ref, sr_ref,
                    dskip_ref, norm_ref, o_ref):
    x, b_rep, c_rep, dt, cs, _ = _ssd_prep(nc, cur_ref, prev_ref, next_ref, cw_ref, cb_ref, dtb_ref, aneg_ref)
    z = cur_ref[0, :, 768:1024]
    lane_head = _iota((1, W_BR), 1) // HEAD
    lane16 = _iota((1, 16), 1)
    cs2 = jnp.concatenate([cs, -cs], axis=1)
    y = dskip_ref[...] * x
    for d, s_ref in enumerate((sf_ref, sr_ref)):
        rev = d == 1
        incl = _tri_mask(CH, rev, False)
        xs = x * _expand_heads(dt, 4 * d)
        y = y + dot1(c_rep * _expand_heads(jnp.exp(cs), 4 * d), _expand_blockdiag(s_ref[0, 0], HEAD, 4))
        for h in range(4):
            col = 4 * d + h
            lhs = jnp.where(lane16 == col, cs2, jnp.where(lane16 == 8 + col, 1.0, 0.0))
            rhs = jnp.where(lane16 == col, 1.0, jnp.where(lane16 == 8 + col, cs2, 0.0))
            l1, l2, l3 = _split3(lhs)
            r1, r2, r3 = _split3(rhs)
            diff = (_dg(l1, r1, "nt") + _dg(l2, r1, "nt")) + (_dg(l3, r1, "nt") + _dg(l1, r2, "nt")) + _dg(l1, r3, "nt")
            seg = jnp.exp(jnp.where(incl, diff, -1e30))
            mh = lane_head == h
            g = dot1(jnp.where(mh, c_rep, 0.0), b_rep, "nt")
            y = y + dot1(g * seg, jnp.where(mh, xs, 0.0))
    y = y * _silu(z)
    y = y * lax.rsqrt(jnp.mean(y * y, axis=-1, keepdims=True) + EPS) * norm_ref[...]
    o_ref[0] = y


def _ssd_out(f_ssd, sf, sr, p):
    b, L, _ = f_ssd.shape
    nc = L // CH
    prev, nxt = _ssd_halo_specs(nc)
    ent_spec = pl.BlockSpec((1, 1, HEAD, W_BR), lambda i, c: (i, c, 0, 0))
    return pl.pallas_call(
        functools.partial(_ssd_out_kernel, nc),
        grid=(b, nc),
        in_specs=[pl.BlockSpec((1, CH, W_SSD), lambda i, c: (i, c, 0)), prev, nxt,
                  _full((8, 768)), _full((1, 768)), _full((1, 8)), _full((1, 8)), ent_spec, ent_spec,
                  _full((1, W_BR)), _full((1, W_BR))],
        out_specs=pl.BlockSpec((1, CH, W_BR), lambda i, c: (i, c, 0)),
        out_shape=jax.ShapeDtypeStruct((b, L, W_BR), F32),
        compiler_params=_params(("parallel", "parallel")),
        name="ssd_out",
    )(f_ssd, f_ssd, f_ssd, *_ssd_param_args(p), sf, sr,
      jnp.repeat(p["ssd_d"], HEAD).reshape(1, W_BR), p["ssd_norm"].reshape(1, W_BR))


def _s5_weights(p):
    t = S5_T
    steps = jnp.arange(t + 1, dtype=F32)
    kmats, fmats, emats, lam_re, lam_im = [], [], [], [], []
    tt = jnp.arange(t)
    tau = tt[None, :] - tt[:, None]
    for d in range(2):
        lam = lax.complex(p["s5_a_re"][d], p["s5_a_im"][d])
        dt = jnp.exp(p["s5_log_dt"][d])[:, None]
        lam_bar = jnp.exp(lam * dt)
        b_bar = ((lam_bar - 1.0) / lam)[..., None] * lax.complex(p["s5_b_re"][d], p["s5_b_im"][d])
        c_c = lax.complex(p["s5_c_re"][d], p["s5_c_im"][d])
        pw = jnp.exp((lam * dt)[None] * steps[:, None, None].astype(jnp.complex64))
        kd = jnp.real(jnp.einsum("gcp,jgp,gpe->jgce", c_c, pw[:t], b_bar))
        lag = -tau if d else tau
        kfull = jnp.where((lag >= 0)[:, :, None, None, None], kd[jnp.clip(lag, 0, t - 1)], 0.0)
        kmats.append(jnp.transpose(kfull, (2, 0, 4, 1, 3)).reshape(S5_G, t * S5_CH, t * S5_CH))
        fpow = pw[:t] if d else pw[:t][::-1]
        fc = fpow[:, :, :, None] * b_bar[None]
        fc = jnp.transpose(fc, (1, 0, 3, 2)).reshape(S5_G, t * S5_CH, S5_P)
        fmats += [jnp.real(fc), jnp.imag(fc)]
        epow = pw[1:][::-1] if d else pw[1:]
        ec = c_c[None] * epow[:, :, None, :]
        ec = jnp.transpose(ec, (1, 3, 0, 2)).reshape(S5_G, S5_P, t * S5_CH)
        emats += [jnp.real(ec), -jnp.imag(ec)]
        lam_re += [jnp.real(pw[t]), jnp.real(pw[t])]
        lam_im += [-jnp.imag(pw[t]), jnp.imag(pw[t])]
    kmat = kmats[0] + kmats[1]
    fmat = jnp.concatenate(fmats, axis=2)
    emat = jnp.concatenate(emats, axis=1)
    return (kmat.astype(BF16), fmat.astype(BF16), emat.astype(BF16),
            jnp.concatenate(lam_re, axis=1), jnp.concatenate(lam_im, axis=1))


def _s5_local_kernel(u_ref, f_ref, z_ref):
    z_ref[0, 0] = _dg(_bf(u_ref[0, 0]), f_ref[0], "nn")


def _s5_local(ug, fmat):
    b, g, nc, w = ug.shape
    return pl.pallas_call(
        _s5_local_kernel,
        grid=(b, g),
        in_specs=[pl.BlockSpec((1, 1, nc, w), lambda i, j: (i, j, 0, 0)),
                  pl.BlockSpec((1, w, w), lambda i, j: (j, 0, 0))],
        out_specs=pl.BlockSpec((1, 1, nc, w), lambda i, j: (i, j, 0, 0)),
        out_shape=jax.ShapeDtypeStruct((b, g, nc, w), F32),
        compiler_params=_params(("parallel", "parallel")),
        name="s5_local",
    )(ug, fmat)


def _s5_scan_kernel(nblk, zf_ref, zr_ref, lre_ref, lim_ref, init_ref, xf_ref, xr_ref, fin_ref, x_scr):
    ci = pl.program_id(1)

    @pl.when(ci == 0)
    def _():
        x_scr[...] = init_ref[0]

    lane = _iota((1, 256), 1)
    is_fwd = lane < 128
    even_q = (lane // S5_P) % 2 == 0
    lre, lim = lre_ref[...], lim_ref[...]
    x = x_scr[...]
    for i in range(8):
        xf_ref[0, :, i, :] = x
        xr_ref[0, :, 7 - i, :] = x
        z = jnp.where(is_fwd, zf_ref[0, :, i, :], zr_ref[0, :, 7 - i, :])
        swapped = jnp.where(even_q, pltpu.roll(x, 192, 1), pltpu.roll(x, 64, 1))
        x = lre * x + lim * swapped + z
    x_scr[...] = x

    @pl.when(ci == nblk - 1)
    def _():
        fin_ref[0] = x


def _s5_scan(z, lam_re, lam_im, init):
    b, g, nc, w = z.shape
    nblk = nc // 8
    fwd = pl.BlockSpec((1, g, 8, w), lambda i, j: (i, 0, j, 0))
    rev = pl.BlockSpec((1, g, 8, w), lambda i, j: (i, 0, nblk - 1 - j, 0))
    ent = jax.ShapeDtypeStruct((b, g, nc, w), F32)
    return pl.pallas_call(
        functools.partial(_s5_scan_kernel, nblk),
        grid=(b, nblk),
        in_specs=[fwd, rev, _full((g, w)), _full((g, w)), pl.BlockSpec((1, g, w), lambda i, j: (i, 0, 0))],
        out_specs=[fwd, rev, pl.BlockSpec((1, g, w), lambda i, j: (i, 0, 0))],
        out_shape=[ent, ent, jax.ShapeDtypeStruct((b, g, w), F32)],
        scratch_shapes=[pltpu.VMEM((g, w), F32)],
        compiler_params=_params(("parallel", "arbitrary")),
        name="s5_scan",
    )(z, z, lam_re, lam_im, init)


def _s5_out_kernel(u_ref, xf_ref, xr_ref, k_ref, e_ref, y_ref):
    lane = _iota((1, 256), 1)
    x_ent = jnp.where(lane < 128, xf_ref[0, 0], xr_ref[0, 0])
    y_ref[0, 0] = _dg(_bf(u_ref[0, 0]), k_ref[0], "nn") + dot3(x_ent, e_ref[0].astype(F32))


def _s5_out(ug, xf, xr, kmat, emat):
    b, g, nc, w = ug.shape
    row = pl.BlockSpec((1, 1, nc, w), lambda i, j: (i, j, 0, 0))
    mat = pl.BlockSpec((1, w, w), lambda i, j: (j, 0, 0))
    return pl.pallas_call(
        _s5_out_kernel,
        grid=(b, g),
        in_specs=[row, row, row, mat, mat],
        out_specs=row,
        out_shape=jax.ShapeDtypeStruct((b, g, nc, w), F32),
        compiler_params=_params(("parallel", "parallel")),
        name="s5_out",
    )(ug, xf, xr, kmat, emat)


def _outproj_kernel(h_ref, mod_ref, yrw_ref, ys5_ref, fs5_ref, yssd_ref, ygla_ref, s5d_ref, gluw_ref, glub_ref,
                    w_ref, g_ref, o_ref):
    d = h_ref.shape[2]
    u = fs5_ref[0, :, 0:256]
    y = ys5_ref[0] + s5d_ref[...] * u
    y = 0.5 * y * (1.0 + jnp.tanh(math.sqrt(2.0 / math.pi) * (y + 0.044715 * (y * y * y))))
    y = y * _sigmoid(dot1(y, gluw_ref[...]) + glub_ref[...])
    y_s5 = y * _silu(fs5_ref[0, :, 256:512])
    ycat = jnp.concatenate([yrw_ref[0], y_s5, yssd_ref[0], ygla_ref[0]], axis=1)
    o = dot1(ycat, w_ref[...])
    o = o * lax.rsqrt(jnp.mean(o * o, axis=-1, keepdims=True) + EPS) * g_ref[...]
    o_ref[0] = h_ref[0] + mod_ref[0][:, 2 * d:3 * d] * o


def _outproj(h, mod3, mod_row, y_rw, y_s5, f_s5, y_ssd, y_gla, p, tm):
    b, L, d = h.shape
    row = lambda w: pl.BlockSpec((1, tm, w), lambda i, j: (i, j, 0))
    return pl.pallas_call(
        _outproj_kernel,
        grid=(b, L // tm),
        in_specs=[row(d), pl.BlockSpec((1, 1, 3 * d), lambda i, j: (mod_row(i), 0, 0)),
                  row(W_BR), row(W_BR), row(W_S5), row(W_BR), row(W_BR),
                  _full((1, W_BR)), _full((W_BR, W_BR)), _full((1, W_BR)), _full((d, d)), _full((1, d))],
        out_specs=row(d),
        out_shape=jax.ShapeDtypeStruct((b, L, d), F32),
        compiler_params=_params(("parallel", "parallel")),
        name="out_proj",
    )(h, mod3, y_rw, y_s5, f_s5, y_ssd, y_gla, p["s5_d"].reshape(1, W_BR), p["s5_glu_w"].astype(BF16),
      p["s5_glu_b"].reshape(1, W_BR), p["w_out"].astype(BF16), p["norm_post"].reshape(1, d))


def _permute_w_in(w_in):
    rw = w_in[:, 0:1280]
    s5 = w_in[:, 1280:1792]
    o = 1792
    x, bm, cm = w_in[:, o:o + 256], w_in[:, o + 256:o + 384], w_in[:, o + 384:o + 512]
    dt, z = w_in[:, o + 512:o + 520], w_in[:, o + 520:o + 776]
    rep = lambda m: jnp.concatenate([m[:, 0:64], m[:, 0:64], m[:, 64:128], m[:, 64:128]], axis=1)
    pad = lambda n: jnp.zeros((w_in.shape[0], n), w_in.dtype)
    ssd = jnp.concatenate([x, rep(bm), rep(cm), z, dt, pad(120)], axis=1)
    o = 1792 + 776
    q, k, v = w_in[:, o:o + 128], w_in[:, o + 128:o + 256], w_in[:, o + 256:o + 512]
    gl, gate = w_in[:, o + 512:o + 544], w_in[:, o + 544:o + 800]
    gla = jnp.concatenate([q, k, v, gate, gl, pad(96)], axis=1)
    return jnp.concatenate([rw, s5, ssd, gla], axis=1).astype(BF16)


def _layer_params(l, a):
    p = {k: v[l] for k, v in a.items()}
    rep = lambda m: jnp.concatenate([m[..., 0:64], m[..., 0:64], m[..., 64:128], m[..., 64:128]], axis=-1)
    cw, cb = p["ssd_conv_w"], p["ssd_conv_b"]
    p["conv_w_rep"] = jnp.concatenate([cw[:, 0:256], rep(cw[:, 256:384]), rep(cw[:, 384:512])], axis=1)
    p["conv_b_rep"] = jnp.concatenate([cb[0:256], rep(cb[256:384]), rep(cb[384:512])], axis=0)
    p["ssd_a_neg"] = -jnp.exp(p["ssd_a_log"])
    p["w_in_p"] = _permute_w_in(p["w_in"])
    p["s5_ops"] = _s5_weights(p)
    return p


def _mixer_layer(h, mod3, mod_row, p, is_grid, init, need_out):
    b, L, d = h.shape
    tm = min(256, L)
    f_rw, f_s5, f_ssd, f_gla = _inproj(h, mod3, mod_row, p["norm_pre"], p["w_in_p"], tm)
    if init is None:
        init = (jnp.zeros((b, 2, HEAD, W_BR), F32), jnp.zeros((b, S5_G, 256), F32),
                jnp.zeros((b, 2, HEAD, W_BR), F32), jnp.zeros((b, 2, GLA_DK, W_BR), F32))
    rt, y0, bonus, m_rw, c_rw = _rw_local(f_rw, p, is_grid)
    rw_sf, rw_sr, rw_fin = _rw_scan(m_rw, c_rw, init[0])
    kmat, fmat, emat, lam_re, lam_im = p["s5_ops"]
    nc5 = L // S5_T
    ug = f_s5[:, :, 0:256].reshape(b, nc5, S5_T, S5_G, S5_CH).transpose(0, 3, 1, 2, 4).reshape(b, S5_G, nc5, 256)
    z5 = _s5_local(ug, fmat)
    s5_xf, s5_xr, s5_fin = _s5_scan(z5, lam_re, lam_im, init[1])
    ssd_c, ssd_d = _ssd_local(f_ssd, p)
    ssd_sf, ssd_sr, ssd_fin = _ew_scan(ssd_c, ssd_d, init[2], "ssd_scan")
    gla_c, gla_d = _gla_local(f_gla, p)
    gla_sf, gla_sr, gla_fin = _ew_scan(gla_c, gla_d, init[3], "gla_scan")
    states = (rw_fin, s5_fin, ssd_fin, gla_fin)
    if not need_out:
        return None, states
    y_rw = _rw_out(y0, bonus, rt, rw_sf, rw_sr, f_rw, p)
    y5 = _s5_out(ug, s5_xf, s5_xr, kmat, emat)
    y5 = y5.reshape(b, S5_G, nc5, S5_T, S5_CH).transpose(0, 2, 3, 1, 4).reshape(b, L, W_BR)
    y_ssd = _ssd_out(f_ssd, ssd_sf, ssd_sr, p)
    y_gla = _gla_out(f_gla, gla_sf, gla_sr, p)
    return _outproj(h, mod3, mod_row, y_rw, y5, f_s5, y_ssd, y_gla, p, tm), states


def kernel(x, c, ctx, c_ctx, ada_w, ada_b, norm_pre, norm_post, w_in, w_out, rw_mu, rw_w0, rw_w2, rw_a0, rw_a2, rw_kk, rw_ka, rw_rk, rw_ln_w, rw_ln_b, s5_a_re, s5_a_im, s5_log_dt, s5_b_re, s5_b_im, s5_c_re, s5_c_im, s5_d, s5_glu_w, s5_glu_b, ssd_conv_w, ssd_conv_b, ssd_dt_bias, ssd_a_log, ssd_d, ssd_norm, gla_g2, gla_gb, gla_norm):
    stacked = dict(norm_pre=norm_pre, norm_post=norm_post, w_in=w_in, w_out=w_out, rw_mu=rw_mu, rw_w0=rw_w0,
                   rw_w2=rw_w2, rw_a0=rw_a0, rw_a2=rw_a2, rw_kk=rw_kk, rw_ka=rw_ka, rw_rk=rw_rk, rw_ln_w=rw_ln_w,
                   rw_ln_b=rw_ln_b, s5_a_re=s5_a_re, s5_a_im=s5_a_im, s5_log_dt=s5_log_dt, s5_b_re=s5_b_re,
                   s5_b_im=s5_b_im, s5_c_re=s5_c_re, s5_c_im=s5_c_im, s5_d=s5_d, s5_glu_w=s5_glu_w,
                   s5_glu_b=s5_glu_b, ssd_conv_w=ssd_conv_w, ssd_conv_b=ssd_conv_b, ssd_dt_bias=ssd_dt_bias,
                   ssd_a_log=ssd_a_log, ssd_d=ssd_d, ssd_norm=ssd_norm, gla_g2=gla_g2, gla_gb=gla_gb,
                   gla_norm=gla_norm)
    depth = ada_w.shape[0]
    b, d = c.shape
    rows = -(-(b + 1) // 8) * 8
    cvec = jnp.concatenate([c, c_ctx[None, :], jnp.zeros((rows - b - 1, d), F32)], axis=0)
    mod = _modulation(cvec, ada_w, ada_b)
    h, hc = x, ctx
    for l in range(depth):
        p = _layer_params(l, stacked)
        mod3 = mod[l].reshape(rows, 1, 3 * d)
        last = l == depth - 1
        hc_next, ctx_states = _mixer_layer(hc, mod3, lambda i: b, p, False, None, not last)
        h, _ = _mixer_layer(h, mod3, lambda i: i, p, True, ctx_states, True)
        hc = hc_next
    return h
```

```python
import functools
import math

import jax
import jax.numpy as jnp
from jax import lax
from jax.experimental import pallas as pl
from jax.experimental.pallas import tpu as pltpu

F32 = jnp.float32
BF16 = jnp.bfloat16

EPS = 1e-6
GRID_W = 64
W_BR = 256
CH = 64
HEAD = 64
RW_GN_EPS = 64e-5
RW_LORA = 64
S5_G, S5_CH, S5_P, S5_T = 16, 16, 64, 16
SSD_CONV = 5
GLA_RANK = 16
GLA_DK = 32
GLA_TAU = 16.0
W_RW, W_S5, W_SSD, W_GLA = 1280, 512, 1152, 896
N_PROJ = W_RW + W_S5 + W_SSD + W_GLA
VMEM_LIMIT = 48 * 1024 * 1024


def _dg(a, b, mode):
    ca, cb = {"nn": (1, 0), "nt": (1, 1), "tn": (0, 0)}[mode]
    return lax.dot_general(a, b, (((ca,), (cb,)), ((), ())), preferred_element_type=F32)


def _bf(a):
    return a.astype(BF16)


def _split2(a):
    hi = _bf(a)
    return hi, _bf(a - hi.astype(F32))


def _split3(a):
    hi = _bf(a)
    r = a - hi.astype(F32)
    mid = _bf(r)
    return hi, mid, _bf(r - mid.astype(F32))


def dot1(a, b, mode="nn"):
    return _dg(_bf(a), _bf(b), mode)


def dot3(a, b, mode="nn"):
    ah, al = _split2(a)
    bh, bl = _split2(b)
    return _dg(ah, bh, mode) + (_dg(ah, bl, mode) + _dg(al, bh, mode))


def dot_xl(a_exact, b, mode="nn"):
    b1, b2, b3 = _split3(b)
    return _dg(a_exact, b1, mode) + (_dg(a_exact, b2, mode) + _dg(a_exact, b3, mode))


def dot_xr(a, b_exact, mode="nn"):
    a1, a2, a3 = _split3(a)
    return _dg(a1, b_exact, mode) + (_dg(a2, b_exact, mode) + _dg(a3, b_exact, mode))


def _sigmoid(x):
    return 1.0 / (1.0 + jnp.exp(-x))


def _silu(x):
    return x * _sigmoid(x)


def _softplus(x):
    return jnp.maximum(x, 0.0) + jnp.log(1.0 + jnp.exp(-jnp.abs(x)))


def _iota(shape, dim):
    return lax.broadcasted_iota(jnp.int32, shape, dim)


def _tri_mask(n, rev, strict):
    r, c = _iota((n, n), 0), _iota((n, n), 1)
    if rev:
        return (r < c) if strict else (r <= c)
    return (r > c) if strict else (r >= c)


def _block_ones(n, blk):
    return (_iota((n, n), 0) // blk == _iota((n, n), 1) // blk).astype(BF16)


def _expand_blockdiag(compact, rows_per_head, n_heads):
    lane_head = _iota((1, compact.shape[1]), 1) // HEAD
    return jnp.concatenate([jnp.where(lane_head == h, compact, 0.0) for h in range(n_heads)], axis=0)


def _compact_blockdiag(blk, rows_per_head, n_heads):
    lane_head = _iota((1, blk.shape[1]), 1) // HEAD
    out = jnp.where(lane_head == 0, blk[0:rows_per_head], 0.0)
    for h in range(1, n_heads):
        out = out + jnp.where(lane_head == h, blk[h * rows_per_head:(h + 1) * rows_per_head], 0.0)
    return out


def _expand_heads(cols, first, n_heads=4):
    lane_head = _iota((1, n_heads * HEAD), 1) // HEAD
    out = jnp.where(lane_head == 0, cols[:, first:first + 1], 0.0)
    for h in range(1, n_heads):
        out = jnp.where(lane_head == h, cols[:, first + h:first + h + 1], out)
    return out


def _shift_rows(x, first_row, last_row, down):
    n = x.shape[0]
    rows = _iota((n, 1), 0)
    if down:
        return jnp.where(rows == 0, first_row, pltpu.roll(x, 1, 0))
    return jnp.where(rows == n - 1, last_row, pltpu.roll(x, n - 1, 0))


def _params(sem):
    return pltpu.CompilerParams(dimension_semantics=sem, vmem_limit_bytes=VMEM_LIMIT)


def _full(shape):
    nd = len(shape)
    return pl.BlockSpec(shape, lambda *_: (0,) * nd)


def _mod_kernel(c_ref, w_ref, b_ref, o_ref):
    s = _silu(c_ref[...])
    o_ref[0] = dot3(s, w_ref[0]) + b_ref[0]


def _modulation(cvec, ada_w, ada_b):
    depth, d, n = ada_w.shape
    rows = cvec.shape[0]
    tn = 1024
    return pl.pallas_call(
        _mod_kernel,
        grid=(depth, n // tn),
        in_specs=[pl.BlockSpec((rows, d), lambda l, j: (0, 0)),
                  pl.BlockSpec((1, d, tn), lambda l, j: (l, 0, j)),
                  pl.BlockSpec((1, 1, tn), lambda l, j: (l, 0, j))],
        out_specs=pl.BlockSpec((1, rows, tn), lambda l, j: (l, 0, j)),
        out_shape=jax.ShapeDtypeStruct((depth, rows, n), F32),
        compiler_params=_params(("parallel", "parallel")),
        name="adaln_mod",
    )(cvec, ada_w, ada_b.reshape(depth, 1, n))


def _inproj_kernel(x_ref, mod_ref, g_ref, w_ref, o_rw, o_s5, o_ssd, o_gla):
    x = x_ref[0]
    d = x.shape[1]
    hn = x * lax.rsqrt(jnp.mean(x * x, axis=-1, keepdims=True) + EPS) * g_ref[...]
    m = mod_ref[0]
    hn = hn * (1.0 + m[:, d:2 * d]) + m[:, 0:d]
    p = dot1(hn, w_ref[...])
    o_rw[0] = p[:, 0:W_RW]
    o_s5[0] = p[:, W_RW:W_RW + W_S5]
    o_ssd[0] = p[:, W_RW + W_S5:W_RW + W_S5 + W_SSD]
    o_gla[0] = p[:, W_RW + W_S5 + W_SSD:N_PROJ]


def _inproj(h, mod3, mod_row, norm_pre, w_in_p, tm):
    b, L, d = h.shape
    widths = (W_RW, W_S5, W_SSD, W_GLA)
    return pl.pallas_call(
        _inproj_kernel,
        grid=(b, L // tm),
        in_specs=[pl.BlockSpec((1, tm, d), lambda i, j: (i, j, 0)),
                  pl.BlockSpec((1, 1, 3 * d), lambda i, j: (mod_row(i), 0, 0)),
                  _full((1, d)),
                  _full(w_in_p.shape)],
        out_specs=[pl.BlockSpec((1, tm, w), lambda i, j: (i, j, 0)) for w in widths],
        out_shape=[jax.ShapeDtypeStruct((b, L, w), F32) for w in widths],
        compiler_params=_params(("parallel", "parallel")),
        name="in_proj",
    )(h, mod3, norm_pre.reshape(1, d), w_in_p)


def _hilo(a):
    hi = _bf(a).astype(F32)
    return hi, a - hi


def dotk3(a, b, mode="nn"):
    ah, al = _hilo(a)
    bh, bl = _hilo(b)
    ax = 0 if mode == "tn" else 1
    bx = 1 if mode == "nt" else 0
    return _dg(_bf(jnp.concatenate([ah, al, ah], axis=ax)), _bf(jnp.concatenate([bh, bh, bl], axis=bx)), mode)


def _tri_inv_many(n_mats):
    r, c = _iota((CH, CH), 0), _iota((CH, CH), 1)
    blk16 = (r // 16) == (c // 16)
    blk32 = (r // 32) == (c // 32)
    eye = jnp.where(r == c, 1.0, 0.0)
    nd = [jnp.where(blk16, n, 0.0) for n in n_mats]
    d = [eye + x for x in nd]
    p = [dotk3(x, x) for x in nd]
    for _ in range(2):
        d = [x + dotk3(x, y) for x, y in zip(d, p)]
        p = [dotk3(y, y) for y in p]
    d = [x + dotk3(x, y) for x, y in zip(d, p)]
    for sel in (blk32 & jnp.logical_not(blk16), jnp.logical_not(blk32)):
        od = [dotk3(jnp.where(sel, n, 0.0), x) for n, x in zip(n_mats, d)]
        d = [x + dotk3(x, y) for x, y in zip(d, od)]
    return d


def _rw_local_kernel(is_grid, nc, cur_ref, prev_ref, next_ref, mu_ref, w0_ref, w2_ref, a0_ref, a2_ref,
                     kk_ref, ka_ref, rk_ref, rt_ref, y0_ref, bonus_ref, m_ref, c_ref):
    ci = pl.program_id(1)
    has_prev = (ci > 0).astype(F32)
    has_next = (ci < nc - 1).astype(F32)
    z = cur_ref[0, :, 0:1024]
    if is_grid:
        zero_row = jnp.zeros((1, W_BR), F32)
        left = _shift_rows(z[:, 0:256], zero_row, zero_row, True)
        right = _shift_rows(z[:, 256:512], zero_row, zero_row, False)
        up = prev_ref[0] * has_prev
        down = next_ref[0] * has_next
        sh = jnp.concatenate([left, right, up, down], axis=1)
    else:
        prow = prev_ref[0][7:8, :] * has_prev
        nrow = next_ref[0][0:1, :] * has_next
        sh = jnp.concatenate([_shift_rows(z[:, 0:512], prow, prow, True),
                              _shift_rows(z[:, 512:1024], nrow, nrow, False)], axis=1)
    zs = z + mu_ref[...] * (sh - z)
    r, k, v = zs[:, 0:256], zs[:, 256:512], zs[:, 512:768]
    wl, al = zs[:, 768:896], zs[:, 896:1024]

    bo = _block_ones(W_BR, HEAD)
    lane_head = _iota((1, W_BR), 1) // HEAD
    kk = k * kk_ref[...]
    kk = kk * lax.rsqrt(dot_xr(kk * kk, bo) + EPS)
    eye = _iota((HEAD, HEAD), 0) == _iota((HEAD, HEAD), 1)
    heads = lambda t: [t[:, HEAD * h:HEAD * (h + 1)] for h in range(4)]
    v_h = heads(v)

    bonus = jnp.zeros((CH, W_BR), F32)
    pa_l, pam_l, prm_l, pr_l, nb_l, nk_l, eb_l, ek_l, dg_l, strict_l, incl_l = ([] for _ in range(11))
    for d in range(2):
        rev = d == 1
        w_pre = w0_ref[d:d + 1, :] + dot3(jnp.tanh(wl[:, RW_LORA * d:RW_LORA * (d + 1)]), w2_ref[d])
        lw = (-math.exp(-0.5)) * _sigmoid(w_pre)
        a = _sigmoid(a0_ref[d:d + 1, :] + dot3(al[:, RW_LORA * d:RW_LORA * (d + 1)], a2_ref[d]))
        k_d = k * (1.0 + (a - 1.0) * ka_ref[...])
        bonus = bonus + dot_xr(r * k_d * rk_ref[...], bo) * v

        cs = dot_xl(_tri_mask(CH, rev, False).astype(BF16), lw)
        cs_ex = cs - lw
        cs_tot = cs[0:1] if rev else cs[CH - 1:CH]
        cs_mid = cs[CH // 2:CH // 2 + 1]
        kka = kk * a
        e_mid = jnp.exp(cs_mid - cs)
        e_tot = jnp.exp(cs_tot - cs)
        pa_l += heads(-kk * jnp.exp(cs_ex))
        pr_l += heads(r * jnp.exp(cs))
        pam_l += heads(-kk * jnp.exp(cs_ex - cs_mid))
        prm_l += heads(r * jnp.exp(cs - cs_mid))
        nb_l += heads(kka * e_mid)
        nk_l += heads(k_d * e_mid)
        eb_l += heads(kka * e_tot)
        ek_l += heads(k_d * e_tot)
        dg_l += [jnp.where(eye, t, 0.0) for t in heads(jnp.exp(cs_tot))]
        strict_l += [_tri_mask(CH, rev, True)] * 4
        incl_l += [_tri_mask(CH, rev, False)] * 4
    vv = v_h + v_h
    idx = range(8)

    n_l = [jnp.where(strict_l[i], dotk3(pam_l[i], nb_l[i], "nt"), 0.0) for i in idx]
    g1 = [dot1(jnp.concatenate([pam_l[i], prm_l[i]], axis=0), nk_l[i], "nt") for i in idx]
    aak = [jnp.where(strict_l[i], g1[i][0:CH], 0.0) for i in idx]
    gk = [jnp.where(incl_l[i], g1[i][CH:2 * CH], 0.0) for i in idx]
    gb = [jnp.where(incl_l[i], dot1(prm_l[i], nb_l[i], "nt"), 0.0) for i in idx]
    t_inv = _tri_inv_many(n_l)
    av = [dot1(aak[i], vv[i]) for i in idx]
    wu = [dotk3(t_inv[i], jnp.concatenate([pa_l[i], av[i]], axis=1)) for i in idx]
    gwu = [dot1(gb[i], wu[i]) for i in idx]
    gkv = [dot1(gk[i], vv[i]) for i in idx]
    mc = [dotk3(eb_l[i], wu[i], "tn") for i in idx]
    ekv = [dot1(ek_l[i], vv[i], "tn") for i in idx]
    y0_parts = []
    for d in range(2):
        sl = range(4 * d, 4 * d + 4)
        rt_ref[0, 0, d] = jnp.concatenate([pr_l[i] + gwu[i][:, 0:HEAD] for i in sl], axis=1)
        m_ref[0, 0, d] = jnp.concatenate([dg_l[i] + mc[i][:, 0:HEAD] for i in sl], axis=1)
        c_ref[0, 0, d] = jnp.concatenate([mc[i][:, HEAD:2 * HEAD] + ekv[i] for i in sl], axis=1)
        y0_parts.append(jnp.concatenate([gwu[i][:, HEAD:2 * HEAD] + gkv[i] for i in sl], axis=1))
    y0_ref[0] = y0_parts[0] + y0_parts[1]
    bonus_ref[0] = bonus


def _halo_specs(is_grid, nc, width_cols):
    if is_grid:
        prev = pl.BlockSpec((1, CH, W_BR), lambda b, c: (b, jnp.maximum(c - 1, 0), 2))
        nxt = pl.BlockSpec((1, CH, W_BR), lambda b, c: (b, jnp.minimum(c + 1, nc - 1), 3))
    else:
        rb = CH // 8
        prev = pl.BlockSpec((1, 8, width_cols), lambda b, c: (b, jnp.maximum(c * rb - 1, 0), 0))
        nxt = pl.BlockSpec((1, 8, width_cols), lambda b, c: (b, jnp.minimum((c + 1) * rb, nc * rb - 1), 1))
    return prev, nxt


def _rw_local(f_rw, p, is_grid):
    b, L, _ = f_rw.shape
    nc = L // CH
    prev, nxt = _halo_specs(is_grid, nc, 512)
    state_shape = jax.ShapeDtypeStruct((b, nc, 2, HEAD, W_BR), F32)
    state_spec = pl.BlockSpec((1, 1, 2, HEAD, W_BR), lambda i, c: (i, c, 0, 0, 0))
    row_spec = pl.BlockSpec((1, CH, W_BR), lambda i, c: (i, c, 0))
    row_shape = jax.ShapeDtypeStruct((b, L, W_BR), F32)
    return pl.pallas_call(
        functools.partial(_rw_local_kernel, is_grid, nc),
        grid=(b, nc),
        in_specs=[pl.BlockSpec((1, CH, W_RW), lambda i, c: (i, c, 0)), prev, nxt,
                  _full((1, 1024)), _full((2, W_BR)), _full((2, RW_LORA, W_BR)), _full((2, W_BR)),
                  _full((2, RW_LORA, W_BR)), _full((1, W_BR)), _full((1, W_BR)), _full((1, W_BR))],
        out_specs=[state_spec, row_spec, row_spec, state_spec, state_spec],
        out_shape=[state_shape, row_shape, row_shape, state_shape, state_shape],
        compiler_params=_params(("parallel", "parallel")),
        name="rwkv_local",
    )(f_rw, f_rw, f_rw, p["rw_mu"].reshape(1, 1024), p["rw_w0"], p["rw_w2"], p["rw_a0"], p["rw_a2"],
      p["rw_kk"].reshape(1, W_BR), p["rw_ka"].reshape(1, W_BR), p["rw_rk"].reshape(1, W_BR))


def _rw_scan_kernel(nc, mf_ref, mr_ref, cf_ref, cr_ref, init_ref, sf_ref, sr_ref, fin_ref, s_scr):
    ci = pl.program_id(1)

    @pl.when(ci == 0)
    def _():
        s_scr[0] = _expand_blockdiag(init_ref[0, 0], HEAD, 4)
        s_scr[1] = _expand_blockdiag(init_ref[0, 1], HEAD, 4)

    for d, (m_ref, c_ref, o_ref) in enumerate(((mf_ref, cf_ref, sf_ref), (mr_ref, cr_ref, sr_ref))):
        s = s_scr[d]
        o_ref[0, 0] = _compact_blockdiag(s, HEAD, 4)
        m_blk = _expand_blockdiag(m_ref[0, 0, 0], HEAD, 4)
        c_blk = _expand_blockdiag(c_ref[0, 0, 0], HEAD, 4)
        s_scr[d] = dot3(m_blk, s) + c_blk

    @pl.when(ci == nc - 1)
    def _():
        fin_ref[0, 0] = _compact_blockdiag(s_scr[0], HEAD, 4)
        fin_ref[0, 1] = _compact_blockdiag(s_scr[1], HEAD, 4)


def _rw_scan(m, c, init):
    b, nc = m.shape[0], m.shape[1]
    fwd = pl.BlockSpec((1, 1, 1, HEAD, W_BR), lambda i, j: (i, j, 0, 0, 0))
    rev = pl.BlockSpec((1, 1, 1, HEAD, W_BR), lambda i, j: (i, nc - 1 - j, 1, 0, 0))
    ent_shape = jax.ShapeDtypeStruct((b, nc, HEAD, W_BR), F32)
    return pl.pallas_call(
        functools.partial(_rw_scan_kernel, nc),
        grid=(b, nc),
        in_specs=[fwd, rev, fwd, rev, pl.BlockSpec((1, 2, HEAD, W_BR), lambda i, j: (i, 0, 0, 0))],
        out_specs=[pl.BlockSpec((1, 1, HEAD, W_BR), lambda i, j: (i, j, 0, 0)),
                   pl.BlockSpec((1, 1, HEAD, W_BR), lambda i, j: (i, nc - 1 - j, 0, 0)),
                   pl.BlockSpec((1, 2, HEAD, W_BR), lambda i, j: (i, 0, 0, 0))],
        out_shape=[ent_shape, ent_shape, jax.ShapeDtypeStruct((b, 2, HEAD, W_BR), F32)],
        scratch_shapes=[pltpu.VMEM((2, W_BR, W_BR), F32)],
        compiler_params=_params(("parallel", "arbitrary")),
        name="rwkv_scan",
    )(m, m, c, c, init)


def _rw_out_kernel(y0_ref, bonus_ref, rt_ref, sf_ref, sr_ref, gate_ref, lnw_ref, lnb_ref, o_ref):
    y = y0_ref[0]
    y = y + dot1(rt_ref[0, 0, 0], _expand_blockdiag(sf_ref[0, 0], HEAD, 4))
    y = y + dot1(rt_ref[0, 0, 1], _expand_blockdiag(sr_ref[0, 0], HEAD, 4))
    bo = _block_ones(W_BR, HEAD)
    mu = dot_xr(y, bo) * (1.0 / HEAD)
    yc = y - mu
    var = dot_xr(yc * yc, bo) * (1.0 / HEAD)
    yn = yc * lax.rsqrt(var + RW_GN_EPS) * lnw_ref[...] + lnb_ref[...]
    o_ref[0] = (yn + bonus_ref[0]) * _silu(gate_ref[0])


def _rw_out(y0, bonus, rt, sf, sr, f_rw, p):
    b, L, _ = y0.shape
    nc = L // CH
    row_spec = pl.BlockSpec((1, CH, W_BR), lambda i, c: (i, c, 0))
    ent_spec = pl.BlockSpec((1, 1, HEAD, W_BR), lambda i, c: (i, c, 0, 0))
    return pl.pallas_call(
        _rw_out_kernel,
        grid=(b, nc),
        in_specs=[row_spec, row_spec, pl.BlockSpec((1, 1, 2, HEAD, W_BR), lambda i, c: (i, c, 0, 0, 0)),
                  ent_spec, ent_spec, pl.BlockSpec((1, CH, W_BR), lambda i, c: (i, c, 4)),
                  _full((1, W_BR)), _full((1, W_BR))],
        out_specs=row_spec,
        out_shape=jax.ShapeDtypeStruct((b, L, W_BR), F32),
        compiler_params=_params(("parallel", "parallel")),
        name="rwkv_out",
    )(y0, bonus, rt, sf, sr, f_rw, p["rw_ln_w"].reshape(1, W_BR), p["rw_ln_b"].reshape(1, W_BR))


def _ew_scan_kernel(nc, cf_ref, cr_ref, df_ref, dr_ref, init_ref, sf_ref, sr_ref, fin_ref, s_scr):
    ci = pl.program_id(1)

    @pl.when(ci == 0)
    def _():
        s_scr[...] = init_ref[0]

    for d, (c_ref, d_ref, o_ref) in enumerate(((cf_ref, df_ref, sf_ref), (cr_ref, dr_ref, sr_ref))):
        s = s_scr[d]
        o_ref[0, 0] = s
        s_scr[d] = d_ref[0, 0, 0] * s + c_ref[0, 0, 0]

    @pl.when(ci == nc - 1)
    def _():
        fin_ref[0] = s_scr[...]


def _ew_scan(cst, dec, init, name):
    b, nc, _, rows, width = cst.shape
    fwd = pl.BlockSpec((1, 1, 1, rows, width), lambda i, j: (i, j, 0, 0, 0))
    rev = pl.BlockSpec((1, 1, 1, rows, width), lambda i, j: (i, nc - 1 - j, 1, 0, 0))
    ent_shape = jax.ShapeDtypeStruct((b, nc, rows, width), F32)
    return pl.pallas_call(
        functools.partial(_ew_scan_kernel, nc),
        grid=(b, nc),
        in_specs=[fwd, rev, fwd, rev, pl.BlockSpec((1, 2, rows, width), lambda i, j: (i, 0, 0, 0))],
        out_specs=[pl.BlockSpec((1, 1, rows, width), lambda i, j: (i, j, 0, 0)),
                   pl.BlockSpec((1, 1, rows, width), lambda i, j: (i, nc - 1 - j, 0, 0)),
                   pl.BlockSpec((1, 2, rows, width), lambda i, j: (i, 0, 0, 0))],
        out_shape=[ent_shape, ent_shape, jax.ShapeDtypeStruct((b, 2, rows, width), F32)],
        scratch_shapes=[pltpu.VMEM((2, rows, width), F32)],
        compiler_params=_params(("parallel", "arbitrary")),
        name=name,
    )(cst, cst, dec, dec, init)


def _gla_log_decay(gl, g2_ref, gb_ref, d):
    x = dot3(gl[:, GLA_RANK * d:GLA_RANK * (d + 1)], g2_ref[d]) + gb_ref[d:d + 1, :]
    return -_softplus(-x) * (1.0 / GLA_TAU)


def _gla_local_kernel(f_ref, g2_ref, gb_ref, cst_ref, dec_ref):
    k = f_ref[0, :, 128:256]
    v = f_ref[0, :, 256:512]
    gl = f_ref[0, :, 768:800]
    sel = (_iota((128, W_BR), 0) // GLA_DK) == (_iota((128, W_BR), 1) // HEAD)
    first_row = _iota((CH, 128), 0) == 0
    ones = jnp.ones((CH, W_BR), BF16)
    for d in range(2):
        rev = d == 1
        la = _gla_log_decay(gl, g2_ref, gb_ref, d)
        cs = dot_xl(_tri_mask(CH, rev, False).astype(BF16), la)
        cs_tot = cs[0:1] if rev else cs[CH - 1:CH]
        ke = k * jnp.exp(cs_tot - cs)
        c_blk = jnp.where(sel, dot1(ke, v, "tn"), 0.0)
        d_blk = dot_xr(jnp.where(first_row, jnp.exp(cs_tot), 0.0), ones, "tn")
        cst_ref[0, 0, d] = _compact_blockdiag(c_blk, GLA_DK, 4)
        dec_ref[0, 0, d] = _compact_blockdiag(d_blk, GLA_DK, 4)


def _gla_local(f_gla, p):
    b, L, _ = f_gla.shape
    nc = L // CH
    st_shape = jax.ShapeDtypeStruct((b, nc, 2, GLA_DK, W_BR), F32)
    st_spec = pl.BlockSpec((1, 1, 2, GLA_DK, W_BR), lambda i, c: (i, c, 0, 0, 0))
    return pl.pallas_call(
        _gla_local_kernel,
        grid=(b, nc),
        in_specs=[pl.BlockSpec((1, CH, W_GLA), lambda i, c: (i, c, 0)),
                  _full((2, GLA_RANK, 128)), _full((2, 128))],
        out_specs=[st_spec, st_spec],
        out_shape=[st_shape, st_shape],
        compiler_params=_params(("parallel", "parallel")),
        name="gla_local",
    )(f_gla, p["gla_g2"], p["gla_gb"])


def _gla_out_kernel(f_ref, sf_ref, sr_ref, g2_ref, gb_ref, norm_ref, o_ref):
    q = f_ref[0, :, 0:128] * (GLA_DK ** -0.5)
    k = f_ref[0, :, 128:256]
    v = f_ref[0, :, 256:512]
    gate = f_ref[0, :, 512:768]
    gl = f_ref[0, :, 768:800]
    khead = _iota((1, 128), 1) // GLA_DK
    vhead = _iota((1, W_BR), 1) // HEAD
    y = jnp.zeros((CH, W_BR), F32)
    scores = [None] * 4
    for d, s_ref in enumerate((sf_ref, sr_ref)):
        rev = d == 1
        la = _gla_log_decay(gl, g2_ref, gb_ref, d)
        cs = dot_xl(_tri_mask(CH, rev, False).astype(BF16), la)
        cs_mid = cs[CH // 2:CH // 2 + 1]
        qe = q * jnp.exp(cs - cs_mid)
        kn = k * jnp.exp(cs_mid - cs)
        incl = _tri_mask(CH, rev, False)
        for h in range(4):
            sc = jnp.where(incl, dot1(jnp.where(khead == h, qe, 0.0), kn, "nt"), 0.0)
            scores[h] = sc if scores[h] is None else scores[h] + sc
        y = y + dot1(q * jnp.exp(cs), _expand_blockdiag(s_ref[0, 0], GLA_DK, 4))
    for h in range(4):
        y = y + dot1(scores[h], jnp.where(vhead == h, v, 0.0))
    ms = dot_xr(y * y, _block_ones(W_BR, HEAD)) * (1.0 / HEAD)
    y = y * lax.rsqrt(ms + EPS) * norm_ref[...]
    o_ref[0] = y * _silu(gate)


def _gla_out(f_gla, sf, sr, p):
    b, L, _ = f_gla.shape
    nc = L // CH
    ent_spec = pl.BlockSpec((1, 1, GLA_DK, W_BR), lambda i, c: (i, c, 0, 0))
    return pl.pallas_call(
        _gla_out_kernel,
        grid=(b, nc),
        in_specs=[pl.BlockSpec((1, CH, W_GLA), lambda i, c: (i, c, 0)), ent_spec, ent_spec,
                  _full((2, GLA_RANK, 128)), _full((2, 128)), _full((1, W_BR))],
        out_specs=pl.BlockSpec((1, CH, W_BR), lambda i, c: (i, c, 0)),
        out_shape=jax.ShapeDtypeStruct((b, L, W_BR), F32),
        compiler_params=_params(("parallel", "parallel")),
        name="gla_out",
    )(f_gla, sf, sr, p["gla_g2"], p["gla_gb"], p["gla_norm"].reshape(1, W_BR))


def _ssd_prep(nc, cur_ref, prev_ref, next_ref, cw_ref, cb_ref, dtb_ref, aneg_ref):
    ci = pl.program_id(1)
    has_prev = (ci > 0).astype(F32)
    has_next = (ci < nc - 1).astype(F32)
    xbc = cur_ref[0, :, 0:768]
    ext = jnp.concatenate([prev_ref[0] * has_prev, xbc, next_ref[0] * has_next], axis=0)
    acc = cb_ref[...] + cw_ref[0:1, :] * ext[6:6 + CH]
    for j in range(1, SSD_CONV):
        acc = acc + cw_ref[j:j + 1, :] * ext[6 + j:6 + j + CH]
    xbc = _silu(acc)
    dt = _softplus(cur_ref[0, :, 1024:1032] + dtb_ref[...])
    la = dt * aneg_ref[...]
    lane = _iota((1, 8), 1)
    cs = jnp.where(lane < 4, dot_xl(_tri_mask(CH, False, False).astype(BF16), la),
                   dot_xl(_tri_mask(CH, True, False).astype(BF16), la))
    cs_tot = jnp.where(lane < 4, cs[CH - 1:CH], cs[0:1])
    return xbc[:, 0:256], xbc[:, 256:512], xbc[:, 512:768], dt, cs, cs_tot


def _ssd_local_kernel(nc, cur_ref, prev_ref, next_ref, cw_ref, cb_ref, dtb_ref, aneg_ref, cst_ref, dec_ref):
    x, b_rep, _, dt, cs, cs_tot = _ssd_prep(nc, cur_ref, prev_ref, next_ref, cw_ref, cb_ref, dtb_ref, aneg_ref)
    wgt = jnp.exp(cs_tot - cs) * dt
    sel = (_iota((W_BR, W_BR), 0) // HEAD) == (_iota((W_BR, W_BR), 1) // HEAD)
    e_tot = jnp.exp(cs_tot)
    for d in range(2):
        xs = x * _expand_heads(wgt, 4 * d)
        c_blk = jnp.where(sel, dot1(b_rep, xs, "tn"), 0.0)
        cst_ref[0, 0, d] = _compact_blockdiag(c_blk, HEAD, 4)
        dec_ref[0, 0, d] = jnp.broadcast_to(_expand_heads(e_tot, 4 * d), (HEAD, W_BR))


def _ssd_halo_specs(nc):
    rb = CH // 8
    prev = pl.BlockSpec((1, 8, 768), lambda b, c: (b, jnp.maximum(c * rb - 1, 0), 0))
    nxt = pl.BlockSpec((1, 8, 768), lambda b, c: (b, jnp.minimum((c + 1) * rb, nc * rb - 1), 0))
    return prev, nxt


def _ssd_param_args(p):
    cw = jnp.concatenate([p["conv_w_rep"], jnp.zeros((8 - SSD_CONV, 768), F32)], axis=0)
    return cw, p["conv_b_rep"].reshape(1, 768), p["ssd_dt_bias"].reshape(1, 8), p["ssd_a_neg"].reshape(1, 8)


def _ssd_local(f_ssd, p):
    b, L, _ = f_ssd.shape
    nc = L // CH
    prev, nxt = _ssd_halo_specs(nc)
    st_shape = jax.ShapeDtypeStruct((b, nc, 2, HEAD, W_BR), F32)
    st_spec = pl.BlockSpec((1, 1, 2, HEAD, W_BR), lambda i, c: (i, c, 0, 0, 0))
    return pl.pallas_call(
        functools.partial(_ssd_local_kernel, nc),
        grid=(b, nc),
        in_specs=[pl.BlockSpec((1, CH, W_SSD), lambda i, c: (i, c, 0)), prev, nxt,
                  _full((8, 768)), _full((1, 768)), _full((1, 8)), _full((1, 8))],
        out_specs=[st_spec, st_spec],
        out_shape=[st_shape, st_shape],
        compiler_params=_params(("parallel", "parallel")),
        name="ssd_local",
    )(f_ssd, f_ssd, f_ssd, *_ssd_param_args(p))


def _ssd_out_kernel(nc, cur_ref, prev_ref, next_ref, cw_ref, cb_ref, dtb_ref, aneg_ref, sf_ref, sr_ref,
                    dskip_ref, norm_ref, o_ref):
    x, b_rep, c_rep, dt, cs, _ = _ssd_prep(nc, cur_ref, prev_ref, next_ref, cw_ref, cb_ref, dtb_ref, aneg_ref)
    z = cur_ref[0, :, 768:1024]
    lane_head = _iota((1, W_BR), 1) // HEAD
    lane16 = _iota((1, 16), 1)
    cs2 = jnp.concatenate([cs, -cs], axis=1)
    y = dskip_ref[...] * x
    for d, s_ref in enumerate((sf_ref, sr_ref)):
        rev = d == 1
        incl = _tri_mask(CH, rev, False)
        xs = x * _expand_heads(dt, 4 * d)
        y = y + dot1(c_rep * _expand_heads(jnp.exp(cs), 4 * d), _expand_blockdiag(s_ref[0, 0], HEAD, 4))
        for h in range(4):
            col = 4 * d + h
            lhs = jnp.where(lane16 == col, cs2, jnp.where(lane16 == 8 + col, 1.0, 0.0))
            rhs = jnp.where(lane16 == col, 1.0, jnp.where(lane16 == 8 + col, cs2, 0.0))
            l1, l2, l3 = _split3(lhs)
            r1, r2, r3 = _split3(rhs)
            diff = (_dg(l1, r1, "nt") + _dg(l2, r1, "nt")) + (_dg(l3, r1, "nt") + _dg(l1, r2, "nt")) + _dg(l1, r3, "nt")
            seg = jnp.exp(jnp.where(incl, diff, -1e30))
            mh = lane_head == h
            g = dot1(jnp.where(mh, c_rep, 0.0), b_rep, "nt")
            y = y + dot1(g * seg, jnp.where(mh, xs, 0.0))
    y = y * _silu(z)
    y = y * lax.rsqrt(jnp.mean(y * y, axis=-1, keepdims=True) + EPS) * norm_ref[...]
    o_ref[0] = y


def _ssd_out(f_ssd, sf, sr, p):
    b, L, _ = f_ssd.shape
    nc = L // CH
    prev, nxt = _ssd_halo_specs(nc)
    ent_spec = pl.BlockSpec((1, 1, HEAD, W_BR), lambda i, c: (i, c, 0, 0))
    return pl.pallas_call(
        functools.partial(_ssd_out_kernel, nc),
        grid=(b, nc),
        in_specs=[pl.BlockSpec((1, CH, W_SSD), lambda i, c: (i, c, 0)), prev, nxt,
                  _full((8, 768)), _full((1, 768)), _full((1, 8)), _full((1, 8)), ent_spec, ent_spec,
                  _full((1, W_BR)), _full((1, W_BR))],
        out_specs=pl.BlockSpec((1, CH, W_BR), lambda i, c: (i, c, 0)),
        out_shape=jax.ShapeDtypeStruct((b, L, W_BR), F32),
        compiler_params=_params(("parallel", "parallel")),
        name="ssd_out",
    )(f_ssd, f_ssd, f_ssd, *_ssd_param_args(p), sf, sr,
      jnp.repeat(p["ssd_d"], HEAD).reshape(1, W_BR), p["ssd_norm"].reshape(1, W_BR))


def _s5_weights(p):
    t = S5_T
    steps = jnp.arange(t + 1, dtype=F32)
    kmats, fmats, emats, lam_re, lam_im = [], [], [], [], []
    tt = jnp.arange(t)
    tau = tt[None, :] - tt[:, None]
    cmul = lambda x, y: (x[0] * y[0] - x[1] * y[1], x[0] * y[1] + x[1] * y[0])
    for d in range(2):
        a_re, a_im = p["s5_a_re"][d], p["s5_a_im"][d]
        dt = jnp.exp(p["s5_log_dt"][d])[:, None]
        mag = jnp.exp((a_re * dt)[None] * steps[:, None, None])
        ang = (a_im * dt)[None] * steps[:, None, None]
        pw = (mag * jnp.cos(ang), mag * jnp.sin(ang))
        num = (pw[0][1] - 1.0, pw[1][1])
        den = a_re * a_re + a_im * a_im
        quo = ((num[0] * a_re + num[1] * a_im) / den, (num[1] * a_re - num[0] * a_im) / den)
        b_bar = cmul((quo[0][..., None], quo[1][..., None]), (p["s5_b_re"][d], p["s5_b_im"][d]))
        c_c = (p["s5_c_re"][d], p["s5_c_im"][d])
        qb = cmul((pw[0][:t, :, :, None], pw[1][:t, :, :, None]), (b_bar[0][None], b_bar[1][None]))
        kd = jnp.einsum("gcp,jgpe->jgce", c_c[0], qb[0]) - jnp.einsum("gcp,jgpe->jgce", c_c[1], qb[1])
        lag = -tau if d else tau
        kfull = jnp.where((lag >= 0)[:, :, None, None, None], kd[jnp.clip(lag, 0, t - 1)], 0.0)
        kmats.append(jnp.transpose(kfull, (2, 0, 4, 1, 3)).reshape(S5_G, t * S5_CH, t * S5_CH))
        fsel = (lambda x: x) if d else (lambda x: x[::-1])
        fmats += [jnp.transpose(fsel(x), (1, 0, 3, 2)).reshape(S5_G, t * S5_CH, S5_P) for x in qb]
        esel = (lambda x: x[1:][::-1]) if d else (lambda x: x[1:])
        ec = cmul((c_c[0][None], c_c[1][None]), (esel(pw[0])[:, :, None, :], esel(pw[1])[:, :, None, :]))
        ec = [jnp.transpose(x, (1, 3, 0, 2)).reshape(S5_G, S5_P, t * S5_CH) for x in ec]
        emats += [ec[0], -ec[1]]
        lam_re += [pw[0][t], pw[0][t]]
        lam_im += [-pw[1][t], pw[1][t]]
    kmat = kmats[0] + kmats[1]
    fmat = jnp.concatenate(fmats, axis=2)
    emat = jnp.concatenate(emats, axis=1)
    return (kmat.astype(BF16), fmat.astype(BF16), emat.astype(BF16),
            jnp.concatenate(lam_re, axis=1), jnp.concatenate(lam_im, axis=1))


def _s5_local_kernel(u_ref, f_ref, z_ref):
    z_ref[0, 0] = _dg(_bf(u_ref[0, 0]), f_ref[0], "nn")


def _s5_local(ug, fmat):
    b, g, nc, w = ug.shape
    return pl.pallas_call(
        _s5_local_kernel,
        grid=(b, g),
        in_specs=[pl.BlockSpec((1, 1, nc, w), lambda i, j: (i, j, 0, 0)),
                  pl.BlockSpec((1, w, w), lambda i, j: (j, 0, 0))],
        out_specs=pl.BlockSpec((1, 1, nc, w), lambda i, j: (i, j, 0, 0)),
        out_shape=jax.ShapeDtypeStruct((b, g, nc, w), F32),
        compiler_params=_params(("parallel", "parallel")),
        name="s5_local",
    )(ug, fmat)


def _s5_scan_kernel(nblk, zf_ref, zr_ref, lre_ref, lim_ref, init_ref, xf_ref, xr_ref, fin_ref, x_scr):
    ci = pl.program_id(1)

    @pl.when(ci == 0)
    def _():
        x_scr[...] = init_ref[0]

    lane = _iota((1, 256), 1)
    is_fwd = lane < 128
    even_q = (lane // S5_P) % 2 == 0
    lre, lim = lre_ref[...], lim_ref[...]
    x = x_scr[...]
    for i in range(8):
        xf_ref[0, :, i, :] = x
        xr_ref[0, :, 7 - i, :] = x
        z = jnp.where(is_fwd, zf_ref[0, :, i, :], zr_ref[0, :, 7 - i, :])
        swapped = jnp.where(even_q, pltpu.roll(x, 192, 1), pltpu.roll(x, 64, 1))
        x = lre * x + lim * swapped + z
    x_scr[...] = x

    @pl.when(ci == nblk - 1)
    def _():
        fin_ref[0] = x


def _s5_scan(z, lam_re, lam_im, init):
    b, g, nc, w = z.shape
    nblk = nc // 8
    fwd = pl.BlockSpec((1, g, 8, w), lambda i, j: (i, 0, j, 0))
    rev = pl.BlockSpec((1, g, 8, w), lambda i, j: (i, 0, nblk - 1 - j, 0))
    ent = jax.ShapeDtypeStruct((b, g, nc, w), F32)
    return pl.pallas_call(
        functools.partial(_s5_scan_kernel, nblk),
        grid=(b, nblk),
        in_specs=[fwd, rev, _full((g, w)), _full((g, w)), pl.BlockSpec((1, g, w), lambda i, j: (i, 0, 0))],
        out_specs=[fwd, rev, pl.BlockSpec((1, g, w), lambda i, j: (i, 0, 0))],
        out_shape=[ent, ent, jax.ShapeDtypeStruct((b, g, w), F32)],
        scratch_shapes=[pltpu.VMEM((g, w), F32)],
        compiler_params=_params(("parallel", "arbitrary")),
        name="s5_scan",
    )(z, z, lam_re, lam_im, init)


def _s5_out_kernel(u_ref, xf_ref, xr_ref, k_ref, e_ref, y_ref):
    lane = _iota((1, 256), 1)
    x_ent = jnp.where(lane < 128, xf_ref[0, 0], xr_ref[0, 0])
    x_hi, x_lo = _split2(x_ent)
    y_ref[0, 0] = _dg(_bf(u_ref[0, 0]), k_ref[0], "nn") + (_dg(x_hi, e_ref[0], "nn") + _dg(x_lo, e_ref[0], "nn"))


def _s5_out(ug, xf, xr, kmat, emat):
    b, g, nc, w = ug.shape
    row = pl.BlockSpec((1, 1, nc, w), lambda i, j: (i, j, 0, 0))
    mat = pl.BlockSpec((1, w, w), lambda i, j: (j, 0, 0))
    return pl.pallas_call(
        _s5_out_kernel,
        grid=(b, g),
        in_specs=[row, row, row, mat, mat],
        out_specs=row,
        out_shape=jax.ShapeDtypeStruct((b, g, nc, w), F32),
        compiler_params=_params(("parallel", "parallel")),
        name="s5_out",
    )(ug, xf, xr, kmat, emat)


def _outproj_kernel(h_ref, mod_ref, yrw_ref, ys5_ref, fs5_ref, yssd_ref, ygla_ref, s5d_ref, gluw_ref, glub_ref,
                    w_ref, g_ref, o_ref):
    d = h_ref.shape[2]
    u = fs5_ref[0, :, 0:256]
    y = ys5_ref[0] + s5d_ref[...] * u
    y = 0.5 * y * (1.0 + jnp.tanh(math.sqrt(2.0 / math.pi) * (y + 0.044715 * (y * y * y))))
    y = y * _sigmoid(dot1(y, gluw_ref[...]) + glub_ref[...])
    y_s5 = y * _silu(fs5_ref[0, :, 256:512])
    ycat = jnp.concatenate([yrw_ref[0], y_s5, yssd_ref[0], ygla_ref[0]], axis=1)
    o = dot1(ycat, w_ref[...])
    o = o * lax.rsqrt(jnp.mean(o * o, axis=-1, keepdims=True) + EPS) * g_ref[...]
    o_ref[0] = h_ref[0] + mod_ref[0][:, 2 * d:3 * d] * o


def _outproj(h, mod3, mod_row, y_rw, y_s5, f_s5, y_ssd, y_gla, p, tm):
    b, L, d = h.shape
    row = lambda w: pl.BlockSpec((1, tm, w), lambda i, j: (i, j, 0))
    return pl.pallas_call(
        _outproj_kernel,
        grid=(b, L // tm),
        in_specs=[row(d), pl.BlockSpec((1, 1, 3 * d), lambda i, j: (mod_row(i), 0, 0)),
                  row(W_BR), row(W_BR), row(W_S5), row(W_BR), row(W_BR),
                  _full((1, W_BR)), _full((W_BR, W_BR)), _full((1, W_BR)), _full((d, d)), _full((1, d))],
        out_specs=row(d),
        out_shape=jax.ShapeDtypeStruct((b, L, d), F32),
        compiler_params=_params(("parallel", "parallel")),
        name="out_proj",
    )(h, mod3, y_rw, y_s5, f_s5, y_ssd, y_gla, p["s5_d"].reshape(1, W_BR), p["s5_glu_w"].astype(BF16),
      p["s5_glu_b"].reshape(1, W_BR), p["w_out"].astype(BF16), p["norm_post"].reshape(1, d))


def _permute_w_in(w_in):
    rw = w_in[:, 0:1280]
    s5 = w_in[:, 1280:1792]
    o = 1792
    x, bm, cm = w_in[:, o:o + 256], w_in[:, o + 256:o + 384], w_in[:, o + 384:o + 512]
    dt, z = w_in[:, o + 512:o + 520], w_in[:, o + 520:o + 776]
    rep = lambda m: jnp.concatenate([m[:, 0:64], m[:, 0:64], m[:, 64:128], m[:, 64:128]], axis=1)
    pad = lambda n: jnp.zeros((w_in.shape[0], n), w_in.dtype)
    ssd = jnp.concatenate([x, rep(bm), rep(cm), z, dt, pad(120)], axis=1)
    o = 1792 + 776
    q, k, v = w_in[:, o:o + 128], w_in[:, o + 128:o + 256], w_in[:, o + 256:o + 512]
    gl, gate = w_in[:, o + 512:o + 544], w_in[:, o + 544:o + 800]
    gla = jnp.concatenate([q, k, v, gate, gl, pad(96)], axis=1)
    return jnp.concatenate([rw, s5, ssd, gla], axis=1).astype(BF16)


def _layer_params(l, a):
    p = {k: v[l] for k, v in a.items()}
    rep = lambda m: jnp.concatenate([m[..., 0:64], m[..., 0:64], m[..., 64:128], m[..., 64:128]], axis=-1)
    cw, cb = p["ssd_conv_w"], p["ssd_conv_b"]
    p["conv_w_rep"] = jnp.concatenate([cw[:, 0:256], rep(cw[:, 256:384]), rep(cw[:, 384:512])], axis=1)
    p["conv_b_rep"] = jnp.concatenate([cb[0:256], rep(cb[256:384]), rep(cb[384:512])], axis=0)
    p["ssd_a_neg"] = -jnp.exp(p["ssd_a_log"])
    p["w_in_p"] = _permute_w_in(p["w_in"])
    p["s5_ops"] = _s5_weights(p)
    return p


def _mixer_layer(h, mod3, mod_row, p, is_grid, init, need_out):
    b, L, d = h.shape
    tm = min(256, L)
    f_rw, f_s5, f_ssd, f_gla = _inproj(h, mod3, mod_row, p["norm_pre"], p["w_in_p"], tm)
    if init is None:
        init = (jnp.zeros((b, 2, HEAD, W_BR), F32), jnp.zeros((b, S5_G, 256), F32),
                jnp.zeros((b, 2, HEAD, W_BR), F32), jnp.zeros((b, 2, GLA_DK, W_BR), F32))
    rt, y0, bonus, m_rw, c_rw = _rw_local(f_rw, p, is_grid)
    rw_sf, rw_sr, rw_fin = _rw_scan(m_rw, c_rw, init[0])
    kmat, fmat, emat, lam_re, lam_im = p["s5_ops"]
    nc5 = L // S5_T
    ug = f_s5[:, :, 0:256].reshape(b, nc5, S5_T, S5_G, S5_CH).transpose(0, 3, 1, 2, 4).reshape(b, S5_G, nc5, 256)
    z5 = _s5_local(ug, fmat)
    s5_xf, s5_xr, s5_fin = _s5_scan(z5, lam_re, lam_im, init[1])
    ssd_c, ssd_d = _ssd_local(f_ssd, p)
    ssd_sf, ssd_sr, ssd_fin = _ew_scan(ssd_c, ssd_d, init[2], "ssd_scan")
    gla_c, gla_d = _gla_local(f_gla, p)
    gla_sf, gla_sr, gla_fin = _ew_scan(gla_c, gla_d, init[3], "gla_scan")
    states = (rw_fin, s5_fin, ssd_fin, gla_fin)
    if not need_out:
        return None, states
    y_rw = _rw_out(y0, bonus, rt, rw_sf, rw_sr, f_rw, p)
    y5 = _s5_out(ug, s5_xf, s5_xr, kmat, emat)
    y5 = y5.reshape(b, S5_G, nc5, S5_T, S5_CH).transpose(0, 2, 3, 1, 4).reshape(b, L, W_BR)
    y_ssd = _ssd_out(f_ssd, ssd_sf, ssd_sr, p)
    y_gla = _gla_out(f_gla, gla_sf, gla_sr, p)
    return _outproj(h, mod3, mod_row, y_rw, y5, f_s5, y_ssd, y_gla, p, tm), states


def kernel(x, c, ctx, c_ctx, ada_w, ada_b, norm_pre, norm_post, w_in, w_out, rw_mu, rw_w0, rw_w2, rw_a0, rw_a2, rw_kk, rw_ka, rw_rk, rw_ln_w, rw_ln_b, s5_a_re, s5_a_im, s5_log_dt, s5_b_re, s5_b_im, s5_c_re, s5_c_im, s5_d, s5_glu_w, s5_glu_b, ssd_conv_w, ssd_conv_b, ssd_dt_bias, ssd_a_log, ssd_d, ssd_norm, gla_g2, gla_gb, gla_norm):
    stacked = dict(norm_pre=norm_pre, norm_post=norm_post, w_in=w_in, w_out=w_out, rw_mu=rw_mu, rw_w0=rw_w0,
                   rw_w2=rw_w2, rw_a0=rw_a0, rw_a2=rw_a2, rw_kk=rw_kk, rw_ka=rw_ka, rw_rk=rw_rk, rw_ln_w=rw_ln_w,
                   rw_ln_b=rw_ln_b, s5_a_re=s5_a_re, s5_a_im=s5_a_im, s5_log_dt=s5_log_dt, s5_b_re=s5_b_re,
                   s5_b_im=s5_b_im, s5_c_re=s5_c_re, s5_c_im=s5_c_im, s5_d=s5_d, s5_glu_w=s5_glu_w,
                   s5_glu_b=s5_glu_b, ssd_conv_w=ssd_conv_w, ssd_conv_b=ssd_conv_b, ssd_dt_bias=ssd_dt_bias,
                   ssd_a_log=ssd_a_log, ssd_d=ssd_d, ssd_norm=ssd_norm, gla_g2=gla_g2, gla_gb=gla_gb,
                   gla_norm=gla_norm)
    depth = ada_w.shape[0]
    b, d = c.shape
    rows = -(-(b + 1) // 8) * 8
    cvec = jnp.concatenate([c, c_ctx[None, :], jnp.zeros((rows - b - 1, d), F32)], axis=0)
    mod = _modulation(cvec, ada_w, ada_b)
    h, hc = x, ctx
    for l in range(depth):
        p = _layer_params(l, stacked)
        mod3 = mod[l].reshape(rows, 1, 3 * d)
        last = l == depth - 1
        hc_next, ctx_states = _mixer_layer(hc, mod3, lambda i: b, p, False, None, not last)
        h, _ = _mixer_layer(h, mod3, lambda i: i, p, True, ctx_states, True)
        hc = hc_next
    return h
```

```python
import functools
import math
import operator

import jax
import jax.numpy as jnp
from jax import lax
from jax.experimental import pallas as pl
from jax.experimental.pallas import tpu as pltpu

F32 = jnp.float32
BF16 = jnp.bfloat16

EPS = 1e-6
GRID_W = 64
W_BR = 256
CH = 64
HEAD = 64
RW_GN_EPS = 64e-5
RW_LORA = 64
S5_G, S5_CH, S5_P, S5_T = 16, 16, 64, 16
SSD_CONV = 5
GLA_RANK = 16
GLA_DK = 32
GLA_TAU = 16.0
W_RW, W_S5, W_SSD, W_GLA = 1280, 512, 1152, 896
N_PROJ = W_RW + W_S5 + W_SSD + W_GLA
VMEM_LIMIT = 48 * 1024 * 1024
NB_RW = 2
NB = 4
SCAN_CB = 8
SCAN_NB = 2


class _Lk:
    def __init__(self, xs):
        self.xs = list(xs)

    @property
    def shape(self):
        return self.xs[0].shape

    def __getitem__(self, idx):
        return _Lk([x[idx] for x in self.xs])

    def astype(self, dt):
        return _Lk([x.astype(dt) for x in self.xs])


def _pick(a, i):
    if isinstance(a, _Lk):
        return a.xs[i]
    if isinstance(a, (list, tuple)):
        return [_pick(e, i) for e in a]
    return a


def _width(a):
    if isinstance(a, _Lk):
        return len(a.xs)
    if isinstance(a, (list, tuple)):
        for e in a:
            w = _width(e)
            if w:
                return w
    return 0


def _lift(f):
    def g(*args, **kw):
        n = max(_width(a) for a in args)
        if not n:
            return f(*args, **kw)
        return _Lk([f(*[_pick(a, i) for a in args], **kw) for i in range(n)])
    return g


for _name, _op in (("add", operator.add), ("sub", operator.sub), ("mul", operator.mul), ("truediv", operator.truediv)):
    setattr(_Lk, f"__{_name}__", lambda s, o, _f=_lift(_op): _f(s, o))
    setattr(_Lk, f"__r{_name}__", lambda s, o, _f=_lift(_op): _f(o, s))
_Lk.__neg__ = lambda s: _lift(operator.neg)(s)

_exp = _lift(jnp.exp)
_log = _lift(jnp.log)
_tanh = _lift(jnp.tanh)
_abs = _lift(jnp.abs)
_maximum = _lift(jnp.maximum)
_rsqrt = _lift(lax.rsqrt)
_where = _lift(jnp.where)
_cat = _lift(lambda parts, axis: jnp.concatenate(parts, axis=axis))
_roll = _lift(pltpu.roll)
_bcast = _lift(jnp.broadcast_to)
_mean_last = _lift(lambda x: jnp.mean(x, axis=-1, keepdims=True))


def _load(ref, *idx):
    return _Lk([ref[(s,) + idx] for s in range(ref.shape[0])])


def _store(ref, idx, val):
    for s in range(ref.shape[0]):
        ref[(s,) + idx] = val.xs[s]


_dg = _lift(lambda a, b, mode: lax.dot_general(
    a, b, ((({"nn": 1, "nt": 1, "tn": 0}[mode],), ({"nn": 0, "nt": 1, "tn": 0}[mode],)), ((), ())),
    preferred_element_type=F32))


def _bf(a):
    return a.astype(BF16)


def _split2(a):
    hi = _bf(a)
    return hi, _bf(a - hi.astype(F32))


def _split3(a):
    hi = _bf(a)
    r = a - hi.astype(F32)
    mid = _bf(r)
    return hi, mid, _bf(r - mid.astype(F32))


def _hilo(a):
    hi = _bf(a).astype(F32)
    return hi, a - hi


def dot1(a, b, mode="nn"):
    return _dg(_bf(a), _bf(b), mode)


def dot3(a, b, mode="nn"):
    ah, al = _split2(a)
    bh, bl = _split2(b)
    return _dg(ah, bh, mode) + (_dg(ah, bl, mode) + _dg(al, bh, mode))


def dotk3(a, b, mode="nn"):
    ah, al = _hilo(a)
    bh, bl = _hilo(b)
    ax = 0 if mode == "tn" else 1
    bx = 1 if mode == "nt" else 0
    return _dg(_bf(_cat([ah, al, ah], ax)), _bf(_cat([bh, bh, bl], bx)), mode)


def dot_xl(a_exact, b, mode="nn"):
    b1, b2, b3 = _split3(b)
    return _dg(a_exact, b1, mode) + (_dg(a_exact, b2, mode) + _dg(a_exact, b3, mode))


def dot_xr(a, b_exact, mode="nn"):
    a1, a2, a3 = _split3(a)
    return _dg(a1, b_exact, mode) + (_dg(a2, b_exact, mode) + _dg(a3, b_exact, mode))


def _sigmoid(x):
    return 1.0 / (1.0 + _exp(-x))


def _silu(x):
    return x * _sigmoid(x)


def _softplus(x):
    return _maximum(x, 0.0) + _log(1.0 + _exp(-_abs(x)))


def _iota(shape, dim):
    return lax.broadcasted_iota(jnp.int32, shape, dim)


def _tri_mask(n, rev, strict):
    r, c = _iota((n, n), 0), _iota((n, n), 1)
    if rev:
        return (r < c) if strict else (r <= c)
    return (r > c) if strict else (r >= c)


def _block_ones(n, blk):
    return (_iota((n, n), 0) // blk == _iota((n, n), 1) // blk).astype(BF16)


def _expand_blockdiag(compact, n_heads=4):
    lane_head = _iota((1, compact.shape[1]), 1) // HEAD
    return _cat([_where(lane_head == h, compact, 0.0) for h in range(n_heads)], 0)


def _compact_blockdiag(blk, rows_per_head, n_heads=4):
    lane_head = _iota((1, blk.shape[1]), 1) // HEAD
    out = _where(lane_head == 0, blk[0:rows_per_head], 0.0)
    for h in range(1, n_heads):
        out = out + _where(lane_head == h, blk[h * rows_per_head:(h + 1) * rows_per_head], 0.0)
    return out


def _expand_heads(cols, first, n_heads=4):
    lane_head = _iota((1, n_heads * HEAD), 1) // HEAD
    out = _where(lane_head == 0, cols[:, first:first + 1], 0.0)
    for h in range(1, n_heads):
        out = _where(lane_head == h, cols[:, first + h:first + h + 1], out)
    return out


def _shift_rows(x, edge_row, down):
    n = x.shape[0]
    rows = _iota((n, 1), 0)
    if down:
        return _where(rows == 0, edge_row, _roll(x, 1, 0))
    return _where(rows == n - 1, edge_row, _roll(x, n - 1, 0))


def _params(sem):
    return pltpu.CompilerParams(dimension_semantics=sem, vmem_limit_bytes=VMEM_LIMIT)


def _full(shape):
    nd = len(shape)
    return pl.BlockSpec(shape, lambda *_: (0,) * nd)


def _nb(b, want):
    return want if b % want == 0 else 1


def _mod_kernel(c_ref, w_ref, b_ref, o_ref):
    s = _silu(c_ref[...])
    o_ref[0] = dot3(s, w_ref[0]) + b_ref[0]


def _modulation(cvec, ada_w, ada_b):
    depth, d, n = ada_w.shape
    rows = cvec.shape[0]
    tn = 1024
    return pl.pallas_call(
        _mod_kernel,
        grid=(depth, n // tn),
        in_specs=[pl.BlockSpec((rows, d), lambda l, j: (0, 0)),
                  pl.BlockSpec((1, d, tn), lambda l, j: (l, 0, j)),
                  pl.BlockSpec((1, 1, tn), lambda l, j: (l, 0, j))],
        out_specs=pl.BlockSpec((1, rows, tn), lambda l, j: (l, 0, j)),
        out_shape=jax.ShapeDtypeStruct((depth, rows, n), F32),
        compiler_params=_params(("parallel", "parallel")),
        name="adaln_mod",
    )(cvec, ada_w, ada_b.reshape(depth, 1, n))


def _inproj_kernel(x_ref, mod_ref, g_ref, w_ref, o_rw, o_s5, o_ssd, o_gla):
    x = x_ref[0]
    d = x.shape[1]
    hn = x * lax.rsqrt(jnp.mean(x * x, axis=-1, keepdims=True) + EPS) * g_ref[...]
    m = mod_ref[0]
    hn = hn * (1.0 + m[:, d:2 * d]) + m[:, 0:d]
    p = dot1(hn, w_ref[...])
    o_rw[0] = p[:, 0:W_RW]
    o_s5[0] = p[:, W_RW:W_RW + W_S5]
    o_ssd[0] = p[:, W_RW + W_S5:W_RW + W_S5 + W_SSD]
    o_gla[0] = p[:, W_RW + W_S5 + W_SSD:N_PROJ]


def _inproj(h, mod3, mod_row, norm_pre, w_in_p, tm):
    b, L, d = h.shape
    widths = (W_RW, W_S5, W_SSD, W_GLA)
    return pl.pallas_call(
        _inproj_kernel,
        grid=(b, L // tm),
        in_specs=[pl.BlockSpec((1, tm, d), lambda i, j: (i, j, 0)),
                  pl.BlockSpec((1, 1, 3 * d), lambda i, j: (mod_row(i), 0, 0)),
                  _full((1, d)),
                  _full(w_in_p.shape)],
        out_specs=[pl.BlockSpec((1, tm, w), lambda i, j: (i, j, 0)) for w in widths],
        out_shape=[jax.ShapeDtypeStruct((b, L, w), F32) for w in widths],
        compiler_params=_params(("parallel", "parallel")),
        name="in_proj",
    )(h, mod3, norm_pre.reshape(1, d), w_in_p)


def _tri_inv_many(n_mats):
    r, c = _iota((CH, CH), 0), _iota((CH, CH), 1)
    blk16 = (r // 16) == (c // 16)
    blk32 = (r // 32) == (c // 32)
    eye = jnp.where(r == c, 1.0, 0.0)
    nd = [_where(blk16, n, 0.0) for n in n_mats]
    d = [eye + x for x in nd]
    p = [dot1(x, x) for x in nd]
    for _ in range(2):
        d = [x + dot1(x, y) for x, y in zip(d, p)]
        p = [dot1(y, y) for y in p]
    d = [x + dot1(x, y) for x, y in zip(d, p)]
    for sel in (blk32 & jnp.logical_not(blk16), jnp.logical_not(blk32)):
        od = [dot1(_where(sel, n, 0.0), x) for n, x in zip(n_mats, d)]
        d = [x + dot1(x, y) for x, y in zip(d, od)]
    return d


def _rw_local_kernel(is_grid, nc, cur_ref, prev_ref, next_ref, mu_ref, w0_ref, w2_ref, a0_ref, a2_ref,
                     kk_ref, ka_ref, rk_ref, rt_ref, y0_ref, bonus_ref, m_ref, c_ref):
    ci = pl.program_id(1)
    has_prev = (ci > 0).astype(F32)
    has_next = (ci < nc - 1).astype(F32)
    z = _load(cur_ref, slice(None), slice(0, 1024))
    if is_grid:
        zero_row = jnp.zeros((1, W_BR), F32)
        left = _shift_rows(z[:, 0:256], zero_row, True)
        right = _shift_rows(z[:, 256:512], zero_row, False)
        up = _load(prev_ref) * has_prev
        down = _load(next_ref) * has_next
        sh = _cat([left, right, up, down], 1)
    else:
        prow = _load(prev_ref, slice(7, 8)) * has_prev
        nrow = _load(next_ref, slice(0, 1)) * has_next
        sh = _cat([_shift_rows(z[:, 0:512], prow, True), _shift_rows(z[:, 512:1024], nrow, False)], 1)
    zs = z + mu_ref[...] * (sh - z)
    r, k, v = zs[:, 0:256], zs[:, 256:512], zs[:, 512:768]
    wl, al = zs[:, 768:896], zs[:, 896:1024]

    bo = _block_ones(W_BR, HEAD)
    kk = k * kk_ref[...]
    kk = kk * _rsqrt(dot_xr(kk * kk, bo) + EPS)
    eye = _iota((HEAD, HEAD), 0) == _iota((HEAD, HEAD), 1)
    heads = lambda t: [t[:, HEAD * h:HEAD * (h + 1)] for h in range(4)]
    v_h = heads(v)

    bonus = None
    pa_l, pam_l, prm_l, pr_l, nb_l, nk_l, eb_l, ek_l, dg_l, strict_l, incl_l = ([] for _ in range(11))
    for d in range(2):
        rev = d == 1
        w_pre = w0_ref[d:d + 1, :] + dot3(_tanh(wl[:, RW_LORA * d:RW_LORA * (d + 1)]), w2_ref[d])
        lw = (-math.exp(-0.5)) * _sigmoid(w_pre)
        a = _sigmoid(a0_ref[d:d + 1, :] + dot3(al[:, RW_LORA * d:RW_LORA * (d + 1)], a2_ref[d]))
        k_d = k * (1.0 + (a - 1.0) * ka_ref[...])
        bterm = dot_xr(r * k_d * rk_ref[...], bo) * v
        bonus = bterm if bonus is None else bonus + bterm

        cs = dot_xl(_tri_mask(CH, rev, False).astype(BF16), lw)
        cs_ex = cs - lw
        cs_tot = cs[0:1] if rev else cs[CH - 1:CH]
        cs_mid = cs[CH // 2:CH // 2 + 1]
        kka = kk * a
        e_mid = _exp(cs_mid - cs)
        e_tot = _exp(cs_tot - cs)
        pa_l += heads(-kk * _exp(cs_ex))
        pr_l += heads(r * _exp(cs))
        pam_l += heads(-kk * _exp(cs_ex - cs_mid))
        prm_l += heads(r * _exp(cs - cs_mid))
        nb_l += heads(kka * e_mid)
        nk_l += heads(k_d * e_mid)
        eb_l += heads(kka * e_tot)
        ek_l += heads(k_d * e_tot)
        dg_l += [_where(eye, t, 0.0) for t in heads(_exp(cs_tot))]
        strict_l += [_tri_mask(CH, rev, True)] * 4
        incl_l += [_tri_mask(CH, rev, False)] * 4
    vv = v_h + v_h
    idx = range(8)

    n_l = [_where(strict_l[i], dotk3(pam_l[i], nb_l[i], "nt"), 0.0) for i in idx]
    g1 = [dot1(_cat([pam_l[i], prm_l[i]], 0), nk_l[i], "nt") for i in idx]
    aak = [_where(strict_l[i], g1[i][0:CH], 0.0) for i in idx]
    gk = [_where(incl_l[i], g1[i][CH:2 * CH], 0.0) for i in idx]
    gb = [_where(incl_l[i], dot1(prm_l[i], nb_l[i], "nt"), 0.0) for i in idx]
    t_inv = _tri_inv_many(n_l)
    av = [dot1(aak[i], vv[i]) for i in idx]
    wu = [dotk3(t_inv[i], _cat([pa_l[i], av[i]], 1)) for i in idx]
    gwu = [dot1(gb[i], wu[i]) for i in idx]
    gkv = [dot1(gk[i], vv[i]) for i in idx]
    mc = [dotk3(eb_l[i], wu[i], "tn") for i in idx]
    ekv = [dot1(ek_l[i], vv[i], "tn") for i in idx]
    y0_parts = []
    for d in range(2):
        sl = range(4 * d, 4 * d + 4)
        _store(rt_ref, (0, d), _cat([pr_l[i] + gwu[i][:, 0:HEAD] for i in sl], 1))
        _store(m_ref, (0, d), _cat([dg_l[i] + mc[i][:, 0:HEAD] for i in sl], 1))
        _store(c_ref, (0, d), _cat([mc[i][:, HEAD:2 * HEAD] + ekv[i] for i in sl], 1))
        y0_parts.append(_cat([gwu[i][:, HEAD:2 * HEAD] + gkv[i] for i in sl], 1))
    _store(y0_ref, (), y0_parts[0] + y0_parts[1])
    _store(bonus_ref, (), bonus)


def _halo_specs(nb, is_grid, nc):
    if is_grid:
        prev = pl.BlockSpec((nb, CH, W_BR), lambda b, c: (b, jnp.maximum(c - 1, 0), 2))
        nxt = pl.BlockSpec((nb, CH, W_BR), lambda b, c: (b, jnp.minimum(c + 1, nc - 1), 3))
    else:
        rb = CH // 8
        prev = pl.BlockSpec((nb, 8, 512), lambda b, c: (b, jnp.maximum(c * rb - 1, 0), 0))
        nxt = pl.BlockSpec((nb, 8, 512), lambda b, c: (b, jnp.minimum((c + 1) * rb, nc * rb - 1), 1))
    return prev, nxt


def _rw_local(f_rw, p, is_grid):
    b, L, _ = f_rw.shape
    nc = L // CH
    nb = _nb(b, NB_RW)
    prev, nxt = _halo_specs(nb, is_grid, nc)
    state_shape = jax.ShapeDtypeStruct((b, nc, 2, HEAD, W_BR), F32)
    state_spec = pl.BlockSpec((nb, 1, 2, HEAD, W_BR), lambda i, c: (i, c, 0, 0, 0))
    row_spec = pl.BlockSpec((nb, CH, W_BR), lambda i, c: (i, c, 0))
    row_shape = jax.ShapeDtypeStruct((b, L, W_BR), F32)
    return pl.pallas_call(
        functools.partial(_rw_local_kernel, is_grid, nc),
        grid=(b // nb, nc),
        in_specs=[pl.BlockSpec((nb, CH, W_RW), lambda i, c: (i, c, 0)), prev, nxt,
                  _full((1, 1024)), _full((2, W_BR)), _full((2, RW_LORA, W_BR)), _full((2, W_BR)),
                  _full((2, RW_LORA, W_BR)), _full((1, W_BR)), _full((1, W_BR)), _full((1, W_BR))],
        out_specs=[state_spec, row_spec, row_spec, state_spec, state_spec],
        out_shape=[state_shape, row_shape, row_shape, state_shape, state_shape],
        compiler_params=_params(("parallel", "parallel")),
        name="rwkv_local",
    )(f_rw, f_rw, f_rw, p["rw_mu"].reshape(1, 1024), p["rw_w0"], p["rw_w2"], p["rw_a0"], p["rw_a2"],
      p["rw_kk"].reshape(1, W_BR), p["rw_ka"].reshape(1, W_BR), p["rw_rk"].reshape(1, W_BR))


def _scan_specs(nb, cb, nblk, rows, width):
    in_f = pl.BlockSpec((nb, cb, 1, rows, width), lambda i, j: (i, j, 0, 0, 0))
    in_r = pl.BlockSpec((nb, cb, 1, rows, width), lambda i, j: (i, nblk - 1 - j, 1, 0, 0))
    out_f = pl.BlockSpec((nb, cb, rows, width), lambda i, j: (i, j, 0, 0))
    out_r = pl.BlockSpec((nb, cb, rows, width), lambda i, j: (i, nblk - 1 - j, 0, 0))
    st = pl.BlockSpec((nb, 2, rows, width), lambda i, j: (i, 0, 0, 0))
    return in_f, in_r, out_f, out_r, st


def _rw_scan_kernel(nblk, cb, mf_ref, mr_ref, cf_ref, cr_ref, init_ref, sf_ref, sr_ref, fin_ref, s_scr):
    ci = pl.program_id(1)

    @pl.when(ci == 0)
    def _():
        s_scr[...] = init_ref[...]

    heads = lambda t: [t[:, HEAD * h:HEAD * (h + 1)] for h in range(4)]
    s = [_load(s_scr, 0), _load(s_scr, 1)]
    for j in range(cb):
        new = []
        for d, (m_ref, c_ref, o_ref) in enumerate(((mf_ref, cf_ref, sf_ref), (mr_ref, cr_ref, sr_ref))):
            jj = cb - 1 - j if d else j
            _store(o_ref, (jj,), s[d])
            m_h = heads(_load(m_ref, jj, 0))
            s_h = heads(s[d])
            prod = [dotk3(m_h[h], s_h[h]) for h in range(4)]
            new.append(_cat(prod, 1) + _load(c_ref, jj, 0))
        s = new
    _store(s_scr, (0,), s[0])
    _store(s_scr, (1,), s[1])

    @pl.when(ci == nblk - 1)
    def _():
        _store(fin_ref, (0,), s[0])
        _store(fin_ref, (1,), s[1])


def _scan_call(kernel_fn, name, a, b_arr, init):
    b, nc, _, rows, width = a.shape
    nb = _nb(b, SCAN_NB)
    cb = SCAN_CB if nc % SCAN_CB == 0 else nc
    nblk = nc // cb
    in_f, in_r, out_f, out_r, st = _scan_specs(nb, cb, nblk, rows, width)
    ent_shape = jax.ShapeDtypeStruct((b, nc, rows, width), F32)
    return pl.pallas_call(
        functools.partial(kernel_fn, nblk, cb),
        grid=(b // nb, nblk),
        in_specs=[in_f, in_r, in_f, in_r, st],
        out_specs=[out_f, out_r, st],
        out_shape=[ent_shape, ent_shape, jax.ShapeDtypeStruct((b, 2, rows, width), F32)],
        scratch_shapes=[pltpu.VMEM((nb, 2, rows, width), F32)],
        compiler_params=_params(("parallel", "arbitrary")),
        name=name,
    )(a, a, b_arr, b_arr, init)


def _rw_out_kernel(y0_ref, bonus_ref, rt_ref, sf_ref, sr_ref, gate_ref, lnw_ref, lnb_ref, o_ref):
    y = _load(y0_ref)
    y = y + dot1(_load(rt_ref, 0, 0), _expand_blockdiag(_load(sf_ref, 0)))
    y = y + dot1(_load(rt_ref, 0, 1), _expand_blockdiag(_load(sr_ref, 0)))
    bo = _block_ones(W_BR, HEAD)
    mu = dot_xr(y, bo) * (1.0 / HEAD)
    yc = y - mu
    var = dot_xr(yc * yc, bo) * (1.0 / HEAD)
    yn = yc * _rsqrt(var + RW_GN_EPS) * lnw_ref[...] + lnb_ref[...]
    _store(o_ref, (), (yn + _load(bonus_ref)) * _silu(_load(gate_ref)))


def _rw_out(y0, bonus, rt, sf, sr, f_rw, p):
    b, L, _ = y0.shape
    nc = L // CH
    nb = _nb(b, NB)
    row_spec = pl.BlockSpec((nb, CH, W_BR), lambda i, c: (i, c, 0))
    ent_spec = pl.BlockSpec((nb, 1, HEAD, W_BR), lambda i, c: (i, c, 0, 0))
    return pl.pallas_call(
        _rw_out_kernel,
        grid=(b // nb, nc),
        in_specs=[row_spec, row_spec, pl.BlockSpec((nb, 1, 2, HEAD, W_BR), lambda i, c: (i, c, 0, 0, 0)),
                  ent_spec, ent_spec, pl.BlockSpec((nb, CH, W_BR), lambda i, c: (i, c, 4)),
                  _full((1, W_BR)), _full((1, W_BR))],
        out_specs=row_spec,
        out_shape=jax.ShapeDtypeStruct((b, L, W_BR), F32),
        compiler_params=_params(("parallel", "parallel")),
        name="rwkv_out",
    )(y0, bonus, rt, sf, sr, f_rw, p["rw_ln_w"].reshape(1, W_BR), p["rw_ln_b"].reshape(1, W_BR))


def _ew_scan_kernel(nblk, cb, cf_ref, cr_ref, df_ref, dr_ref, init_ref, sf_ref, sr_ref, fin_ref, s_scr):
    ci = pl.program_id(1)

    @pl.when(ci == 0)
    def _():
        s_scr[...] = init_ref[...]

    s = [_load(s_scr, 0), _load(s_scr, 1)]
    for j in range(cb):
        for d, (c_ref, d_ref, o_ref) in enumerate(((cf_ref, df_ref, sf_ref), (cr_ref, dr_ref, sr_ref))):
            jj = cb - 1 - j if d else j
            _store(o_ref, (jj,), s[d])
            s[d] = _load(d_ref, jj, 0) * s[d] + _load(c_ref, jj, 0)
    _store(s_scr, (0,), s[0])
    _store(s_scr, (1,), s[1])

    @pl.when(ci == nblk - 1)
    def _():
        _store(fin_ref, (0,), s[0])
        _store(fin_ref, (1,), s[1])


def _gla_log_decay(gl, g2_ref, gb_ref, d):
    x = dot3(gl[:, GLA_RANK * d:GLA_RANK * (d + 1)], g2_ref[d]) + gb_ref[d:d + 1, :]
    return -_softplus(-x) * (1.0 / GLA_TAU)


def _gla_local_kernel(f_ref, g2_ref, gb_ref, cst_ref, dec_ref):
    k = _load(f_ref, slice(None), slice(128, 256))
    v = _load(f_ref, slice(None), slice(256, 512))
    gl = _load(f_ref, slice(None), slice(768, 800))
    sel = (_iota((128, W_BR), 0) // GLA_DK) == (_iota((128, W_BR), 1) // HEAD)
    first_row = _iota((CH, 128), 0) == 0
    ones = jnp.ones((CH, W_BR), BF16)
    las = [_gla_log_decay(gl, g2_ref, gb_ref, d) for d in range(2)]
    css = [dot_xl(_tri_mask(CH, d == 1, False).astype(BF16), las[d]) for d in range(2)]
    tots = [css[0][CH - 1:CH], css[1][0:1]]
    kes = [k * _exp(tots[d] - css[d]) for d in range(2)]
    c_blk = [_where(sel, dot1(kes[d], v, "tn"), 0.0) for d in range(2)]
    d_blk = [dot_xr(_where(first_row, _exp(tots[d]), 0.0), ones, "tn") for d in range(2)]
    for d in range(2):
        _store(cst_ref, (0, d), _compact_blockdiag(c_blk[d], GLA_DK))
        _store(dec_ref, (0, d), _compact_blockdiag(d_blk[d], GLA_DK))


def _gla_local(f_gla, p):
    b, L, _ = f_gla.shape
    nc = L // CH
    nb = _nb(b, NB)
    st_shape = jax.ShapeDtypeStruct((b, nc, 2, GLA_DK, W_BR), F32)
    st_spec = pl.BlockSpec((nb, 1, 2, GLA_DK, W_BR), lambda i, c: (i, c, 0, 0, 0))
    return pl.pallas_call(
        _gla_local_kernel,
        grid=(b // nb, nc),
        in_specs=[pl.BlockSpec((nb, CH, W_GLA), lambda i, c: (i, c, 0)),
                  _full((2, GLA_RANK, 128)), _full((2, 128))],
        out_specs=[st_spec, st_spec],
        out_shape=[st_shape, st_shape],
        compiler_params=_params(("parallel", "parallel")),
        name="gla_local",
    )(f_gla, p["gla_g2"], p["gla_gb"])


def _gla_out_kernel(f_ref, sf_ref, sr_ref, g2_ref, gb_ref, norm_ref, o_ref):
    q = _load(f_ref, slice(None), slice(0, 128)) * (GLA_DK ** -0.5)
    k = _load(f_ref, slice(None), slice(128, 256))
    v = _load(f_ref, slice(None), slice(256, 512))
    gate = _load(f_ref, slice(None), slice(512, 768))
    gl = _load(f_ref, slice(None), slice(768, 800))
    khead = _iota((1, 128), 1) // GLA_DK
    vhead = _iota((1, W_BR), 1) // HEAD
    las = [_gla_log_decay(gl, g2_ref, gb_ref, d) for d in range(2)]
    css = [dot_xl(_tri_mask(CH, d == 1, False).astype(BF16), las[d]) for d in range(2)]
    mids = [css[d][CH // 2:CH // 2 + 1] for d in range(2)]
    qes = [q * _exp(css[d] - mids[d]) for d in range(2)]
    kns = [k * _exp(mids[d] - css[d]) for d in range(2)]
    sc = [[_where(_tri_mask(CH, d == 1, False), dot1(_where(khead == h, qes[d], 0.0), kns[d], "nt"), 0.0)
           for h in range(4)] for d in range(2)]
    inter = [dot1(q * _exp(css[d]), _expand_blockdiag(_load(s_ref, 0)))
             for d, s_ref in enumerate((sf_ref, sr_ref))]
    pv = [dot1(sc[0][h] + sc[1][h], _where(vhead == h, v, 0.0)) for h in range(4)]
    y = (inter[0] + inter[1]) + ((pv[0] + pv[1]) + (pv[2] + pv[3]))
    ms = dot_xr(y * y, _block_ones(W_BR, HEAD)) * (1.0 / HEAD)
    y = y * _rsqrt(ms + EPS) * norm_ref[...]
    _store(o_ref, (), y * _silu(gate))


def _gla_out(f_gla, sf, sr, p):
    b, L, _ = f_gla.shape
    nc = L // CH
    nb = _nb(b, NB)
    ent_spec = pl.BlockSpec((nb, 1, GLA_DK, W_BR), lambda i, c: (i, c, 0, 0))
    return pl.pallas_call(
        _gla_out_kernel,
        grid=(b // nb, nc),
        in_specs=[pl.BlockSpec((nb, CH, W_GLA), lambda i, c: (i, c, 0)), ent_spec, ent_spec,
                  _full((2, GLA_RANK, 128)), _full((2, 128)), _full((1, W_BR))],
        out_specs=pl.BlockSpec((nb, CH, W_BR), lambda i, c: (i, c, 0)),
        out_shape=jax.ShapeDtypeStruct((b, L, W_BR), F32),
        compiler_params=_params(("parallel", "parallel")),
        name="gla_out",
    )(f_gla, sf, sr, p["gla_g2"], p["gla_gb"], p["gla_norm"].reshape(1, W_BR))


def _ssd_conv(nc, cur_ref, prev_ref, next_ref, cw_ref, cb_ref):
    ci = pl.program_id(1)
    has_prev = (ci > 0).astype(F32)
    has_next = (ci < nc - 1).astype(F32)
    xbc = _load(cur_ref, slice(None), slice(0, 768))
    ext = _cat([_load(prev_ref) * has_prev, xbc, _load(next_ref) * has_next], 0)
    acc = cb_ref[...] + cw_ref[0:1, :] * ext[6:6 + CH]
    for j in range(1, SSD_CONV):
        acc = acc + cw_ref[j:j + 1, :] * ext[6 + j:6 + j + CH]
    return _silu(acc)


def _ssd_steps(dt_raw, dtb_ref, aneg_ref):
    dt = _softplus(dt_raw + dtb_ref[...])
    return dt, dt * aneg_ref[...]


def _ssd_cumsum(la):
    lane = _iota((1, 8), 1)
    la3 = _split3(la)
    tril, triu = (_tri_mask(CH, r, False).astype(BF16) for r in (False, True))
    cs_f = _dg(tril, la3[0], "nn") + (_dg(tril, la3[1], "nn") + _dg(tril, la3[2], "nn"))
    cs_r = _dg(triu, la3[0], "nn") + (_dg(triu, la3[1], "nn") + _dg(triu, la3[2], "nn"))
    cs = _where(lane < 4, cs_f, cs_r)
    return cs, _where(lane < 4, cs[CH - 1:CH], cs[0:1]), la3


def _ssd_local_kernel(nc, cur_ref, prev_ref, next_ref, cw_ref, cb_ref, dtb_ref, aneg_ref, cst_ref, dec_ref, xbc_ref):
    xbc = _ssd_conv(nc, cur_ref, prev_ref, next_ref, cw_ref, cb_ref)
    _store(xbc_ref, (), xbc)
    x, b_rep = xbc[:, 0:256], xbc[:, 256:512]
    dt, la = _ssd_steps(_load(cur_ref, slice(None), slice(1024, 1032)), dtb_ref, aneg_ref)
    cs, cs_tot, _ = _ssd_cumsum(la)
    wgt = _exp(cs_tot - cs) * dt
    sel = (_iota((W_BR, W_BR), 0) // HEAD) == (_iota((W_BR, W_BR), 1) // HEAD)
    e_tot = _exp(cs_tot)
    xs = [x * _expand_heads(wgt, 4 * d) for d in range(2)]
    c_blk = [_where(sel, dot1(b_rep, xs[d], "tn"), 0.0) for d in range(2)]
    for d in range(2):
        _store(cst_ref, (0, d), _compact_blockdiag(c_blk[d], HEAD))
        _store(dec_ref, (0, d), _bcast(_expand_heads(e_tot, 4 * d), (HEAD, W_BR)))


def _ssd_halo_specs(nb, nc):
    rb = CH // 8
    prev = pl.BlockSpec((nb, 8, 768), lambda b, c: (b, jnp.maximum(c * rb - 1, 0), 0))
    nxt = pl.BlockSpec((nb, 8, 768), lambda b, c: (b, jnp.minimum((c + 1) * rb, nc * rb - 1), 0))
    return prev, nxt


def _ssd_param_args(p):
    cw = jnp.concatenate([p["conv_w_rep"], jnp.zeros((8 - SSD_CONV, 768), F32)], axis=0)
    return cw, p["conv_b_rep"].reshape(1, 768), p["ssd_dt_bias"].reshape(1, 8), p["ssd_a_neg"].reshape(1, 8)


def _ssd_local(f_ssd, p):
    b, L, _ = f_ssd.shape
    nc = L // CH
    nb = _nb(b, NB)
    prev, nxt = _ssd_halo_specs(nb, nc)
    st_shape = jax.ShapeDtypeStruct((b, nc, 2, HEAD, W_BR), F32)
    st_spec = pl.BlockSpec((nb, 1, 2, HEAD, W_BR), lambda i, c: (i, c, 0, 0, 0))
    return pl.pallas_call(
        functools.partial(_ssd_local_kernel, nc),
        grid=(b // nb, nc),
        in_specs=[pl.BlockSpec((nb, CH, W_SSD), lambda i, c: (i, c, 0)), prev, nxt,
                  _full((8, 768)), _full((1, 768)), _full((1, 8)), _full((1, 8))],
        out_specs=[st_spec, st_spec, pl.BlockSpec((nb, CH, 768), lambda i, c: (i, c, 0))],
        out_shape=[st_shape, st_shape, jax.ShapeDtypeStruct((b, L, 768), F32)],
        compiler_params=_params(("parallel", "parallel")),
        name="ssd_local",
    )(f_ssd, f_ssd, f_ssd, *_ssd_param_args(p))


def _ssd_out_kernel(xbc_ref, tail_ref, dtb_ref, aneg_ref, sf_ref, sr_ref, dskip_ref, norm_ref, o_ref):
    x = _load(xbc_ref, slice(None), slice(0, 256))
    b_rep = _load(xbc_ref, slice(None), slice(256, 512))
    c_rep = _load(xbc_ref, slice(None), slice(512, 768))
    z = _load(tail_ref, slice(None), slice(0, 256))
    dt, la = _ssd_steps(_load(tail_ref, slice(None), slice(256, 264)), dtb_ref, aneg_ref)
    cs, _, la3 = _ssd_cumsum(la)
    lane_head = _iota((1, W_BR), 1) // HEAD
    tril, triu = (_tri_mask(CH, r, False).astype(BF16) for r in (False, True))
    cst_f = _dg(la3[0], triu, "tn") + (_dg(la3[1], triu, "tn") + _dg(la3[2], triu, "tn"))
    cst_r = _dg(la3[0], tril, "tn") + (_dg(la3[1], tril, "tn") + _dg(la3[2], tril, "tn"))
    cst = _where(_iota((8, 1), 0) < 4, cst_f, cst_r)
    e_cs = _exp(cs)
    g = [dot1(c_rep[:, 2 * HEAD * grp:2 * HEAD * grp + HEAD], b_rep[:, 2 * HEAD * grp:2 * HEAD * grp + HEAD], "nt")
         for grp in range(2)]
    inter = [dot1(c_rep * _expand_heads(e_cs, 4 * d), _expand_blockdiag(_load(s_ref, 0)))
             for d, s_ref in enumerate((sf_ref, sr_ref))]
    xs = [x * _expand_heads(dt, 4 * d) for d in range(2)]
    intra = []
    for d in range(2):
        incl = _tri_mask(CH, d == 1, False)
        for h in range(4):
            col = 4 * d + h
            seg = _exp(_where(incl, cs[:, col:col + 1] - cst[col:col + 1, :], -1e30))
            intra.append(dot1(g[h // 2] * seg, _where(lane_head == h, xs[d], 0.0)))
    y = dskip_ref[...] * x + (inter[0] + inter[1])
    y = y + (((intra[0] + intra[1]) + (intra[2] + intra[3])) + ((intra[4] + intra[5]) + (intra[6] + intra[7])))
    y = y * _silu(z)
    y = y * _rsqrt(_mean_last(y * y) + EPS) * norm_ref[...]
    _store(o_ref, (), y)


def _ssd_out(f_ssd, xbc, sf, sr, p):
    b, L, _ = f_ssd.shape
    nc = L // CH
    nb = _nb(b, NB)
    ent_spec = pl.BlockSpec((nb, 1, HEAD, W_BR), lambda i, c: (i, c, 0, 0))
    tail = W_SSD - 768
    return pl.pallas_call(
        _ssd_out_kernel,
        grid=(b // nb, nc),
        in_specs=[pl.BlockSpec((nb, CH, 768), lambda i, c: (i, c, 0)),
                  pl.BlockSpec((nb, CH, tail), lambda i, c: (i, c, 768 // tail)),
                  _full((1, 8)), _full((1, 8)), ent_spec, ent_spec, _full((1, W_BR)), _full((1, W_BR))],
        out_specs=pl.BlockSpec((nb, CH, W_BR), lambda i, c: (i, c, 0)),
        out_shape=jax.ShapeDtypeStruct((b, L, W_BR), F32),
        compiler_params=_params(("parallel", "parallel")),
        name="ssd_out",
    )(xbc, f_ssd, p["ssd_dt_bias"].reshape(1, 8), p["ssd_a_neg"].reshape(1, 8), sf, sr,
      jnp.repeat(p["ssd_d"], HEAD).reshape(1, W_BR), p["ssd_norm"].reshape(1, W_BR))


def _s5_weights(p):
    t = S5_T
    steps = jnp.arange(t + 1, dtype=F32)
    kmats, fmats, emats, lam_re, lam_im = [], [], [], [], []
    tt = jnp.arange(t)
    tau = tt[None, :] - tt[:, None]
    cmul = lambda x, y: (x[0] * y[0] - x[1] * y[1], x[0] * y[1] + x[1] * y[0])
    for d in range(2):
        a_re, a_im = p["s5_a_re"][d], p["s5_a_im"][d]
        dt = jnp.exp(p["s5_log_dt"][d])[:, None]
        mag = jnp.exp((a_re * dt)[None] * steps[:, None, None])
        ang = (a_im * dt)[None] * steps[:, None, None]
        pw = (mag * jnp.cos(ang), mag * jnp.sin(ang))
        num = (pw[0][1] - 1.0, pw[1][1])
        den = a_re * a_re + a_im * a_im
        quo = ((num[0] * a_re + num[1] * a_im) / den, (num[1] * a_re - num[0] * a_im) / den)
        b_bar = cmul((quo[0][..., None], quo[1][..., None]), (p["s5_b_re"][d], p["s5_b_im"][d]))
        c_c = (p["s5_c_re"][d], p["s5_c_im"][d])
        qb = cmul((pw[0][:t, :, :, None], pw[1][:t, :, :, None]), (b_bar[0][None], b_bar[1][None]))
        kd = jnp.einsum("gcp,jgpe->jgce", c_c[0], qb[0]) - jnp.einsum("gcp,jgpe->jgce", c_c[1], qb[1])
        lag = -tau if d else tau
        kfull = jnp.where((lag >= 0)[:, :, None, None, None], kd[jnp.clip(lag, 0, t - 1)], 0.0)
        kmats.append(jnp.transpose(kfull, (2, 0, 4, 1, 3)).reshape(S5_G, t * S5_CH, t * S5_CH))
        fsel = (lambda x: x) if d else (lambda x: x[::-1])
        fmats += [jnp.transpose(fsel(x), (1, 0, 3, 2)).reshape(S5_G, t * S5_CH, S5_P) for x in qb]
        esel = (lambda x: x[1:][::-1]) if d else (lambda x: x[1:])
        ec = cmul((c_c[0][None], c_c[1][None]), (esel(pw[0])[:, :, None, :], esel(pw[1])[:, :, None, :]))
        ec = [jnp.transpose(x, (1, 3, 0, 2)).reshape(S5_G, S5_P, t * S5_CH) for x in ec]
        emats += [ec[0], -ec[1]]
        lam_re += [pw[0][t], pw[0][t]]
        lam_im += [-pw[1][t], pw[1][t]]
    kmat = kmats[0] + kmats[1]
    fmat = jnp.concatenate(fmats, axis=2)
    emat = jnp.concatenate(emats, axis=1)
    return (kmat.astype(BF16), fmat.astype(BF16), emat.astype(BF16),
            jnp.concatenate(lam_re, axis=1), jnp.concatenate(lam_im, axis=1))


def _s5_local_kernel(u_ref, f_ref, z_ref):
    z_ref[0, 0] = _dg(_bf(u_ref[0, 0]), f_ref[0], "nn")


def _s5_local(ug, fmat):
    b, g, nc, w = ug.shape
    return pl.pallas_call(
        _s5_local_kernel,
        grid=(b, g),
        in_specs=[pl.BlockSpec((1, 1, nc, w), lambda i, j: (i, j, 0, 0)),
                  pl.BlockSpec((1, w, w), lambda i, j: (j, 0, 0))],
        out_specs=pl.BlockSpec((1, 1, nc, w), lambda i, j: (i, j, 0, 0)),
        out_shape=jax.ShapeDtypeStruct((b, g, nc, w), F32),
        compiler_params=_params(("parallel", "parallel")),
        name="s5_local",
    )(ug, fmat)


def _s5_scan_kernel(nblk, cb, zf_ref, zr_ref, lre_ref, lim_ref, init_ref, xf_ref, xr_ref, fin_ref, x_scr):
    ci = pl.program_id(0)

    @pl.when(ci == 0)
    def _():
        x_scr[...] = init_ref[...]

    lane = _iota((1, 256), 1)
    is_fwd = lane < 128
    even_q = (lane // S5_P) % 2 == 0
    lre, lim = lre_ref[...], lim_ref[...]
    x = _load(x_scr)
    for i in range(cb):
        for s in range(len(x.xs)):
            xf_ref[s, :, i, :] = x.xs[s]
            xr_ref[s, :, cb - 1 - i, :] = x.xs[s]
        z = _Lk([jnp.where(is_fwd, zf_ref[s, :, i, :], zr_ref[s, :, cb - 1 - i, :]) for s in range(len(x.xs))])
        swapped = _where(even_q, _roll(x, 192, 1), _roll(x, 64, 1))
        x = lre * x + lim * swapped + z
    _store(x_scr, (), x)

    @pl.when(ci == nblk - 1)
    def _():
        _store(fin_ref, (), x)


def _s5_scan(z, lam_re, lam_im, init):
    b, g, nc, w = z.shape
    cb = 8
    nblk = nc // cb
    fwd = pl.BlockSpec((b, g, cb, w), lambda j: (0, 0, j, 0))
    rev = pl.BlockSpec((b, g, cb, w), lambda j: (0, 0, nblk - 1 - j, 0))
    ent = jax.ShapeDtypeStruct((b, g, nc, w), F32)
    return pl.pallas_call(
        functools.partial(_s5_scan_kernel, nblk, cb),
        grid=(nblk,),
        in_specs=[fwd, rev, _full((g, w)), _full((g, w)), _full((b, g, w))],
        out_specs=[fwd, rev, _full((b, g, w))],
        out_shape=[ent, ent, jax.ShapeDtypeStruct((b, g, w), F32)],
        scratch_shapes=[pltpu.VMEM((b, g, w), F32)],
        compiler_params=_params(("arbitrary",)),
        name="s5_scan",
    )(z, z, lam_re, lam_im, init)


def _s5_out_kernel(u_ref, xf_ref, xr_ref, k_ref, e_ref, y_ref):
    lane = _iota((1, 256), 1)
    x_ent = jnp.where(lane < 128, xf_ref[0, 0], xr_ref[0, 0])
    x_hi, x_lo = _split2(x_ent)
    y_ref[0, 0] = _dg(_bf(u_ref[0, 0]), k_ref[0], "nn") + (_dg(x_hi, e_ref[0], "nn") + _dg(x_lo, e_ref[0], "nn"))


def _s5_out(ug, xf, xr, kmat, emat):
    b, g, nc, w = ug.shape
    row = pl.BlockSpec((1, 1, nc, w), lambda i, j: (i, j, 0, 0))
    mat = pl.BlockSpec((1, w, w), lambda i, j: (j, 0, 0))
    return pl.pallas_call(
        _s5_out_kernel,
        grid=(b, g),
        in_specs=[row, row, row, mat, mat],
        out_specs=row,
        out_shape=jax.ShapeDtypeStruct((b, g, nc, w), F32),
        compiler_params=_params(("parallel", "parallel")),
        name="s5_out",
    )(ug, xf, xr, kmat, emat)


def _outproj_kernel(h_ref, mod_ref, yrw_ref, ys5_ref, fs5_ref, yssd_ref, ygla_ref, s5d_ref, gluw_ref, glub_ref,
                    w_ref, g_ref, o_ref):
    d = h_ref.shape[2]
    u = fs5_ref[0, :, 0:256]
    y = ys5_ref[0] + s5d_ref[...] * u
    y = 0.5 * y * (1.0 + jnp.tanh(math.sqrt(2.0 / math.pi) * (y + 0.044715 * (y * y * y))))
    y = y * _sigmoid(dot1(y, gluw_ref[...]) + glub_ref[...])
    y_s5 = y * _silu(fs5_ref[0, :, 256:512])
    ycat = jnp.concatenate([yrw_ref[0], y_s5, yssd_ref[0], ygla_ref[0]], axis=1)
    o = dot1(ycat, w_ref[...])
    o = o * lax.rsqrt(jnp.mean(o * o, axis=-1, keepdims=True) + EPS) * g_ref[...]
    o_ref[0] = h_ref[0] + mod_ref[0][:, 2 * d:3 * d] * o


def _outproj(h, mod3, mod_row, y_rw, y_s5, f_s5, y_ssd, y_gla, p, tm):
    b, L, d = h.shape
    row = lambda w: pl.BlockSpec((1, tm, w), lambda i, j: (i, j, 0))
    return pl.pallas_call(
        _outproj_kernel,
        grid=(b, L // tm),
        in_specs=[row(d), pl.BlockSpec((1, 1, 3 * d), lambda i, j: (mod_row(i), 0, 0)),
                  row(W_BR), row(W_BR), row(W_S5), row(W_BR), row(W_BR),
                  _full((1, W_BR)), _full((W_BR, W_BR)), _full((1, W_BR)), _full((d, d)), _full((1, d))],
        out_specs=row(d),
        out_shape=jax.ShapeDtypeStruct((b, L, d), F32),
        compiler_params=_params(("parallel", "parallel")),
        name="out_proj",
    )(h, mod3, y_rw, y_s5, f_s5, y_ssd, y_gla, p["s5_d"].reshape(1, W_BR), p["s5_glu_w"].astype(BF16),
      p["s5_glu_b"].reshape(1, W_BR), p["w_out"].astype(BF16), p["norm_post"].reshape(1, d))


def _permute_w_in(w_in):
    rw = w_in[:, 0:1280]
    s5 = w_in[:, 1280:1792]
    o = 1792
    x, bm, cm = w_in[:, o:o + 256], w_in[:, o + 256:o + 384], w_in[:, o + 384:o + 512]
    dt, z = w_in[:, o + 512:o + 520], w_in[:, o + 520:o + 776]
    rep = lambda m: jnp.concatenate([m[:, 0:64], m[:, 0:64], m[:, 64:128], m[:, 64:128]], axis=1)
    pad = lambda n: jnp.zeros((w_in.shape[0], n), w_in.dtype)
    ssd = jnp.concatenate([x, rep(bm), rep(cm), z, dt, pad(120)], axis=1)
    o = 1792 + 776
    q, k, v = w_in[:, o:o + 128], w_in[:, o + 128:o + 256], w_in[:, o + 256:o + 512]
    gl, gate = w_in[:, o + 512:o + 544], w_in[:, o + 544:o + 800]
    gla = jnp.concatenate([q, k, v, gate, gl, pad(96)], axis=1)
    return jnp.concatenate([rw, s5, ssd, gla], axis=1).astype(BF16)


def _layer_params(l, a):
    p = {k: v[l] for k, v in a.items()}
    rep = lambda m: jnp.concatenate([m[..., 0:64], m[..., 0:64], m[..., 64:128], m[..., 64:128]], axis=-1)
    cw, cb = p["ssd_conv_w"], p["ssd_conv_b"]
    p["conv_w_rep"] = jnp.concatenate([cw[:, 0:256], rep(cw[:, 256:384]), rep(cw[:, 384:512])], axis=1)
    p["conv_b_rep"] = jnp.concatenate([cb[0:256], rep(cb[256:384]), rep(cb[384:512])], axis=0)
    p["ssd_a_neg"] = -jnp.exp(p["ssd_a_log"])
    p["w_in_p"] = _permute_w_in(p["w_in"])
    p["s5_ops"] = _s5_weights(p)
    return p


def _mixer_layer(h, mod3, mod_row, p, is_grid, init, need_out):
    b, L, d = h.shape
    tm = min(256, L)
    f_rw, f_s5, f_ssd, f_gla = _inproj(h, mod3, mod_row, p["norm_pre"], p["w_in_p"], tm)
    if init is None:
        init = (jnp.zeros((b, 2, HEAD, W_BR), F32), jnp.zeros((b, S5_G, 256), F32),
                jnp.zeros((b, 2, HEAD, W_BR), F32), jnp.zeros((b, 2, GLA_DK, W_BR), F32))
    rt, y0, bonus, m_rw, c_rw = _rw_local(f_rw, p, is_grid)
    rw_sf, rw_sr, rw_fin = _scan_call(_rw_scan_kernel, "rwkv_scan", m_rw, c_rw, init[0])
    kmat, fmat, emat, lam_re, lam_im = p["s5_ops"]
    nc5 = L // S5_T
    ug = f_s5[:, :, 0:256].reshape(b, nc5, S5_T, S5_G, S5_CH).transpose(0, 3, 1, 2, 4).reshape(b, S5_G, nc5, 256)
    z5 = _s5_local(ug, fmat)
    s5_xf, s5_xr, s5_fin = _s5_scan(z5, lam_re, lam_im, init[1])
    ssd_c, ssd_d, ssd_xbc = _ssd_local(f_ssd, p)
    ssd_sf, ssd_sr, ssd_fin = _scan_call(_ew_scan_kernel, "ssd_scan", ssd_c, ssd_d, init[2])
    gla_c, gla_d = _gla_local(f_gla, p)
    gla_sf, gla_sr, gla_fin = _scan_call(_ew_scan_kernel, "gla_scan", gla_c, gla_d, init[3])
    states = (rw_fin, s5_fin, ssd_fin, gla_fin)
    if not need_out:
        return None, states
    y_rw = _rw_out(y0, bonus, rt, rw_sf, rw_sr, f_rw, p)
    y5 = _s5_out(ug, s5_xf, s5_xr, kmat, emat)
    y5 = y5.reshape(b, S5_G, nc5, S5_T, S5_CH).transpose(0, 2, 3, 1, 4).reshape(b, L, W_BR)
    y_ssd = _ssd_out(f_ssd, ssd_xbc, ssd_sf, ssd_sr, p)
    y_gla = _gla_out(f_gla, gla_sf, gla_sr, p)
    return _outproj(h, mod3, mod_row, y_rw, y5, f_s5, y_ssd, y_gla, p, tm), states


def kernel(x, c, ctx, c_ctx, ada_w, ada_b, norm_pre, norm_post, w_in, w_out, rw_mu, rw_w0, rw_w2, rw_a0, rw_a2, rw_kk, rw_ka, rw_rk, rw_ln_w, rw_ln_b, s5_a_re, s5_a_im, s5_log_dt, s5_b_re, s5_b_im, s5_c_re, s5_c_im, s5_d, s5_glu_w, s5_glu_b, ssd_conv_w, ssd_conv_b, ssd_dt_bias, ssd_a_log, ssd_d, ssd_norm, gla_g2, gla_gb, gla_norm):
    stacked = dict(norm_pre=norm_pre, norm_post=norm_post, w_in=w_in, w_out=w_out, rw_mu=rw_mu, rw_w0=rw_w0,
                   rw_w2=rw_w2, rw_a0=rw_a0, rw_a2=rw_a2, rw_kk=rw_kk, rw_ka=rw_ka, rw_rk=rw_rk, rw_ln_w=rw_ln_w,
                   rw_ln_b=rw_ln_b, s5_a_re=s5_a_re, s5_a_im=s5_a_im, s5_log_dt=s5_log_dt, s5_b_re=s5_b_re,
                   s5_b_im=s5_b_im, s5_c_re=s5_c_re, s5_c_im=s5_c_im, s5_d=s5_d, s5_glu_w=s5_glu_w,
                   s5_glu_b=s5_glu_b, ssd_conv_w=ssd_conv_w, ssd_conv_b=ssd_conv_b, ssd_dt_bias=ssd_dt_bias,
                   ssd_a_log=ssd_a_log, ssd_d=ssd_d, ssd_norm=ssd_norm, gla_g2=gla_g2, gla_gb=gla_gb,
                   gla_norm=gla_norm)
    depth = ada_w.shape[0]
    b, d = c.shape
    rows = -(-(b + 1) // 8) * 8
    cvec = jnp.concatenate([c, c_ctx[None, :], jnp.zeros((rows - b - 1, d), F32)], axis=0)
    mod = _modulation(cvec, ada_w, ada_b)
    h, hc = x, ctx
    for l in range(depth):
        p = _layer_params(l, stacked)
        mod3 = mod[l].reshape(rows, 1, 3 * d)
        last = l == depth - 1
        hc_next, ctx_states = _mixer_layer(hc, mod3, lambda i: b, p, False, None, not last)
        h, _ = _mixer_layer(h, mod3, lambda i: i, p, True, ctx_states, True)
        hc = hc_next
    return h
```

```python
import functools
import math
import operator

import jax
import jax.numpy as jnp
from jax import lax
from jax.experimental import pallas as pl
from jax.experimental.pallas import tpu as pltpu

F32 = jnp.float32
BF16 = jnp.bfloat16

EPS = 1e-6
GRID_W = 64
W_BR = 256
CH = 64
HEAD = 64
RW_GN_EPS = 64e-5
RW_LORA = 64
S5_G, S5_CH, S5_P, S5_T = 16, 16, 64, 16
SSD_CONV = 5
GLA_RANK = 16
GLA_DK = 32
GLA_TAU = 16.0
W_RW, W_S5, W_SSD, W_GLA = 1280, 512, 1152, 896
N_PROJ = W_RW + W_S5 + W_SSD + W_GLA
VMEM_LIMIT = 48 * 1024 * 1024
NB_RW = 8
NB = 4
SCAN_CB = 8
SCAN_NB = 2


class _Lk:
    def __init__(self, xs):
        self.xs = list(xs)

    @property
    def shape(self):
        return self.xs[0].shape

    def __getitem__(self, idx):
        return _Lk([x[idx] for x in self.xs])

    def astype(self, dt):
        return _Lk([x.astype(dt) for x in self.xs])


def _pick(a, i):
    if isinstance(a, _Lk):
        return a.xs[i]
    if isinstance(a, (list, tuple)):
        return [_pick(e, i) for e in a]
    return a


def _width(a):
    if isinstance(a, _Lk):
        return len(a.xs)
    if isinstance(a, (list, tuple)):
        for e in a:
            w = _width(e)
            if w:
                return w
    return 0


def _lift(f):
    def g(*args, **kw):
        n = max(_width(a) for a in args)
        if not n:
            return f(*args, **kw)
        return _Lk([f(*[_pick(a, i) for a in args], **kw) for i in range(n)])
    return g


for _name, _op in (("add", operator.add), ("sub", operator.sub), ("mul", operator.mul), ("truediv", operator.truediv)):
    setattr(_Lk, f"__{_name}__", lambda s, o, _f=_lift(_op): _f(s, o))
    setattr(_Lk, f"__r{_name}__", lambda s, o, _f=_lift(_op): _f(o, s))
_Lk.__neg__ = lambda s: _lift(operator.neg)(s)

_exp = _lift(jnp.exp)
_log = _lift(jnp.log)
_tanh = _lift(jnp.tanh)
_abs = _lift(jnp.abs)
_maximum = _lift(jnp.maximum)
_rsqrt = _lift(lax.rsqrt)
_where = _lift(jnp.where)
_cat = _lift(lambda parts, axis: jnp.concatenate(parts, axis=axis))
_roll = _lift(pltpu.roll)
_bcast = _lift(jnp.broadcast_to)
_mean_last = _lift(lambda x: jnp.mean(x, axis=-1, keepdims=True))


def _load(ref, *idx):
    return _Lk([ref[(s,) + idx] for s in range(ref.shape[0])])


def _store(ref, idx, val):
    for s in range(ref.shape[0]):
        ref[(s,) + idx] = val.xs[s]


_dg = _lift(lambda a, b, mode: lax.dot_general(
    a, b, ((({"nn": 1, "nt": 1, "tn": 0}[mode],), ({"nn": 0, "nt": 1, "tn": 0}[mode],)), ((), ())),
    preferred_element_type=F32))


def _bf(a):
    return a.astype(BF16)


def _split2(a):
    hi = _bf(a)
    return hi, _bf(a - hi.astype(F32))


def _split3(a):
    hi = _bf(a)
    r = a - hi.astype(F32)
    mid = _bf(r)
    return hi, mid, _bf(r - mid.astype(F32))


def _hilo(a):
    hi = _bf(a).astype(F32)
    return hi, a - hi


def dot1(a, b, mode="nn"):
    return _dg(_bf(a), _bf(b), mode)


def dot3(a, b, mode="nn"):
    ah, al = _split2(a)
    bh, bl = _split2(b)
    return _dg(ah, bh, mode) + (_dg(ah, bl, mode) + _dg(al, bh, mode))


def dotk3(a, b, mode="nn"):
    ah, al = _hilo(a)
    bh, bl = _hilo(b)
    ax = 0 if mode == "tn" else 1
    bx = 1 if mode == "nt" else 0
    return _dg(_bf(_cat([ah, al, ah], ax)), _bf(_cat([bh, bh, bl], bx)), mode)


def dot_xl(a_exact, b, mode="nn"):
    b1, b2, b3 = _split3(b)
    return _dg(a_exact, b1, mode) + (_dg(a_exact, b2, mode) + _dg(a_exact, b3, mode))


def dot_xl2(a_exact, b, mode="nn"):
    b1, b2 = _split2(b)
    return _dg(a_exact, b1, mode) + _dg(a_exact, b2, mode)


def dot_xr(a, b_exact, mode="nn"):
    a1, a2, a3 = _split3(a)
    return _dg(a1, b_exact, mode) + (_dg(a2, b_exact, mode) + _dg(a3, b_exact, mode))


def _sigmoid(x):
    return 1.0 / (1.0 + _exp(-x))


def _silu(x):
    return x * _sigmoid(x)


def _softplus(x):
    return _maximum(x, 0.0) + _log(1.0 + _exp(-_abs(x)))


def _iota(shape, dim):
    return lax.broadcasted_iota(jnp.int32, shape, dim)


def _tri_mask(n, rev, strict):
    r, c = _iota((n, n), 0), _iota((n, n), 1)
    if rev:
        return (r < c) if strict else (r <= c)
    return (r > c) if strict else (r >= c)


def _block_ones(n, blk):
    return (_iota((n, n), 0) // blk == _iota((n, n), 1) // blk).astype(BF16)


def _expand_blockdiag(compact, n_heads=4):
    lane_head = _iota((1, compact.shape[1]), 1) // HEAD
    return _cat([_where(lane_head == h, compact, 0.0) for h in range(n_heads)], 0)


def _compact_blockdiag(blk, rows_per_head, n_heads=4):
    lane_head = _iota((1, blk.shape[1]), 1) // HEAD
    out = _where(lane_head == 0, blk[0:rows_per_head], 0.0)
    for h in range(1, n_heads):
        out = out + _where(lane_head == h, blk[h * rows_per_head:(h + 1) * rows_per_head], 0.0)
    return out


def _expand_heads(cols, first, n_heads=4):
    lane_head = _iota((1, n_heads * HEAD), 1) // HEAD
    out = _where(lane_head == 0, cols[:, first:first + 1], 0.0)
    for h in range(1, n_heads):
        out = _where(lane_head == h, cols[:, first + h:first + h + 1], out)
    return out


def _shift_rows(x, edge_row, down):
    n = x.shape[0]
    rows = _iota((n, 1), 0)
    if down:
        return _where(rows == 0, edge_row, _roll(x, 1, 0))
    return _where(rows == n - 1, edge_row, _roll(x, n - 1, 0))


def _params(sem):
    return pltpu.CompilerParams(dimension_semantics=sem, vmem_limit_bytes=VMEM_LIMIT)


def _full(shape):
    nd = len(shape)
    return pl.BlockSpec(shape, lambda *_: (0,) * nd)


def _nb(b, want):
    return want if b % want == 0 else 1


def _mod_kernel(c_ref, w_ref, b_ref, o_ref):
    s = _silu(c_ref[...])
    o_ref[0] = dot3(s, w_ref[0]) + b_ref[0]


def _modulation(cvec, ada_w, ada_b):
    depth, d, n = ada_w.shape
    rows = cvec.shape[0]
    tn = 1024
    return pl.pallas_call(
        _mod_kernel,
        grid=(depth, n // tn),
        in_specs=[pl.BlockSpec((rows, d), lambda l, j: (0, 0)),
                  pl.BlockSpec((1, d, tn), lambda l, j: (l, 0, j)),
                  pl.BlockSpec((1, 1, tn), lambda l, j: (l, 0, j))],
        out_specs=pl.BlockSpec((1, rows, tn), lambda l, j: (l, 0, j)),
        out_shape=jax.ShapeDtypeStruct((depth, rows, n), F32),
        compiler_params=_params(("parallel", "parallel")),
        name="adaln_mod",
    )(cvec, ada_w, ada_b.reshape(depth, 1, n))


def _inproj_kernel(x_ref, mod_ref, g_ref, w_ref, o_rw, o_s5, o_ssd, o_gla):
    x = x_ref[0]
    d = x.shape[1]
    hn = x * lax.rsqrt(jnp.mean(x * x, axis=-1, keepdims=True) + EPS) * g_ref[...]
    m = mod_ref[0]
    hn = hn * (1.0 + m[:, d:2 * d]) + m[:, 0:d]
    p = dot1(hn, w_ref[...])
    o_rw[0] = p[:, 0:W_RW]
    o_s5[0] = p[:, W_RW:W_RW + W_S5]
    o_ssd[0] = p[:, W_RW + W_S5:W_RW + W_S5 + W_SSD]
    o_gla[0] = p[:, W_RW + W_S5 + W_SSD:N_PROJ]


def _inproj(h, mod3, mod_row, norm_pre, w_in_p, tm):
    b, L, d = h.shape
    widths = (W_RW, W_S5, W_SSD, W_GLA)
    return pl.pallas_call(
        _inproj_kernel,
        grid=(b, L // tm),
        in_specs=[pl.BlockSpec((1, tm, d), lambda i, j: (i, j, 0)),
                  pl.BlockSpec((1, 1, 3 * d), lambda i, j: (mod_row(i), 0, 0)),
                  _full((1, d)),
                  _full(w_in_p.shape)],
        out_specs=[pl.BlockSpec((1, tm, w), lambda i, j: (i, j, 0)) for w in widths],
        out_shape=[jax.ShapeDtypeStruct((b, L, w), F32) for w in widths],
        compiler_params=_params(("parallel", "parallel")),
        name="in_proj",
    )(h, mod3, norm_pre.reshape(1, d), w_in_p)


def _tri_inv_stages(n_mats):
    r, c = _iota((CH, CH), 0), _iota((CH, CH), 1)
    blk16 = (r // 16) == (c // 16)
    blk32 = (r // 32) == (c // 32)
    eye = jnp.where(r == c, 1.0, 0.0)
    nd = [_where(blk16, n, 0.0) for n in n_mats]
    d = [eye + x for x in nd]
    p = [dot1(x, x) for x in nd]
    yield "h"
    for _ in range(2):
        d = [x + dot1(x, y) for x, y in zip(d, p)]
        p = [dot1(y, y) for y in p]
        yield "h"
    d = [x + dot1(x, y) for x, y in zip(d, p)]
    yield "h"
    for sel in (blk32 & jnp.logical_not(blk16), jnp.logical_not(blk32)):
        od = [dot1(_where(sel, n, 0.0), x) for n, x in zip(n_mats, d)]
        yield "h"
        d = [x + dot1(x, y) for x, y in zip(d, od)]
        yield "h"
    return d


def _run_staggered(gens):
    pending, active, started = list(gens), [], set()
    active.append(pending.pop(0))
    while active:
        for g in list(active):
            try:
                tag = next(g)
            except StopIteration:
                active.remove(g)
                continue
            if tag == "h" and pending and id(g) not in started:
                started.add(id(g))
                active.append(pending.pop(0))


def _rw_local_kernel(is_grid, nc, cur_ref, *refs):
    nb = cur_ref.shape[0]
    groups = [range(s, min(s + 2, nb)) for s in range(0, nb, 2)]
    _run_staggered([_rw_local_stages(g, is_grid, nc, cur_ref, *refs) for g in groups])


def _rw_local_stages(members, is_grid, nc, cur_ref, prev_ref, next_ref, mu_ref, w0_ref, w2_ref, a0_ref, a2_ref,
                     kk_ref, ka_ref, rk_ref, rt_ref, y0_ref, bonus_ref, m_ref, c_ref):
    _load = lambda ref, *idx: _Lk([ref[(s,) + idx] for s in members])

    def _store(ref, idx, val):
        for j, s in enumerate(members):
            ref[(s,) + idx] = val.xs[j]

    ci = pl.program_id(1)
    has_prev = (ci > 0).astype(F32)
    has_next = (ci < nc - 1).astype(F32)
    z = _load(cur_ref, slice(None), slice(0, 1024))
    if is_grid:
        zero_row = jnp.zeros((1, W_BR), F32)
        left = _shift_rows(z[:, 0:256], zero_row, True)
        right = _shift_rows(z[:, 256:512], zero_row, False)
        up = _load(prev_ref) * has_prev
        down = _load(next_ref) * has_next
        sh = _cat([left, right, up, down], 1)
    else:
        prow = _load(prev_ref, slice(7, 8)) * has_prev
        nrow = _load(next_ref, slice(0, 1)) * has_next
        sh = _cat([_shift_rows(z[:, 0:512], prow, True), _shift_rows(z[:, 512:1024], nrow, False)], 1)
    zs = z + mu_ref[...] * (sh - z)
    yield "p"
    r, k, v = zs[:, 0:256], zs[:, 256:512], zs[:, 512:768]
    wl, al = zs[:, 768:896], zs[:, 896:1024]

    bo = _block_ones(W_BR, HEAD)
    kk = k * kk_ref[...]
    kk = kk * _rsqrt(dot1(kk * kk, bo) + EPS)
    yield "p"
    eye = _iota((HEAD, HEAD), 0) == _iota((HEAD, HEAD), 1)
    heads = lambda t: [t[:, HEAD * h:HEAD * (h + 1)] for h in range(4)]
    v_h = heads(v)
    row2, col2 = _iota((CH, 2 * CH), 0), _iota((CH, 2 * CH), 1) % CH

    bonus = None
    pa_l, pam_l, prm_l, pr_l, nb_l, nk_l, eb_l, ek_l, dg_l, strict_l, incl2_l = ([] for _ in range(11))
    for d in range(2):
        rev = d == 1
        w_pre = w0_ref[d:d + 1, :] + dot1(_tanh(wl[:, RW_LORA * d:RW_LORA * (d + 1)]), w2_ref[d])
        lw = (-math.exp(-0.5)) * _sigmoid(w_pre)
        a = _sigmoid(a0_ref[d:d + 1, :] + dot1(al[:, RW_LORA * d:RW_LORA * (d + 1)], a2_ref[d]))
        k_d = k * (1.0 + (a - 1.0) * ka_ref[...])
        bterm = dot1(r * k_d * rk_ref[...], bo) * v
        bonus = bterm if bonus is None else bonus + bterm
        yield "p"

        cs = dot_xl2(_tri_mask(CH, rev, False).astype(BF16), lw)
        cs_ex = cs - lw
        cs_tot = cs[0:1] if rev else cs[CH - 1:CH]
        cs_mid = cs[CH // 2:CH // 2 + 1]
        kka = kk * a
        e_mid = _exp(cs_mid - cs)
        e_tot = _exp(cs_tot - cs)
        yield "p"
        pa_l += heads(-kk * _exp(cs_ex))
        pr_l += heads(r * _exp(cs))
        yield "p"
        pam_l += heads(-kk * _exp(cs_ex - cs_mid))
        prm_l += heads(r * _exp(cs - cs_mid))
        yield "p"
        nb_l += heads(kka * e_mid)
        nk_l += heads(k_d * e_mid)
        yield "p"
        eb_l += heads(kka * e_tot)
        ek_l += heads(k_d * e_tot)
        dg_l += [_where(eye, t, 0.0) for t in heads(_exp(cs_tot))]
        strict_l += [_tri_mask(CH, rev, True)] * 4
        incl2_l += [(row2 <= col2) if rev else (row2 >= col2)] * 4
        yield "p"
    vv = v_h + v_h
    idx = range(8)

    quad = [dot1(_cat([pam_l[i], prm_l[i]], 0), _cat([nb_l[i], nk_l[i]], 0), "nt") for i in idx]
    yield "h"
    n_l = [_where(strict_l[i], quad[i][0:CH, 0:CH], 0.0) for i in idx]
    aak = [_where(strict_l[i], quad[i][0:CH, CH:2 * CH], 0.0) for i in idx]
    g2 = [_where(incl2_l[i], quad[i][CH:2 * CH], 0.0) for i in idx]
    t_inv = yield from _tri_inv_stages(n_l)
    av = [dot1(aak[i], vv[i]) for i in idx]
    yield "h"
    wu = [dot1(t_inv[i], _cat([pa_l[i], av[i]], 1)) for i in idx]
    yield "h"
    zero = jnp.zeros((CH, HEAD), F32)
    rhs2 = [_cat([wu[i], _cat([zero, vv[i]], 1)], 0) for i in idx]
    gy = [dot1(g2[i], rhs2[i]) for i in idx]
    yield "h"
    mc = [dot1(_cat([eb_l[i], ek_l[i]], 0), rhs2[i], "tn") for i in idx]
    yield "h"
    y0_parts = []
    for d in range(2):
        sl = range(4 * d, 4 * d + 4)
        _store(rt_ref, (0, d), _cat([pr_l[i] + gy[i][:, 0:HEAD] for i in sl], 1))
        _store(m_ref, (0, d), _cat([dg_l[i] + mc[i][:, 0:HEAD] for i in sl], 1))
        _store(c_ref, (0, d), _cat([mc[i][:, HEAD:2 * HEAD] for i in sl], 1))
        y0_parts.append(_cat([gy[i][:, HEAD:2 * HEAD] for i in sl], 1))
    _store(y0_ref, (), y0_parts[0] + y0_parts[1])
    _store(bonus_ref, (), bonus)


def _halo_specs(nb, is_grid, nc):
    if is_grid:
        prev = pl.BlockSpec((nb, CH, W_BR), lambda b, c: (b, jnp.maximum(c - 1, 0), 2))
        nxt = pl.BlockSpec((nb, CH, W_BR), lambda b, c: (b, jnp.minimum(c + 1, nc - 1), 3))
    else:
        rb = CH // 8
        prev = pl.BlockSpec((nb, 8, 512), lambda b, c: (b, jnp.maximum(c * rb - 1, 0), 0))
        nxt = pl.BlockSpec((nb, 8, 512), lambda b, c: (b, jnp.minimum((c + 1) * rb, nc * rb - 1), 1))
    return prev, nxt


def _rw_local(f_rw, p, is_grid):
    b, L, _ = f_rw.shape
    nc = L // CH
    nb = _nb(b, NB_RW)
    prev, nxt = _halo_specs(nb, is_grid, nc)
    state_shape = jax.ShapeDtypeStruct((b, nc, 2, HEAD, W_BR), F32)
    state_spec = pl.BlockSpec((nb, 1, 2, HEAD, W_BR), lambda i, c: (i, c, 0, 0, 0))
    row_spec = pl.BlockSpec((nb, CH, W_BR), lambda i, c: (i, c, 0))
    row_shape = jax.ShapeDtypeStruct((b, L, W_BR), F32)
    return pl.pallas_call(
        functools.partial(_rw_local_kernel, is_grid, nc),
        grid=(b // nb, nc),
        in_specs=[pl.BlockSpec((nb, CH, W_RW), lambda i, c: (i, c, 0)), prev, nxt,
                  _full((1, 1024)), _full((2, W_BR)), _full((2, RW_LORA, W_BR)), _full((2, W_BR)),
                  _full((2, RW_LORA, W_BR)), _full((1, W_BR)), _full((1, W_BR)), _full((1, W_BR))],
        out_specs=[state_spec, row_spec, row_spec, state_spec, state_spec],
        out_shape=[state_shape, row_shape, row_shape, state_shape, state_shape],
        compiler_params=_params(("parallel", "parallel")),
        name="rwkv_local",
    )(f_rw, f_rw, f_rw, p["rw_mu"].reshape(1, 1024), p["rw_w0"], p["rw_w2"], p["rw_a0"], p["rw_a2"],
      p["rw_kk"].reshape(1, W_BR), p["rw_ka"].reshape(1, W_BR), p["rw_rk"].reshape(1, W_BR))


def _scan_specs(nb, cb, nblk, rows, width):
    in_f = pl.BlockSpec((nb, cb, 1, rows, width), lambda i, j: (i, j, 0, 0, 0))
    in_r = pl.BlockSpec((nb, cb, 1, rows, width), lambda i, j: (i, nblk - 1 - j, 1, 0, 0))
    out_f = pl.BlockSpec((nb, cb, rows, width), lambda i, j: (i, j, 0, 0))
    out_r = pl.BlockSpec((nb, cb, rows, width), lambda i, j: (i, nblk - 1 - j, 0, 0))
    st = pl.BlockSpec((nb, 2, rows, width), lambda i, j: (i, 0, 0, 0))
    return in_f, in_r, out_f, out_r, st


def _rw_scan_kernel(nblk, cb, mf_ref, mr_ref, cf_ref, cr_ref, init_ref, sf_ref, sr_ref, fin_ref, s_scr):
    ci = pl.program_id(1)

    @pl.when(ci == 0)
    def _():
        s_scr[...] = init_ref[...]

    heads = lambda t: [t[:, HEAD * h:HEAD * (h + 1)] for h in range(4)]
    s = [_load(s_scr, 0), _load(s_scr, 1)]
    for j in range(cb):
        new = []
        for d, (m_ref, c_ref, o_ref) in enumerate(((mf_ref, cf_ref, sf_ref), (mr_ref, cr_ref, sr_ref))):
            jj = cb - 1 - j if d else j
            _store(o_ref, (jj,), s[d])
            m_h = heads(_load(m_ref, jj, 0))
            s_h = heads(s[d])
            prod = [dotk3(m_h[h], s_h[h]) for h in range(4)]
            new.append(_cat(prod, 1) + _load(c_ref, jj, 0))
        s = new
    _store(s_scr, (0,), s[0])
    _store(s_scr, (1,), s[1])

    @pl.when(ci == nblk - 1)
    def _():
        _store(fin_ref, (0,), s[0])
        _store(fin_ref, (1,), s[1])


def _scan_call(kernel_fn, name, a, b_arr, init):
    b, nc, _, rows, width = a.shape
    nb = _nb(b, SCAN_NB)
    cb = SCAN_CB if nc % SCAN_CB == 0 else nc
    nblk = nc // cb
    in_f, in_r, out_f, out_r, st = _scan_specs(nb, cb, nblk, rows, width)
    ent_shape = jax.ShapeDtypeStruct((b, nc, rows, width), F32)
    return pl.pallas_call(
        functools.partial(kernel_fn, nblk, cb),
        grid=(b // nb, nblk),
        in_specs=[in_f, in_r, in_f, in_r, st],
        out_specs=[out_f, out_r, st],
        out_shape=[ent_shape, ent_shape, jax.ShapeDtypeStruct((b, 2, rows, width), F32)],
        scratch_shapes=[pltpu.VMEM((nb, 2, rows, width), F32)],
        compiler_params=_params(("parallel", "arbitrary")),
        name=name,
    )(a, a, b_arr, b_arr, init)


def _rw_out_value(y0_ref, bonus_ref, rt_ref, sf_ref, sr_ref, gate_ref, lnw_ref, lnb_ref):
    y = _load(y0_ref)
    y = y + dot1(_load(rt_ref, 0, 0), _expand_blockdiag(_load(sf_ref, 0)))
    y = y + dot1(_load(rt_ref, 0, 1), _expand_blockdiag(_load(sr_ref, 0)))
    bo = _block_ones(W_BR, HEAD)
    y_hi, y_lo = _split2(y)
    mu = (_dg(y_hi, bo, "nn") + _dg(y_lo, bo, "nn")) * (1.0 / HEAD)
    yc = y - mu
    var = dot1(yc * yc, bo) * (1.0 / HEAD)
    yn = yc * _rsqrt(var + RW_GN_EPS) * lnw_ref[...] + lnb_ref[...]
    return (yn + _load(bonus_ref)) * _silu(_load(gate_ref))


def _ew_scan_kernel(nblk, cb, cf_ref, cr_ref, df_ref, dr_ref, init_ref, sf_ref, sr_ref, fin_ref, s_scr):
    ci = pl.program_id(1)

    @pl.when(ci == 0)
    def _():
        s_scr[...] = init_ref[...]

    s = [_load(s_scr, 0), _load(s_scr, 1)]
    for j in range(cb):
        for d, (c_ref, d_ref, o_ref) in enumerate(((cf_ref, df_ref, sf_ref), (cr_ref, dr_ref, sr_ref))):
            jj = cb - 1 - j if d else j
            _store(o_ref, (jj,), s[d])
            s[d] = _load(d_ref, jj, 0) * s[d] + _load(c_ref, jj, 0)
    _store(s_scr, (0,), s[0])
    _store(s_scr, (1,), s[1])

    @pl.when(ci == nblk - 1)
    def _():
        _store(fin_ref, (0,), s[0])
        _store(fin_ref, (1,), s[1])


def _gla_log_decay(gl, g2_ref, gb_ref, d):
    x = dot3(gl[:, GLA_RANK * d:GLA_RANK * (d + 1)], g2_ref[d]) + gb_ref[d:d + 1, :]
    return -_softplus(-x) * (1.0 / GLA_TAU)


def _gla_local_kernel(f_ref, g2_ref, gb_ref, cst_ref, dec_ref):
    k = _load(f_ref, slice(None), slice(128, 256))
    v = _load(f_ref, slice(None), slice(256, 512))
    gl = _load(f_ref, slice(None), slice(768, 800))
    sel = (_iota((128, W_BR), 0) // GLA_DK) == (_iota((128, W_BR), 1) // HEAD)
    first_row = _iota((CH, 128), 0) == 0
    ones = jnp.ones((CH, W_BR), BF16)
    las = [_gla_log_decay(gl, g2_ref, gb_ref, d) for d in range(2)]
    css = [dot_xl(_tri_mask(CH, d == 1, False).astype(BF16), las[d]) for d in range(2)]
    tots = [css[0][CH - 1:CH], css[1][0:1]]
    kes = [k * _exp(tots[d] - css[d]) for d in range(2)]
    c_blk = [_where(sel, dot1(kes[d], v, "tn"), 0.0) for d in range(2)]
    d_blk = [dot_xr(_where(first_row, _exp(tots[d]), 0.0), ones, "tn") for d in range(2)]
    for d in range(2):
        _store(cst_ref, (0, d), _compact_blockdiag(c_blk[d], GLA_DK))
        _store(dec_ref, (0, d), _compact_blockdiag(d_blk[d], GLA_DK))


def _gla_local(f_gla, p):
    b, L, _ = f_gla.shape
    nc = L // CH
    nb = _nb(b, NB)
    st_shape = jax.ShapeDtypeStruct((b, nc, 2, GLA_DK, W_BR), F32)
    st_spec = pl.BlockSpec((nb, 1, 2, GLA_DK, W_BR), lambda i, c: (i, c, 0, 0, 0))
    return pl.pallas_call(
        _gla_local_kernel,
        grid=(b // nb, nc),
        in_specs=[pl.BlockSpec((nb, CH, W_GLA), lambda i, c: (i, c, 0)),
                  _full((2, GLA_RANK, 128)), _full((2, 128))],
        out_specs=[st_spec, st_spec],
        out_shape=[st_shape, st_shape],
        compiler_params=_params(("parallel", "parallel")),
        name="gla_local",
    )(f_gla, p["gla_g2"], p["gla_gb"])


def _gla_out_value(f_ref, sf_ref, sr_ref, g2_ref, gb_ref, norm_ref):
    q = _load(f_ref, slice(None), slice(0, 128)) * (GLA_DK ** -0.5)
    k = _load(f_ref, slice(None), slice(128, 256))
    v = _load(f_ref, slice(None), slice(256, 512))
    gate = _load(f_ref, slice(None), slice(512, 768))
    gl = _load(f_ref, slice(None), slice(768, 800))
    khead = _iota((1, 128), 1) // GLA_DK
    vhead = _iota((1, W_BR), 1) // HEAD
    las = [_gla_log_decay(gl, g2_ref, gb_ref, d) for d in range(2)]
    css = [dot_xl(_tri_mask(CH, d == 1, False).astype(BF16), las[d]) for d in range(2)]
    mids = [css[d][CH // 2:CH // 2 + 1] for d in range(2)]
    qes = [q * _exp(css[d] - mids[d]) for d in range(2)]
    kns = [k * _exp(mids[d] - css[d]) for d in range(2)]
    sc = [[_where(_tri_mask(CH, d == 1, False), dot1(_where(khead == h, qes[d], 0.0), kns[d], "nt"), 0.0)
           for h in range(4)] for d in range(2)]
    inter = [dot1(q * _exp(css[d]), _expand_blockdiag(_load(s_ref, 0)))
             for d, s_ref in enumerate((sf_ref, sr_ref))]
    pv = [dot1(sc[0][h] + sc[1][h], _where(vhead == h, v, 0.0)) for h in range(4)]
    y = (inter[0] + inter[1]) + ((pv[0] + pv[1]) + (pv[2] + pv[3]))
    ms = dot1(y * y, _block_ones(W_BR, HEAD)) * (1.0 / HEAD)
    y = y * _rsqrt(ms + EPS) * norm_ref[...]
    return y * _silu(gate)


def _ssd_conv(nc, cur_ref, prev_ref, next_ref, cw_ref, cb_ref):
    ci = pl.program_id(1)
    has_prev = (ci > 0).astype(F32)
    has_next = (ci < nc - 1).astype(F32)
    xbc = _load(cur_ref, slice(None), slice(0, 768))
    ext = _cat([_load(prev_ref) * has_prev, xbc, _load(next_ref) * has_next], 0)
    acc = cb_ref[...] + cw_ref[0:1, :] * ext[6:6 + CH]
    for j in range(1, SSD_CONV):
        acc = acc + cw_ref[j:j + 1, :] * ext[6 + j:6 + j + CH]
    return _silu(acc)


def _ssd_steps(dt_raw, dtb_ref, aneg_ref):
    dt = _softplus(dt_raw + dtb_ref[...])
    return dt, dt * aneg_ref[...]


def _ssd_cumsum(la):
    lane = _iota((1, 8), 1)
    la3 = _split3(la)
    tril, triu = (_tri_mask(CH, r, False).astype(BF16) for r in (False, True))
    cs_f = _dg(tril, la3[0], "nn") + (_dg(tril, la3[1], "nn") + _dg(tril, la3[2], "nn"))
    cs_r = _dg(triu, la3[0], "nn") + (_dg(triu, la3[1], "nn") + _dg(triu, la3[2], "nn"))
    cs = _where(lane < 4, cs_f, cs_r)
    return cs, _where(lane < 4, cs[CH - 1:CH], cs[0:1]), la3


def _ssd_local_kernel(nc, cur_ref, prev_ref, next_ref, cw_ref, cb_ref, dtb_ref, aneg_ref, cst_ref, dec_ref, xbc_ref):
    xbc = _ssd_conv(nc, cur_ref, prev_ref, next_ref, cw_ref, cb_ref)
    _store(xbc_ref, (), xbc)
    x, b_rep = xbc[:, 0:256], xbc[:, 256:512]
    dt, la = _ssd_steps(_load(cur_ref, slice(None), slice(1024, 1032)), dtb_ref, aneg_ref)
    cs, cs_tot, _ = _ssd_cumsum(la)
    wgt = _exp(cs_tot - cs) * dt
    sel = (_iota((W_BR, W_BR), 0) // HEAD) == (_iota((W_BR, W_BR), 1) // HEAD)
    e_tot = _exp(cs_tot)
    xs = [x * _expand_heads(wgt, 4 * d) for d in range(2)]
    c_blk = [_where(sel, dot1(b_rep, xs[d], "tn"), 0.0) for d in range(2)]
    for d in range(2):
        _store(cst_ref, (0, d), _compact_blockdiag(c_blk[d], HEAD))
        _store(dec_ref, (0, d), _bcast(_expand_heads(e_tot, 4 * d), (HEAD, W_BR)))


def _ssd_halo_specs(nb, nc):
    rb = CH // 8
    prev = pl.BlockSpec((nb, 8, 768), lambda b, c: (b, jnp.maximum(c * rb - 1, 0), 0))
    nxt = pl.BlockSpec((nb, 8, 768), lambda b, c: (b, jnp.minimum((c + 1) * rb, nc * rb - 1), 0))
    return prev, nxt


def _ssd_param_args(p):
    cw = jnp.concatenate([p["conv_w_rep"], jnp.zeros((8 - SSD_CONV, 768), F32)], axis=0)
    return cw, p["conv_b_rep"].reshape(1, 768), p["ssd_dt_bias"].reshape(1, 8), p["ssd_a_neg"].reshape(1, 8)


def _ssd_local(f_ssd, p):
    b, L, _ = f_ssd.shape
    nc = L // CH
    nb = _nb(b, NB)
    prev, nxt = _ssd_halo_specs(nb, nc)
    st_shape = jax.ShapeDtypeStruct((b, nc, 2, HEAD, W_BR), F32)
    st_spec = pl.BlockSpec((nb, 1, 2, HEAD, W_BR), lambda i, c: (i, c, 0, 0, 0))
    return pl.pallas_call(
        functools.partial(_ssd_local_kernel, nc),
        grid=(b // nb, nc),
        in_specs=[pl.BlockSpec((nb, CH, W_SSD), lambda i, c: (i, c, 0)), prev, nxt,
                  _full((8, 768)), _full((1, 768)), _full((1, 8)), _full((1, 8))],
        out_specs=[st_spec, st_spec, pl.BlockSpec((nb, CH, 768), lambda i, c: (i, c, 0))],
        out_shape=[st_shape, st_shape, jax.ShapeDtypeStruct((b, L, 768), F32)],
        compiler_params=_params(("parallel", "parallel")),
        name="ssd_local",
    )(f_ssd, f_ssd, f_ssd, *_ssd_param_args(p))


def _ssd_out_value(xbc_ref, tail_ref, dtb_ref, aneg_ref, sf_ref, sr_ref, dskip_ref, norm_ref):
    x = _load(xbc_ref, slice(None), slice(0, 256))
    b_rep = _load(xbc_ref, slice(None), slice(256, 512))
    c_rep = _load(xbc_ref, slice(None), slice(512, 768))
    z = _load(tail_ref, slice(None), slice(0, 256))
    dt, la = _ssd_steps(_load(tail_ref, slice(None), slice(256, 264)), dtb_ref, aneg_ref)
    cs, _, la3 = _ssd_cumsum(la)
    lane_head = _iota((1, W_BR), 1) // HEAD
    tril, triu = (_tri_mask(CH, r, False).astype(BF16) for r in (False, True))
    cst_f = _dg(la3[0], triu, "tn") + (_dg(la3[1], triu, "tn") + _dg(la3[2], triu, "tn"))
    cst_r = _dg(la3[0], tril, "tn") + (_dg(la3[1], tril, "tn") + _dg(la3[2], tril, "tn"))
    cst = _where(_iota((8, 1), 0) < 4, cst_f, cst_r)
    e_cs = _exp(cs)
    g = [dot1(c_rep[:, 2 * HEAD * grp:2 * HEAD * grp + HEAD], b_rep[:, 2 * HEAD * grp:2 * HEAD * grp + HEAD], "nt")
         for grp in range(2)]
    inter = [dot1(c_rep * _expand_heads(e_cs, 4 * d), _expand_blockdiag(_load(s_ref, 0)))
             for d, s_ref in enumerate((sf_ref, sr_ref))]
    xs = [x * _expand_heads(dt, 4 * d) for d in range(2)]
    intra = []
    for d in range(2):
        incl = _tri_mask(CH, d == 1, False)
        for h in range(4):
            col = 4 * d + h
            seg = _exp(_where(incl, cs[:, col:col + 1] - cst[col:col + 1, :], -1e30))
            intra.append(dot1(g[h // 2] * seg, _where(lane_head == h, xs[d], 0.0)))
    y = dskip_ref[...] * x + (inter[0] + inter[1])
    y = y + (((intra[0] + intra[1]) + (intra[2] + intra[3])) + ((intra[4] + intra[5]) + (intra[6] + intra[7])))
    y = y * _silu(z)
    return y * _rsqrt(_mean_last(y * y) + EPS) * norm_ref[...]


def _s5_weights(p):
    t = S5_T
    steps = jnp.arange(t + 1, dtype=F32)
    kmats, fmats, emats, lam_re, lam_im = [], [], [], [], []
    tt = jnp.arange(t)
    tau = tt[None, :] - tt[:, None]
    cmul = lambda x, y: (x[0] * y[0] - x[1] * y[1], x[0] * y[1] + x[1] * y[0])
    for d in range(2):
        a_re, a_im = p["s5_a_re"][d], p["s5_a_im"][d]
        dt = jnp.exp(p["s5_log_dt"][d])[:, None]
        mag = jnp.exp((a_re * dt)[None] * steps[:, None, None])
        ang = (a_im * dt)[None] * steps[:, None, None]
        pw = (mag * jnp.cos(ang), mag * jnp.sin(ang))
        num = (pw[0][1] - 1.0, pw[1][1])
        den = a_re * a_re + a_im * a_im
        quo = ((num[0] * a_re + num[1] * a_im) / den, (num[1] * a_re - num[0] * a_im) / den)
        b_bar = cmul((quo[0][..., None], quo[1][..., None]), (p["s5_b_re"][d], p["s5_b_im"][d]))
        c_c = (p["s5_c_re"][d], p["s5_c_im"][d])
        qb = cmul((pw[0][:t, :, :, None], pw[1][:t, :, :, None]), (b_bar[0][None], b_bar[1][None]))
        kd = jnp.einsum("gcp,jgpe->jgce", c_c[0], qb[0]) - jnp.einsum("gcp,jgpe->jgce", c_c[1], qb[1])
        lag = -tau if d else tau
        kfull = jnp.where((lag >= 0)[:, :, None, None, None], kd[jnp.clip(lag, 0, t - 1)], 0.0)
        kmats.append(jnp.transpose(kfull, (2, 0, 4, 1, 3)).reshape(S5_G, t * S5_CH, t * S5_CH))
        fsel = (lambda x: x) if d else (lambda x: x[::-1])
        fmats += [jnp.transpose(fsel(x), (1, 0, 3, 2)).reshape(S5_G, t * S5_CH, S5_P) for x in qb]
        esel = (lambda x: x[1:][::-1]) if d else (lambda x: x[1:])
        ec = cmul((c_c[0][None], c_c[1][None]), (esel(pw[0])[:, :, None, :], esel(pw[1])[:, :, None, :]))
        ec = [jnp.transpose(x, (1, 3, 0, 2)).reshape(S5_G, S5_P, t * S5_CH) for x in ec]
        emats += [ec[0], -ec[1]]
        lam_re += [pw[0][t], pw[0][t]]
        lam_im += [-pw[1][t], pw[1][t]]
    kmat = kmats[0] + kmats[1]
    fmat = jnp.concatenate(fmats, axis=2)
    emat = jnp.concatenate(emats, axis=1)
    return (kmat.astype(BF16), fmat.astype(BF16), emat.astype(BF16),
            jnp.concatenate(lam_re, axis=1), jnp.concatenate(lam_im, axis=1))


def _s5_local_kernel(ua_ref, ub_ref, f_ref, ug_ref, z_ref):
    rb = z_ref.shape[2]
    step_rows = lambda s: jnp.concatenate([ua_ref[0, pl.ds(s, rb, stride=S5_T), :],
                                           ub_ref[0, pl.ds(s, rb, stride=S5_T), :]], axis=1)
    ug = _block_transpose16([step_rows(s) for s in range(S5_T)])
    ub = [_bf(x) for x in ug]
    zs = [_dg(ub[g], f_ref[g], "nn") for g in range(S5_G)]
    for g in range(S5_G):
        ug_ref[0, g] = ub[g]
        z_ref[0, g] = zs[g]


def _block_transpose16(arrs):
    lane_blk = _iota((1, 256), 1) // 16
    arrs = list(arrs)
    for k in range(4):
        b = 1 << k
        bit = (lane_blk & b) != 0
        new = list(arrs)
        for i in range(16):
            if not i & b:
                lo, hi = arrs[i], arrs[i | b]
                new[i] = jnp.where(bit, pltpu.roll(hi, 16 * b, 1), lo)
                new[i | b] = jnp.where(bit, hi, pltpu.roll(lo, 256 - 16 * b, 1))
        arrs = new
    return arrs


def _s5_rows(nc):
    return 64 if nc % 64 == 0 else nc


def _s5_local(f_s5, fmat):
    b, L, _ = f_s5.shape
    nc = L // S5_T
    rb = _s5_rows(nc)
    grp = pl.BlockSpec((1, S5_G, rb, 256), lambda i, j: (i, 0, j, 0))
    return pl.pallas_call(
        _s5_local_kernel,
        grid=(b, nc // rb),
        in_specs=[pl.BlockSpec((1, rb * S5_T, 128), lambda i, j: (i, j, 0)),
                  pl.BlockSpec((1, rb * S5_T, 128), lambda i, j: (i, j, 1)), _full(fmat.shape)],
        out_specs=[grp, grp],
        out_shape=[jax.ShapeDtypeStruct((b, S5_G, nc, 256), BF16), jax.ShapeDtypeStruct((b, S5_G, nc, 256), F32)],
        compiler_params=_params(("parallel", "parallel")),
        name="s5_local",
    )(f_s5, f_s5, fmat)


def _s5_scan_kernel(nblk, cb, zf_ref, zr_ref, lre_ref, lim_ref, init_ref, xf_ref, xr_ref, fin_ref, x_scr):
    ci = pl.program_id(0)

    @pl.when(ci == 0)
    def _():
        x_scr[...] = init_ref[...]

    lane = _iota((1, 256), 1)
    is_fwd = lane < 128
    even_q = (lane // S5_P) % 2 == 0
    lre, lim = lre_ref[...], lim_ref[...]
    x = _load(x_scr)
    for i in range(cb):
        for s in range(len(x.xs)):
            xf_ref[s, :, i, :] = x.xs[s]
            xr_ref[s, :, cb - 1 - i, :] = x.xs[s]
        z = _Lk([jnp.where(is_fwd, zf_ref[s, :, i, :], zr_ref[s, :, cb - 1 - i, :]) for s in range(len(x.xs))])
        swapped = _where(even_q, _roll(x, 192, 1), _roll(x, 64, 1))
        x = lre * x + lim * swapped + z
    _store(x_scr, (), x)

    @pl.when(ci == nblk - 1)
    def _():
        _store(fin_ref, (), x)


def _s5_scan(z, lam_re, lam_im, init):
    b, g, nc, w = z.shape
    cb = 8
    nblk = nc // cb
    fwd = pl.BlockSpec((b, g, cb, w), lambda j: (0, 0, j, 0))
    rev = pl.BlockSpec((b, g, cb, w), lambda j: (0, 0, nblk - 1 - j, 0))
    ent = jax.ShapeDtypeStruct((b, g, nc, w), F32)
    return pl.pallas_call(
        functools.partial(_s5_scan_kernel, nblk, cb),
        grid=(nblk,),
        in_specs=[fwd, rev, _full((g, w)), _full((g, w)), _full((b, g, w))],
        out_specs=[fwd, rev, _full((b, g, w))],
        out_shape=[ent, ent, jax.ShapeDtypeStruct((b, g, w), F32)],
        scratch_shapes=[pltpu.VMEM((b, g, w), F32)],
        compiler_params=_params(("arbitrary",)),
        name="s5_scan",
    )(z, z, lam_re, lam_im, init)


def _s5_out_kernel(ug_ref, xf_ref, xr_ref, k_ref, e_ref, ya_ref, yb_ref):
    rb = ug_ref.shape[2]
    lane = _iota((1, 256), 1)
    groups = range(S5_G)
    x_ent = [_split2(jnp.where(lane < 128, xf_ref[0, g], xr_ref[0, g])) for g in groups]
    ys = [_dg(ug_ref[0, g], k_ref[g], "nn") + (_dg(x_ent[g][0], e_ref[g], "nn") + _dg(x_ent[g][1], e_ref[g], "nn"))
          for g in groups]
    yt = _block_transpose16(ys)
    for s in range(S5_T):
        ya_ref[0, pl.ds(s, rb, stride=S5_T), :] = yt[s][:, 0:128]
        yb_ref[0, pl.ds(s, rb, stride=S5_T), :] = yt[s][:, 128:256]


def _s5_out(ug, xf, xr, kmat, emat):
    b, g, nc, w = ug.shape
    rb = _s5_rows(nc)
    grp = pl.BlockSpec((1, g, rb, w), lambda i, j: (i, 0, j, 0))
    half = pl.BlockSpec((1, rb * S5_T, 128), lambda i, j: (i, j, 0))
    half_shape = jax.ShapeDtypeStruct((b, nc * S5_T, 128), F32)
    return pl.pallas_call(
        _s5_out_kernel,
        grid=(b, nc // rb),
        in_specs=[grp, grp, grp, _full(kmat.shape), _full(emat.shape)],
        out_specs=[half, half],
        out_shape=[half_shape, half_shape],
        compiler_params=_params(("parallel", "parallel")),
        name="s5_out",
    )(ug, xf, xr, kmat, emat)


def _s5_out_value(ys5a_ref, ys5b_ref, fs5_ref, s5d_ref, gluw_ref, glub_ref):
    u = _load(fs5_ref, slice(None), slice(0, 256))
    y = _cat([_load(ys5a_ref), _load(ys5b_ref)], 1) + s5d_ref[...] * u
    y = 0.5 * y * (1.0 + _tanh(math.sqrt(2.0 / math.pi) * (y + 0.044715 * (y * y * y))))
    y = y * _sigmoid(dot1(y, gluw_ref[...]) + glub_ref[...])
    return y * _silu(_load(fs5_ref, slice(None), slice(256, 512)))


def _layer_out_kernel(h_ref, mod_ref, w_ref, g_ref, *refs):
    rw_refs, s5_refs, ssd_refs, gla_refs, o_ref = refs[0:8], refs[8:14], refs[14:22], refs[22:28], refs[28]
    nb, rows, d = h_ref.shape
    ycat = _cat([_rw_out_value(*rw_refs), _s5_out_value(*s5_refs), _ssd_out_value(*ssd_refs),
                 _gla_out_value(*gla_refs)], 1)
    o = dot1(jnp.concatenate(ycat.xs, axis=0), w_ref[...])
    o = o * lax.rsqrt(jnp.mean(o * o, axis=-1, keepdims=True) + EPS) * g_ref[...]
    for s in range(nb):
        gate = mod_ref[min(s, mod_ref.shape[0] - 1)][:, 2 * d:3 * d]
        o_ref[s] = h_ref[s] + gate * o[s * rows:(s + 1) * rows]


def _layer_out(h, mod3, shared_mod_row, rw, s5, ssd, gla, p):
    b, L, d = h.shape
    nc = L // CH
    nb = _nb(b, NB)
    row = lambda w, col=0: pl.BlockSpec((nb, CH, w), lambda i, c: (i, c, col))
    ent = lambda r: pl.BlockSpec((nb, 1, r, W_BR), lambda i, c: (i, c, 0, 0))
    vec = lambda w: _full((1, w))
    if shared_mod_row is None:
        mod_spec = pl.BlockSpec((nb, 1, 3 * d), lambda i, c: (i, 0, 0))
    else:
        mod_spec = pl.BlockSpec((1, 1, 3 * d), lambda i, c: (shared_mod_row, 0, 0))
    tail = W_SSD - 768
    y0, bonus, rt, rw_sf, rw_sr, f_rw = rw
    in_specs = [row(d), mod_spec, _full((d, d)), vec(d),
                row(W_BR), row(W_BR), pl.BlockSpec((nb, 1, 2, HEAD, W_BR), lambda i, c: (i, c, 0, 0, 0)),
                ent(HEAD), ent(HEAD), row(W_BR, 4), vec(W_BR), vec(W_BR),
                row(128), row(128), row(W_S5), vec(W_BR), _full((W_BR, W_BR)), vec(W_BR),
                row(768), row(tail, 768 // tail), vec(8), vec(8), ent(HEAD), ent(HEAD), vec(W_BR), vec(W_BR),
                row(W_GLA), ent(GLA_DK), ent(GLA_DK), _full((2, GLA_RANK, 128)), _full((2, 128)), vec(W_BR)]
    args = [h, mod3, p["w_out"].astype(BF16), p["norm_post"].reshape(1, d),
            y0, bonus, rt, rw_sf, rw_sr, f_rw, p["rw_ln_w"].reshape(1, W_BR), p["rw_ln_b"].reshape(1, W_BR),
            s5[0], s5[1], s5[2], p["s5_d"].reshape(1, W_BR), p["s5_glu_w"].astype(BF16),
            p["s5_glu_b"].reshape(1, W_BR),
            ssd[0], ssd[1], p["ssd_dt_bias"].reshape(1, 8), p["ssd_a_neg"].reshape(1, 8), ssd[2], ssd[3],
            jnp.repeat(p["ssd_d"], HEAD).reshape(1, W_BR), p["ssd_norm"].reshape(1, W_BR),
            gla[0], gla[1], gla[2], p["gla_g2"], p["gla_gb"], p["gla_norm"].reshape(1, W_BR)]
    return pl.pallas_call(
        _layer_out_kernel,
        grid=(b // nb, nc),
        in_specs=in_specs,
        out_specs=row(d),
        out_shape=jax.ShapeDtypeStruct((b, L, d), F32),
        compiler_params=_params(("parallel", "parallel")),
        name="layer_out",
    )(*args)


def _permute_w_in(w_in):
    rw = w_in[:, 0:1280]
    s5 = w_in[:, 1280:1792]
    o = 1792
    x, bm, cm = w_in[:, o:o + 256], w_in[:, o + 256:o + 384], w_in[:, o + 384:o + 512]
    dt, z = w_in[:, o + 512:o + 520], w_in[:, o + 520:o + 776]
    rep = lambda m: jnp.concatenate([m[:, 0:64], m[:, 0:64], m[:, 64:128], m[:, 64:128]], axis=1)
    pad = lambda n: jnp.zeros((w_in.shape[0], n), w_in.dtype)
    ssd = jnp.concatenate([x, rep(bm), rep(cm), z, dt, pad(120)], axis=1)
    o = 1792 + 776
    q, k, v = w_in[:, o:o + 128], w_in[:, o + 128:o + 256], w_in[:, o + 256:o + 512]
    gl, gate = w_in[:, o + 512:o + 544], w_in[:, o + 544:o + 800]
    gla = jnp.concatenate([q, k, v, gate, gl, pad(96)], axis=1)
    return jnp.concatenate([rw, s5, ssd, gla], axis=1).astype(BF16)


def _layer_params(l, a):
    p = {k: v[l] for k, v in a.items()}
    rep = lambda m: jnp.concatenate([m[..., 0:64], m[..., 0:64], m[..., 64:128], m[..., 64:128]], axis=-1)
    cw, cb = p["ssd_conv_w"], p["ssd_conv_b"]
    p["conv_w_rep"] = jnp.concatenate([cw[:, 0:256], rep(cw[:, 256:384]), rep(cw[:, 384:512])], axis=1)
    p["conv_b_rep"] = jnp.concatenate([cb[0:256], rep(cb[256:384]), rep(cb[384:512])], axis=0)
    p["ssd_a_neg"] = -jnp.exp(p["ssd_a_log"])
    p["w_in_p"] = _permute_w_in(p["w_in"])
    p["s5_ops"] = _s5_weights(p)
    return p


def _mixer_layer(h, mod3, shared_mod_row, p, is_grid, init, need_out):
    b, L, d = h.shape
    tm = min(256, L)
    mod_row = (lambda i: i) if shared_mod_row is None else (lambda i: shared_mod_row)
    f_rw, f_s5, f_ssd, f_gla = _inproj(h, mod3, mod_row, p["norm_pre"], p["w_in_p"], tm)
    if init is None:
        init = (jnp.zeros((b, 2, HEAD, W_BR), F32), jnp.zeros((b, S5_G, 256), F32),
                jnp.zeros((b, 2, HEAD, W_BR), F32), jnp.zeros((b, 2, GLA_DK, W_BR), F32))
    rt, y0, bonus, m_rw, c_rw = _rw_local(f_rw, p, is_grid)
    rw_sf, rw_sr, rw_fin = _scan_call(_rw_scan_kernel, "rwkv_scan", m_rw, c_rw, init[0])
    kmat, fmat, emat, lam_re, lam_im = p["s5_ops"]
    ug, z5 = _s5_local(f_s5, fmat)
    s5_xf, s5_xr, s5_fin = _s5_scan(z5, lam_re, lam_im, init[1])
    ssd_c, ssd_d, ssd_xbc = _ssd_local(f_ssd, p)
    ssd_sf, ssd_sr, ssd_fin = _scan_call(_ew_scan_kernel, "ssd_scan", ssd_c, ssd_d, init[2])
    gla_c, gla_d = _gla_local(f_gla, p)
    gla_sf, gla_sr, gla_fin = _scan_call(_ew_scan_kernel, "gla_scan", gla_c, gla_d, init[3])
    states = (rw_fin, s5_fin, ssd_fin, gla_fin)
    if not need_out:
        return None, states
    y5a, y5b = _s5_out(ug, s5_xf, s5_xr, kmat, emat)
    h_new = _layer_out(h, mod3, shared_mod_row,
                       (y0, bonus, rt, rw_sf, rw_sr, f_rw), (y5a, y5b, f_s5),
                       (ssd_xbc, f_ssd, ssd_sf, ssd_sr), (f_gla, gla_sf, gla_sr), p)
    return h_new, states


def kernel(x, c, ctx, c_ctx, ada_w, ada_b, norm_pre, norm_post, w_in, w_out, rw_mu, rw_w0, rw_w2, rw_a0, rw_a2, rw_kk, rw_ka, rw_rk, rw_ln_w, rw_ln_b, s5_a_re, s5_a_im, s5_log_dt, s5_b_re, s5_b_im, s5_c_re, s5_c_im, s5_d, s5_glu_w, s5_glu_b, ssd_conv_w, ssd_conv_b, ssd_dt_bias, ssd_a_log, ssd_d, ssd_norm, gla_g2, gla_gb, gla_norm):
    stacked = dict(norm_pre=norm_pre, norm_post=norm_post, w_in=w_in, w_out=w_out, rw_mu=rw_mu, rw_w0=rw_w0,
                   rw_w2=rw_w2, rw_a0=rw_a0, rw_a2=rw_a2, rw_kk=rw_kk, rw_ka=rw_ka, rw_rk=rw_rk, rw_ln_w=rw_ln_w,
                   rw_ln_b=rw_ln_b, s5_a_re=s5_a_re, s5_a_im=s5_a_im, s5_log_dt=s5_log_dt, s5_b_re=s5_b_re,
                   s5_b_im=s5_b_im, s5_c_re=s5_c_re, s5_c_im=s5_c_im, s5_d=s5_d, s5_glu_w=s5_glu_w,
                   s5_glu_b=s5_glu_b, ssd_conv_w=ssd_conv_w, ssd_conv_b=ssd_conv_b, ssd_dt_bias=ssd_dt_bias,
                   ssd_a_log=ssd_a_log, ssd_d=ssd_d, ssd_norm=ssd_norm, gla_g2=gla_g2, gla_gb=gla_gb,
                   gla_norm=gla_norm)
    depth = ada_w.shape[0]
    b, d = c.shape
    rows = -(-(b + 1) // 8) * 8
    cvec = jnp.concatenate([c, c_ctx[None, :], jnp.zeros((rows - b - 1, d), F32)], axis=0)
    mod = _modulation(cvec, ada_w, ada_b)
    h, hc = x, ctx
    for l in range(depth):
        p = _layer_params(l, stacked)
        mod3 = mod[l].reshape(rows, 1, 3 * d)
        last = l == depth - 1
        hc_next, ctx_states = _mixer_layer(hc, mod3, b, p, False, None, not last)
        h, _ = _mixer_layer(h, mod3, None, p, True, ctx_states, True)
        hc = hc_next
    return h
```

```python
import functools
import math
import operator

import jax
import jax.numpy as jnp
from jax import lax
from jax.experimental import pallas as pl
from jax.experimental.pallas import tpu as pltpu

F32 = jnp.float32
BF16 = jnp.bfloat16

EPS = 1e-6
GRID_W = 64
W_BR = 256
CH = 64
HEAD = 64
RW_GN_EPS = 64e-5
RW_LORA = 64
S5_G, S5_CH, S5_P, S5_T = 16, 16, 64, 16
SSD_CONV = 5
GLA_RANK = 16
GLA_DK = 32
GLA_TAU = 16.0
W_RW, W_S5, W_SSD, W_GLA = 1280, 512, 1152, 896
N_PROJ = W_RW + W_S5 + W_SSD + W_GLA
VMEM_LIMIT = 48 * 1024 * 1024
NB_RW = 8
NB = 4
SCAN_CB = 8
SCAN_NB = 2


class _Lk:
    def __init__(self, xs):
        self.xs = list(xs)

    @property
    def shape(self):
        return self.xs[0].shape

    def __getitem__(self, idx):
        return _Lk([x[idx] for x in self.xs])

    def astype(self, dt):
        return _Lk([x.astype(dt) for x in self.xs])


def _pick(a, i):
    if isinstance(a, _Lk):
        return a.xs[i]
    if isinstance(a, (list, tuple)):
        return [_pick(e, i) for e in a]
    return a


def _width(a):
    if isinstance(a, _Lk):
        return len(a.xs)
    if isinstance(a, (list, tuple)):
        for e in a:
            w = _width(e)
            if w:
                return w
    return 0


def _lift(f):
    def g(*args, **kw):
        n = max(_width(a) for a in args)
        if not n:
            return f(*args, **kw)
        return _Lk([f(*[_pick(a, i) for a in args], **kw) for i in range(n)])
    return g


for _name, _op in (("add", operator.add), ("sub", operator.sub), ("mul", operator.mul), ("truediv", operator.truediv)):
    setattr(_Lk, f"__{_name}__", lambda s, o, _f=_lift(_op): _f(s, o))
    setattr(_Lk, f"__r{_name}__", lambda s, o, _f=_lift(_op): _f(o, s))
_Lk.__neg__ = lambda s: _lift(operator.neg)(s)

_exp = _lift(jnp.exp)
_log = _lift(jnp.log)
_tanh = _lift(jnp.tanh)
_abs = _lift(jnp.abs)
_maximum = _lift(jnp.maximum)
_rsqrt = _lift(lax.rsqrt)
_where = _lift(jnp.where)
_cat = _lift(lambda parts, axis: jnp.concatenate(parts, axis=axis))
_roll = _lift(pltpu.roll)
_bcast = _lift(jnp.broadcast_to)
_mean_last = _lift(lambda x: jnp.mean(x, axis=-1, keepdims=True))


def _load(ref, *idx):
    return _Lk([ref[(s,) + idx] for s in range(ref.shape[0])])


def _store(ref, idx, val):
    for s in range(ref.shape[0]):
        ref[(s,) + idx] = val.xs[s]


_dg = _lift(lambda a, b, mode: lax.dot_general(
    a, b, ((({"nn": 1, "nt": 1, "tn": 0}[mode],), ({"nn": 0, "nt": 1, "tn": 0}[mode],)), ((), ())),
    preferred_element_type=F32))


def _bf(a):
    return a.astype(BF16)


def _split2(a):
    hi = _bf(a)
    return hi, _bf(a - hi.astype(F32))


def _split3(a):
    hi = _bf(a)
    r = a - hi.astype(F32)
    mid = _bf(r)
    return hi, mid, _bf(r - mid.astype(F32))


def _hilo(a):
    hi = _bf(a).astype(F32)
    return hi, a - hi


def dot1(a, b, mode="nn"):
    return _dg(_bf(a), _bf(b), mode)


def dot3(a, b, mode="nn"):
    ah, al = _split2(a)
    bh, bl = _split2(b)
    return _dg(ah, bh, mode) + (_dg(ah, bl, mode) + _dg(al, bh, mode))


def dotk3(a, b, mode="nn"):
    ah, al = _hilo(a)
    bh, bl = _hilo(b)
    ax = 0 if mode == "tn" else 1
    bx = 1 if mode == "nt" else 0
    return _dg(_bf(_cat([ah, al, ah], ax)), _bf(_cat([bh, bh, bl], bx)), mode)


def dot_xl(a_exact, b, mode="nn"):
    b1, b2, b3 = _split3(b)
    return _dg(a_exact, b1, mode) + (_dg(a_exact, b2, mode) + _dg(a_exact, b3, mode))


def dot_xl2(a_exact, b, mode="nn"):
    b1, b2 = _split2(b)
    return _dg(a_exact, b1, mode) + _dg(a_exact, b2, mode)


def dot_xr(a, b_exact, mode="nn"):
    a1, a2, a3 = _split3(a)
    return _dg(a1, b_exact, mode) + (_dg(a2, b_exact, mode) + _dg(a3, b_exact, mode))


def _sigmoid(x):
    return 1.0 / (1.0 + _exp(-x))


def _silu(x):
    return x * _sigmoid(x)


def _softplus(x):
    return _maximum(x, 0.0) + _log(1.0 + _exp(-_abs(x)))


def _iota(shape, dim):
    return lax.broadcasted_iota(jnp.int32, shape, dim)


def _tri_mask(n, rev, strict):
    r, c = _iota((n, n), 0), _iota((n, n), 1)
    if rev:
        return (r < c) if strict else (r <= c)
    return (r > c) if strict else (r >= c)


def _block_ones(n, blk):
    return (_iota((n, n), 0) // blk == _iota((n, n), 1) // blk).astype(BF16)


def _expand_blockdiag(compact, n_heads=4):
    lane_head = _iota((1, compact.shape[1]), 1) // HEAD
    return _cat([_where(lane_head == h, compact, 0.0) for h in range(n_heads)], 0)


def _compact_blockdiag(blk, rows_per_head, n_heads=4):
    lane_head = _iota((1, blk.shape[1]), 1) // HEAD
    out = _where(lane_head == 0, blk[0:rows_per_head], 0.0)
    for h in range(1, n_heads):
        out = out + _where(lane_head == h, blk[h * rows_per_head:(h + 1) * rows_per_head], 0.0)
    return out


def _expand_heads(cols, first, n_heads=4):
    lane_head = _iota((1, n_heads * HEAD), 1) // HEAD
    out = _where(lane_head == 0, cols[:, first:first + 1], 0.0)
    for h in range(1, n_heads):
        out = _where(lane_head == h, cols[:, first + h:first + h + 1], out)
    return out


def _shift_rows(x, edge_row, down):
    n = x.shape[0]
    rows = _iota((n, 1), 0)
    if down:
        return _where(rows == 0, edge_row, _roll(x, 1, 0))
    return _where(rows == n - 1, edge_row, _roll(x, n - 1, 0))


def _params(sem):
    return pltpu.CompilerParams(dimension_semantics=sem, vmem_limit_bytes=VMEM_LIMIT)


def _full(shape):
    nd = len(shape)
    return pl.BlockSpec(shape, lambda *_: (0,) * nd)


def _nb(b, want):
    return want if b % want == 0 else 1


def _mod_kernel(c_ref, w_ref, b_ref, o_ref):
    s = _silu(c_ref[...])
    o_ref[0] = dot3(s, w_ref[0]) + b_ref[0]


def _modulation(cvec, ada_w, ada_b):
    depth, d, n = ada_w.shape
    rows = cvec.shape[0]
    tn = 1024
    return pl.pallas_call(
        _mod_kernel,
        grid=(depth, n // tn),
        in_specs=[pl.BlockSpec((rows, d), lambda l, j: (0, 0)),
                  pl.BlockSpec((1, d, tn), lambda l, j: (l, 0, j)),
                  pl.BlockSpec((1, 1, tn), lambda l, j: (l, 0, j))],
        out_specs=pl.BlockSpec((1, rows, tn), lambda l, j: (l, 0, j)),
        out_shape=jax.ShapeDtypeStruct((depth, rows, n), F32),
        compiler_params=_params(("parallel", "parallel")),
        name="adaln_mod",
    )(cvec, ada_w, ada_b.reshape(depth, 1, n))


def _inproj_kernel(x_ref, mod_ref, g_ref, w_ref, o_rw, o_s5, o_ssd, o_gla):
    x = x_ref[0]
    d = x.shape[1]
    hn = x * lax.rsqrt(jnp.mean(x * x, axis=-1, keepdims=True) + EPS) * g_ref[...]
    m = mod_ref[0]
    hn = hn * (1.0 + m[:, d:2 * d]) + m[:, 0:d]
    p = dot1(hn, w_ref[...])
    o_rw[0] = p[:, 0:W_RW]
    o_s5[0] = p[:, W_RW:W_RW + W_S5]
    o_ssd[0] = p[:, W_RW + W_S5:W_RW + W_S5 + W_SSD]
    o_gla[0] = p[:, W_RW + W_S5 + W_SSD:N_PROJ]


def _inproj(h, mod3, mod_row, norm_pre, w_in_p, tm):
    b, L, d = h.shape
    widths = (W_RW, W_S5, W_SSD, W_GLA)
    return pl.pallas_call(
        _inproj_kernel,
        grid=(b, L // tm),
        in_specs=[pl.BlockSpec((1, tm, d), lambda i, j: (i, j, 0)),
                  pl.BlockSpec((1, 1, 3 * d), lambda i, j: (mod_row(i), 0, 0)),
                  _full((1, d)),
                  _full(w_in_p.shape)],
        out_specs=[pl.BlockSpec((1, tm, w), lambda i, j: (i, j, 0)) for w in widths],
        out_shape=[jax.ShapeDtypeStruct((b, L, w), F32) for w in widths],
        compiler_params=_params(("parallel", "parallel")),
        name="in_proj",
    )(h, mod3, norm_pre.reshape(1, d), w_in_p)


def _tri_inv_stages(n_mats):
    r, c = _iota((CH, CH), 0), _iota((CH, CH), 1)
    blk16 = (r // 16) == (c // 16)
    blk32 = (r // 32) == (c // 32)
    eye = jnp.where(r == c, 1.0, 0.0)
    nd = [_where(blk16, n, 0.0) for n in n_mats]
    d = [eye + x for x in nd]
    p = [dot1(x, x) for x in nd]
    yield "h"
    for _ in range(2):
        d = [x + dot1(x, y) for x, y in zip(d, p)]
        p = [dot1(y, y) for y in p]
        yield "h"
    d = [x + dot1(x, y) for x, y in zip(d, p)]
    yield "h"
    for sel in (blk32 & jnp.logical_not(blk16), jnp.logical_not(blk32)):
        od = [dot1(_where(sel, n, 0.0), x) for n, x in zip(n_mats, d)]
        yield "h"
        d = [x + dot1(x, y) for x, y in zip(d, od)]
        yield "h"
    return d


def _run_staggered(gens, width):
    pending, active = list(gens), []
    while pending or active:
        while pending and len(active) < width:
            active.append(pending.pop(0))
        for g in list(active):
            try:
                next(g)
            except StopIteration:
                active.remove(g)


def _member_io(members):
    def ld(ref, *idx):
        return _Lk([ref[(s,) + idx] for s in members])

    def st(ref, idx, val):
        for j, s in enumerate(members):
            ref[(s,) + idx] = val.xs[j]
    return ld, st


def _rw_local_stages(members, is_grid, nc, cur_ref, prev_ref, next_ref, mu_ref, w0_ref, w2_ref, a0_ref, a2_ref,
                     kk_ref, ka_ref, rk_ref, rt_ref, y0_ref, bonus_ref, m_ref, c_ref):
    _load, _store = _member_io(members)

    ci = pl.program_id(1)
    has_prev = (ci > 0).astype(F32)
    has_next = (ci < nc - 1).astype(F32)
    z = _load(cur_ref, slice(None), slice(0, 1024))
    if is_grid:
        zero_row = jnp.zeros((1, W_BR), F32)
        left = _shift_rows(z[:, 0:256], zero_row, True)
        right = _shift_rows(z[:, 256:512], zero_row, False)
        up = _load(prev_ref) * has_prev
        down = _load(next_ref) * has_next
        sh = _cat([left, right, up, down], 1)
    else:
        prow = _load(prev_ref, slice(7, 8)) * has_prev
        nrow = _load(next_ref, slice(0, 1)) * has_next
        sh = _cat([_shift_rows(z[:, 0:512], prow, True), _shift_rows(z[:, 512:1024], nrow, False)], 1)
    zs = z + mu_ref[...] * (sh - z)
    yield "p"
    r, k, v = zs[:, 0:256], zs[:, 256:512], zs[:, 512:768]
    wl, al = zs[:, 768:896], zs[:, 896:1024]

    bo = _block_ones(W_BR, HEAD)
    kk = k * kk_ref[...]
    kk = kk * _rsqrt(dot1(kk * kk, bo) + EPS)
    yield "p"
    eye = _iota((HEAD, HEAD), 0) == _iota((HEAD, HEAD), 1)
    heads = lambda t: [t[:, HEAD * h:HEAD * (h + 1)] for h in range(4)]
    v_h = heads(v)
    row2, col2 = _iota((CH, 2 * CH), 0), _iota((CH, 2 * CH), 1) % CH

    bonus = None
    pa_l, pam_l, prm_l, pr_l, nb_l, nk_l, eb_l, ek_l, dg_l, strict_l, incl2_l = ([] for _ in range(11))
    for d in range(2):
        rev = d == 1
        w_pre = w0_ref[d:d + 1, :] + dot1(_tanh(wl[:, RW_LORA * d:RW_LORA * (d + 1)]), w2_ref[d])
        lw = (-math.exp(-0.5)) * _sigmoid(w_pre)
        a = _sigmoid(a0_ref[d:d + 1, :] + dot1(al[:, RW_LORA * d:RW_LORA * (d + 1)], a2_ref[d]))
        k_d = k * (1.0 + (a - 1.0) * ka_ref[...])
        bterm = dot1(r * k_d * rk_ref[...], bo) * v
        bonus = bterm if bonus is None else bonus + bterm
        yield "p"

        cs = dot_xl2(_tri_mask(CH, rev, False).astype(BF16), lw)
        cs_ex = cs - lw
        cs_tot = cs[0:1] if rev else cs[CH - 1:CH]
        cs_mid = cs[CH // 2:CH // 2 + 1]
        kka = kk * a
        e_mid = _exp(cs_mid - cs)
        e_tot = _exp(cs_tot - cs)
        yield "p"
        pa_l += heads(-kk * _exp(cs_ex))
        pr_l += heads(r * _exp(cs))
        yield "p"
        pam_l += heads(-kk * _exp(cs_ex - cs_mid))
        prm_l += heads(r * _exp(cs - cs_mid))
        yield "p"
        nb_l += heads(kka * e_mid)
        nk_l += heads(k_d * e_mid)
        yield "p"
        eb_l += heads(kka * e_tot)
        ek_l += heads(k_d * e_tot)
        dg_l += [_where(eye, t, 0.0) for t in heads(_exp(cs_tot))]
        strict_l += [_tri_mask(CH, rev, True)] * 4
        incl2_l += [(row2 <= col2) if rev else (row2 >= col2)] * 4
        yield "p"
    vv = v_h + v_h
    idx = range(8)

    quad = [dot1(_cat([pam_l[i], prm_l[i]], 0), _cat([nb_l[i], nk_l[i]], 0), "nt") for i in idx]
    yield "h"
    n_l = [_where(strict_l[i], quad[i][0:CH, 0:CH], 0.0) for i in idx]
    aak = [_where(strict_l[i], quad[i][0:CH, CH:2 * CH], 0.0) for i in idx]
    g2 = [_where(incl2_l[i], quad[i][CH:2 * CH], 0.0) for i in idx]
    t_inv = yield from _tri_inv_stages(n_l)
    av = [dot1(aak[i], vv[i]) for i in idx]
    yield "h"
    wu = [dot1(t_inv[i], _cat([pa_l[i], av[i]], 1)) for i in idx]
    yield "h"
    zero = jnp.zeros((CH, HEAD), F32)
    rhs2 = [_cat([wu[i], _cat([zero, vv[i]], 1)], 0) for i in idx]
    gy = [dot1(g2[i], rhs2[i]) for i in idx]
    yield "h"
    mc = [dot1(_cat([eb_l[i], ek_l[i]], 0), rhs2[i], "tn") for i in idx]
    yield "h"
    y0_parts = []
    for d in range(2):
        sl = range(4 * d, 4 * d + 4)
        _store(rt_ref, (0, d), _cat([pr_l[i] + gy[i][:, 0:HEAD] for i in sl], 1))
        _store(m_ref, (0, d), _cat([dg_l[i] + mc[i][:, 0:HEAD] for i in sl], 1))
        _store(c_ref, (0, d), _cat([mc[i][:, HEAD:2 * HEAD] for i in sl], 1))
        y0_parts.append(_cat([gy[i][:, HEAD:2 * HEAD] for i in sl], 1))
    _store(y0_ref, (), y0_parts[0] + y0_parts[1])
    _store(bonus_ref, (), bonus)


def _halo_specs(nb, is_grid, nc):
    if is_grid:
        prev = pl.BlockSpec((nb, CH, W_BR), lambda b, c: (b, jnp.maximum(c - 1, 0), 2))
        nxt = pl.BlockSpec((nb, CH, W_BR), lambda b, c: (b, jnp.minimum(c + 1, nc - 1), 3))
    else:
        rb = CH // 8
        prev = pl.BlockSpec((nb, 8, 512), lambda b, c: (b, jnp.maximum(c * rb - 1, 0), 0))
        nxt = pl.BlockSpec((nb, 8, 512), lambda b, c: (b, jnp.minimum((c + 1) * rb, nc * rb - 1), 1))
    return prev, nxt


def _scan_specs(nb, cb, nblk, rows, width):
    in_f = pl.BlockSpec((nb, cb, 1, rows, width), lambda i, j: (i, j, 0, 0, 0))
    in_r = pl.BlockSpec((nb, cb, 1, rows, width), lambda i, j: (i, nblk - 1 - j, 1, 0, 0))
    out_f = pl.BlockSpec((nb, cb, rows, width), lambda i, j: (i, j, 0, 0))
    out_r = pl.BlockSpec((nb, cb, rows, width), lambda i, j: (i, nblk - 1 - j, 0, 0))
    st = pl.BlockSpec((nb, 2, rows, width), lambda i, j: (i, 0, 0, 0))
    return in_f, in_r, out_f, out_r, st


def _rw_scan_kernel(nblk, cb, mf_ref, mr_ref, cf_ref, cr_ref, init_ref, sf_ref, sr_ref, fin_ref, s_scr):
    ci = pl.program_id(1)

    @pl.when(ci == 0)
    def _():
        s_scr[...] = init_ref[...]

    heads = lambda t: [t[:, HEAD * h:HEAD * (h + 1)] for h in range(4)]
    s = [_load(s_scr, 0), _load(s_scr, 1)]
    for j in range(cb):
        new = []
        for d, (m_ref, c_ref, o_ref) in enumerate(((mf_ref, cf_ref, sf_ref), (mr_ref, cr_ref, sr_ref))):
            jj = cb - 1 - j if d else j
            _store(o_ref, (jj,), s[d])
            m_h = heads(_load(m_ref, jj, 0))
            s_h = heads(s[d])
            prod = [dotk3(m_h[h], s_h[h]) for h in range(4)]
            new.append(_cat(prod, 1) + _load(c_ref, jj, 0))
        s = new
    _store(s_scr, (0,), s[0])
    _store(s_scr, (1,), s[1])

    @pl.when(ci == nblk - 1)
    def _():
        _store(fin_ref, (0,), s[0])
        _store(fin_ref, (1,), s[1])


def _scan_call(kernel_fn, name, a, b_arr, init):
    b, nc, _, rows, width = a.shape
    nb = _nb(b, SCAN_NB)
    cb = SCAN_CB if nc % SCAN_CB == 0 else nc
    nblk = nc // cb
    in_f, in_r, out_f, out_r, st = _scan_specs(nb, cb, nblk, rows, width)
    ent_shape = jax.ShapeDtypeStruct((b, nc, rows, width), F32)
    return pl.pallas_call(
        functools.partial(kernel_fn, nblk, cb),
        grid=(b // nb, nblk),
        in_specs=[in_f, in_r, in_f, in_r, st],
        out_specs=[out_f, out_r, st],
        out_shape=[ent_shape, ent_shape, jax.ShapeDtypeStruct((b, 2, rows, width), F32)],
        scratch_shapes=[pltpu.VMEM((nb, 2, rows, width), F32)],
        compiler_params=_params(("parallel", "arbitrary")),
        name=name,
    )(a, a, b_arr, b_arr, init)


def _rw_out_value(y0_ref, bonus_ref, rt_ref, sf_ref, sr_ref, gate_ref, lnw_ref, lnb_ref):
    y = _load(y0_ref)
    y = y + dot1(_load(rt_ref, 0, 0), _expand_blockdiag(_load(sf_ref, 0)))
    y = y + dot1(_load(rt_ref, 0, 1), _expand_blockdiag(_load(sr_ref, 0)))
    bo = _block_ones(W_BR, HEAD)
    y_hi, y_lo = _split2(y)
    mu = (_dg(y_hi, bo, "nn") + _dg(y_lo, bo, "nn")) * (1.0 / HEAD)
    yc = y - mu
    var = dot1(yc * yc, bo) * (1.0 / HEAD)
    yn = yc * _rsqrt(var + RW_GN_EPS) * lnw_ref[...] + lnb_ref[...]
    return (yn + _load(bonus_ref)) * _silu(_load(gate_ref))


def _ew_scan_kernel(nblk, cb, cf_ref, cr_ref, df_ref, dr_ref, init_ref, sf_ref, sr_ref, fin_ref, s_scr):
    ci = pl.program_id(1)

    @pl.when(ci == 0)
    def _():
        s_scr[...] = init_ref[...]

    s = [_load(s_scr, 0), _load(s_scr, 1)]
    for j in range(cb):
        for d, (c_ref, d_ref, o_ref) in enumerate(((cf_ref, df_ref, sf_ref), (cr_ref, dr_ref, sr_ref))):
            jj = cb - 1 - j if d else j
            _store(o_ref, (jj,), s[d])
            s[d] = _load(d_ref, jj, 0) * s[d] + _load(c_ref, jj, 0)
    _store(s_scr, (0,), s[0])
    _store(s_scr, (1,), s[1])

    @pl.when(ci == nblk - 1)
    def _():
        _store(fin_ref, (0,), s[0])
        _store(fin_ref, (1,), s[1])


def _gla_log_decay(gl, g2_ref, gb_ref, d):
    x = dot3(gl[:, GLA_RANK * d:GLA_RANK * (d + 1)], g2_ref[d]) + gb_ref[d:d + 1, :]
    return -_softplus(-x) * (1.0 / GLA_TAU)


def _gla_local_stages(members, f_ref, g2_ref, gb_ref, cst_ref, dec_ref):
    ld, st = _member_io(members)
    k = ld(f_ref, slice(None), slice(128, 256))
    v = ld(f_ref, slice(None), slice(256, 512))
    gl = ld(f_ref, slice(None), slice(768, 800))
    sel = (_iota((128, W_BR), 0) // GLA_DK) == (_iota((128, W_BR), 1) // HEAD)
    first_row = _iota((CH, 128), 0) == 0
    ones = jnp.ones((CH, W_BR), BF16)
    las = [_gla_log_decay(gl, g2_ref, gb_ref, d) for d in range(2)]
    yield "p"
    css = [dot_xl(_tri_mask(CH, d == 1, False).astype(BF16), las[d]) for d in range(2)]
    yield "p"
    tots = [css[0][CH - 1:CH], css[1][0:1]]
    kes = [k * _exp(tots[d] - css[d]) for d in range(2)]
    yield "p"
    c_blk = [_where(sel, dot1(kes[d], v, "tn"), 0.0) for d in range(2)]
    yield "p"
    d_blk = [dot_xr(_where(first_row, _exp(tots[d]), 0.0), ones, "tn") for d in range(2)]
    yield "p"
    for d in range(2):
        st(cst_ref, (0, d), _compact_blockdiag(c_blk[d], GLA_DK))
        st(dec_ref, (0, d), _compact_blockdiag(d_blk[d], GLA_DK))


def _gla_out_value(f_ref, sf_ref, sr_ref, g2_ref, gb_ref, norm_ref):
    q = _load(f_ref, slice(None), slice(0, 128)) * (GLA_DK ** -0.5)
    k = _load(f_ref, slice(None), slice(128, 256))
    v = _load(f_ref, slice(None), slice(256, 512))
    gate = _load(f_ref, slice(None), slice(512, 768))
    gl = _load(f_ref, slice(None), slice(768, 800))
    khead = _iota((1, 128), 1) // GLA_DK
    vhead = _iota((1, W_BR), 1) // HEAD
    las = [_gla_log_decay(gl, g2_ref, gb_ref, d) for d in range(2)]
    css = [dot_xl(_tri_mask(CH, d == 1, False).astype(BF16), las[d]) for d in range(2)]
    mids = [css[d][CH // 2:CH // 2 + 1] for d in range(2)]
    qes = [q * _exp(css[d] - mids[d]) for d in range(2)]
    kns = [k * _exp(mids[d] - css[d]) for d in range(2)]
    sc = [[_where(_tri_mask(CH, d == 1, False), dot1(_where(khead == h, qes[d], 0.0), kns[d], "nt"), 0.0)
           for h in range(4)] for d in range(2)]
    inter = [dot1(q * _exp(css[d]), _expand_blockdiag(_load(s_ref, 0)))
             for d, s_ref in enumerate((sf_ref, sr_ref))]
    pv = [dot1(sc[0][h] + sc[1][h], _where(vhead == h, v, 0.0)) for h in range(4)]
    y = (inter[0] + inter[1]) + ((pv[0] + pv[1]) + (pv[2] + pv[3]))
    ms = dot1(y * y, _block_ones(W_BR, HEAD)) * (1.0 / HEAD)
    y = y * _rsqrt(ms + EPS) * norm_ref[...]
    return y * _silu(gate)


def _ssd_steps(dt_raw, dtb_ref, aneg_ref):
    dt = _softplus(dt_raw + dtb_ref[...])
    return dt, dt * aneg_ref[...]


def _ssd_cumsum(la):
    lane = _iota((1, 8), 1)
    la3 = _split3(la)
    tril, triu = (_tri_mask(CH, r, False).astype(BF16) for r in (False, True))
    cs_f = _dg(tril, la3[0], "nn") + (_dg(tril, la3[1], "nn") + _dg(tril, la3[2], "nn"))
    cs_r = _dg(triu, la3[0], "nn") + (_dg(triu, la3[1], "nn") + _dg(triu, la3[2], "nn"))
    cs = _where(lane < 4, cs_f, cs_r)
    return cs, _where(lane < 4, cs[CH - 1:CH], cs[0:1]), la3


def _ssd_local_stages(members, nc, cur_ref, prev_ref, next_ref, cw_ref, cb_ref, dtb_ref, aneg_ref,
                      cst_ref, dec_ref, xbc_ref):
    ld, st = _member_io(members)
    ci = pl.program_id(1)
    has_prev = (ci > 0).astype(F32)
    has_next = (ci < nc - 1).astype(F32)
    xbc = ld(cur_ref, slice(None), slice(0, 768))
    ext = _cat([ld(prev_ref) * has_prev, xbc, ld(next_ref) * has_next], 0)
    acc = cb_ref[...] + cw_ref[0:1, :] * ext[6:6 + CH]
    for j in range(1, SSD_CONV):
        acc = acc + cw_ref[j:j + 1, :] * ext[6 + j:6 + j + CH]
        yield "p"
    xbc = _silu(acc)
    st(xbc_ref, (), xbc)
    yield "p"
    x, b_rep = xbc[:, 0:256], xbc[:, 256:512]
    dt, la = _ssd_steps(ld(cur_ref, slice(None), slice(1024, 1032)), dtb_ref, aneg_ref)
    cs, cs_tot, _ = _ssd_cumsum(la)
    yield "p"
    wgt = _exp(cs_tot - cs) * dt
    sel = (_iota((W_BR, W_BR), 0) // HEAD) == (_iota((W_BR, W_BR), 1) // HEAD)
    e_tot = _exp(cs_tot)
    xs = [x * _expand_heads(wgt, 4 * d) for d in range(2)]
    yield "p"
    c_blk = [_where(sel, dot1(b_rep, xs[d], "tn"), 0.0) for d in range(2)]
    yield "p"
    for d in range(2):
        st(cst_ref, (0, d), _compact_blockdiag(c_blk[d], HEAD))
        st(dec_ref, (0, d), _bcast(_expand_heads(e_tot, 4 * d), (HEAD, W_BR)))


def _local_kernel(is_grid, nc, *refs):
    rw_in, ssd_in, gla_in = refs[0:11], refs[11:18], refs[18:21]
    rw_out, ssd_out, gla_out = refs[21:26], refs[26:29], refs[29:31]
    nb = rw_in[0].shape[0]
    pairs = [range(s, min(s + 2, nb)) for s in range(0, nb, 2)]
    order = []
    for g in pairs:
        order += [_rw_local_stages(g, is_grid, nc, *rw_in, *rw_out), _ssd_local_stages(g, nc, *ssd_in, *ssd_out),
                  _gla_local_stages(g, *gla_in, *gla_out)]
    _run_staggered(order, 8)


def _chunk_local(f_rw, f_ssd, f_gla, p, is_grid):
    b, L, _ = f_rw.shape
    nc = L // CH
    nb = _nb(b, NB_RW)
    rb = CH // 8
    blk = lambda *shape: pl.BlockSpec((nb,) + shape, lambda i, c: (i, c) + (0,) * (len(shape) - 1))
    rw_prev, rw_next = _halo_specs(nb, is_grid, nc)
    ssd_prev = pl.BlockSpec((nb, 8, 768), lambda i, c: (i, jnp.maximum(c * rb - 1, 0), 0))
    ssd_next = pl.BlockSpec((nb, 8, 768), lambda i, c: (i, jnp.minimum((c + 1) * rb, nc * rb - 1), 0))
    st = lambda rows: (blk(1, 2, rows, W_BR), jax.ShapeDtypeStruct((b, nc, 2, rows, W_BR), F32))
    rows = lambda w: (blk(CH, w), jax.ShapeDtypeStruct((b, L, w), F32))
    outs = [st(HEAD), rows(W_BR), rows(W_BR), st(HEAD), st(HEAD),
            st(HEAD), st(HEAD), rows(768),
            st(GLA_DK), st(GLA_DK)]
    cw = jnp.concatenate([p["conv_w_rep"], jnp.zeros((8 - SSD_CONV, 768), F32)], axis=0)
    res = pl.pallas_call(
        functools.partial(_local_kernel, is_grid, nc),
        grid=(b // nb, nc),
        in_specs=[blk(CH, W_RW), rw_prev, rw_next,
                  _full((1, 1024)), _full((2, W_BR)), _full((2, RW_LORA, W_BR)), _full((2, W_BR)),
                  _full((2, RW_LORA, W_BR)), _full((1, W_BR)), _full((1, W_BR)), _full((1, W_BR)),
                  blk(CH, W_SSD), ssd_prev, ssd_next, _full((8, 768)), _full((1, 768)), _full((1, 8)), _full((1, 8)),
                  blk(CH, W_GLA), _full((2, GLA_RANK, 128)), _full((2, 128))],
        out_specs=[o[0] for o in outs],
        out_shape=[o[1] for o in outs],
        compiler_params=_params(("parallel", "parallel")),
        name="chunk_local",
    )(f_rw, f_rw, f_rw, p["rw_mu"].reshape(1, 1024), p["rw_w0"], p["rw_w2"], p["rw_a0"], p["rw_a2"],
      p["rw_kk"].reshape(1, W_BR), p["rw_ka"].reshape(1, W_BR), p["rw_rk"].reshape(1, W_BR),
      f_ssd, f_ssd, f_ssd, cw, p["conv_b_rep"].reshape(1, 768), p["ssd_dt_bias"].reshape(1, 8),
      p["ssd_a_neg"].reshape(1, 8), f_gla, p["gla_g2"], p["gla_gb"])
    return res[0:5], res[5:8], res[8:10]


def _ssd_out_value(xbc_ref, tail_ref, dtb_ref, aneg_ref, sf_ref, sr_ref, dskip_ref, norm_ref):
    x = _load(xbc_ref, slice(None), slice(0, 256))
    b_rep = _load(xbc_ref, slice(None), slice(256, 512))
    c_rep = _load(xbc_ref, slice(None), slice(512, 768))
    z = _load(tail_ref, slice(None), slice(0, 256))
    dt, la = _ssd_steps(_load(tail_ref, slice(None), slice(256, 264)), dtb_ref, aneg_ref)
    cs, _, la3 = _ssd_cumsum(la)
    lane_head = _iota((1, W_BR), 1) // HEAD
    tril, triu = (_tri_mask(CH, r, False).astype(BF16) for r in (False, True))
    cst_f = _dg(la3[0], triu, "tn") + (_dg(la3[1], triu, "tn") + _dg(la3[2], triu, "tn"))
    cst_r = _dg(la3[0], tril, "tn") + (_dg(la3[1], tril, "tn") + _dg(la3[2], tril, "tn"))
    cst = _where(_iota((8, 1), 0) < 4, cst_f, cst_r)
    e_cs = _exp(cs)
    g = [dot1(c_rep[:, 2 * HEAD * grp:2 * HEAD * grp + HEAD], b_rep[:, 2 * HEAD * grp:2 * HEAD * grp + HEAD], "nt")
         for grp in range(2)]
    inter = [dot1(c_rep * _expand_heads(e_cs, 4 * d), _expand_blockdiag(_load(s_ref, 0)))
             for d, s_ref in enumerate((sf_ref, sr_ref))]
    xs = [x * _expand_heads(dt, 4 * d) for d in range(2)]
    intra = []
    for d in range(2):
        incl = _tri_mask(CH, d == 1, False)
        for h in range(4):
            col = 4 * d + h
            seg = _exp(_where(incl, cs[:, col:col + 1] - cst[col:col + 1, :], -1e30))
            intra.append(dot1(g[h // 2] * seg, _where(lane_head == h, xs[d], 0.0)))
    y = dskip_ref[...] * x + (inter[0] + inter[1])
    y = y + (((intra[0] + intra[1]) + (intra[2] + intra[3])) + ((intra[4] + intra[5]) + (intra[6] + intra[7])))
    y = y * _silu(z)
    return y * _rsqrt(_mean_last(y * y) + EPS) * norm_ref[...]


def _s5_weights(p):
    t = S5_T
    steps = jnp.arange(t + 1, dtype=F32)
    kmats, fmats, emats, lam_re, lam_im = [], [], [], [], []
    tt = jnp.arange(t)
    tau = tt[None, :] - tt[:, None]
    cmul = lambda x, y: (x[0] * y[0] - x[1] * y[1], x[0] * y[1] + x[1] * y[0])
    for d in range(2):
        a_re, a_im = p["s5_a_re"][d], p["s5_a_im"][d]
        dt = jnp.exp(p["s5_log_dt"][d])[:, None]
        mag = jnp.exp((a_re * dt)[None] * steps[:, None, None])
        ang = (a_im * dt)[None] * steps[:, None, None]
        pw = (mag * jnp.cos(ang), mag * jnp.sin(ang))
        num = (pw[0][1] - 1.0, pw[1][1])
        den = a_re * a_re + a_im * a_im
        quo = ((num[0] * a_re + num[1] * a_im) / den, (num[1] * a_re - num[0] * a_im) / den)
        b_bar = cmul((quo[0][..., None], quo[1][..., None]), (p["s5_b_re"][d], p["s5_b_im"][d]))
        c_c = (p["s5_c_re"][d], p["s5_c_im"][d])
        qb = cmul((pw[0][:t, :, :, None], pw[1][:t, :, :, None]), (b_bar[0][None], b_bar[1][None]))
        hp = lax.Precision.HIGHEST
        kd = (jnp.einsum("gcp,jgpe->gcje", c_c[0], qb[0], precision=hp)
              - jnp.einsum("gcp,jgpe->gcje", c_c[1], qb[1], precision=hp))
        lag = -tau if d else tau
        sel = (lag[:, :, None] == jnp.arange(t)[None, None, :]).astype(F32)
        kmats.append(jnp.einsum("stj,gcje->gsetc", sel, kd, precision=hp).reshape(S5_G, t * S5_CH, t * S5_CH))
        fsel = (lambda x: x) if d else (lambda x: x[::-1])
        fmats += [jnp.transpose(fsel(x), (1, 0, 3, 2)).reshape(S5_G, t * S5_CH, S5_P) for x in qb]
        esel = (lambda x: x[1:][::-1]) if d else (lambda x: x[1:])
        ec = cmul((c_c[0][None], c_c[1][None]), (esel(pw[0])[:, :, None, :], esel(pw[1])[:, :, None, :]))
        ec = [jnp.transpose(x, (1, 3, 0, 2)).reshape(S5_G, S5_P, t * S5_CH) for x in ec]
        emats += [ec[0], -ec[1]]
        lam_re += [pw[0][t], pw[0][t]]
        lam_im += [-pw[1][t], pw[1][t]]
    kmat = kmats[0] + kmats[1]
    fmat = jnp.concatenate(fmats, axis=2)
    emat = jnp.concatenate(emats, axis=1)
    return (kmat.astype(BF16), fmat.astype(BF16), emat.astype(BF16),
            jnp.concatenate(lam_re, axis=1), jnp.concatenate(lam_im, axis=1))


def _s5_local_kernel(ua_ref, ub_ref, f_ref, ug_ref, z_ref):
    rb = z_ref.shape[2]
    step_rows = lambda s: jnp.concatenate([ua_ref[0, pl.ds(s, rb, stride=S5_T), :],
                                           ub_ref[0, pl.ds(s, rb, stride=S5_T), :]], axis=1)
    ug = _block_transpose16([step_rows(s) for s in range(S5_T)])
    ub = [_bf(x) for x in ug]
    zs = [_dg(ub[g], f_ref[g], "nn") for g in range(S5_G)]
    for g in range(S5_G):
        ug_ref[0, g] = ub[g]
        z_ref[0, g] = zs[g]


def _block_transpose16(arrs):
    lane_blk = _iota((1, 256), 1) // 16
    arrs = list(arrs)
    for k in range(4):
        b = 1 << k
        bit = (lane_blk & b) != 0
        new = list(arrs)
        for i in range(16):
            if not i & b:
                lo, hi = arrs[i], arrs[i | b]
                new[i] = jnp.where(bit, pltpu.roll(hi, 16 * b, 1), lo)
                new[i | b] = jnp.where(bit, hi, pltpu.roll(lo, 256 - 16 * b, 1))
        arrs = new
    return arrs


def _s5_rows(nc):
    return 64 if nc % 64 == 0 else nc


def _s5_local(f_s5, fmat):
    b, L, _ = f_s5.shape
    nc = L // S5_T
    rb = _s5_rows(nc)
    grp = pl.BlockSpec((1, S5_G, rb, 256), lambda i, j: (i, 0, j, 0))
    return pl.pallas_call(
        _s5_local_kernel,
        grid=(b, nc // rb),
        in_specs=[pl.BlockSpec((1, rb * S5_T, 128), lambda i, j: (i, j, 0)),
                  pl.BlockSpec((1, rb * S5_T, 128), lambda i, j: (i, j, 1)), _full(fmat.shape)],
        out_specs=[grp, grp],
        out_shape=[jax.ShapeDtypeStruct((b, S5_G, nc, 256), BF16), jax.ShapeDtypeStruct((b, S5_G, nc, 256), F32)],
        compiler_params=_params(("parallel", "parallel")),
        name="s5_local",
    )(f_s5, f_s5, fmat)


def _s5_scan_kernel(nblk, cb, zf_ref, zr_ref, lre_ref, lim_ref, init_ref, xf_ref, xr_ref, fin_ref, x_scr):
    ci = pl.program_id(0)

    @pl.when(ci == 0)
    def _():
        x_scr[...] = init_ref[...]

    lane = _iota((1, 256), 1)
    is_fwd = lane < 128
    even_q = (lane // S5_P) % 2 == 0
    lre, lim = lre_ref[...], lim_ref[...]
    x = _load(x_scr)
    for i in range(cb):
        for s in range(len(x.xs)):
            xf_ref[s, :, i, :] = x.xs[s]
            xr_ref[s, :, cb - 1 - i, :] = x.xs[s]
        z = _Lk([jnp.where(is_fwd, zf_ref[s, :, i, :], zr_ref[s, :, cb - 1 - i, :]) for s in range(len(x.xs))])
        swapped = _where(even_q, _roll(x, 192, 1), _roll(x, 64, 1))
        x = lre * x + lim * swapped + z
    _store(x_scr, (), x)

    @pl.when(ci == nblk - 1)
    def _():
        _store(fin_ref, (), x)


def _s5_scan(z, lam_re, lam_im, init):
    b, g, nc, w = z.shape
    cb = 8
    nblk = nc // cb
    fwd = pl.BlockSpec((b, g, cb, w), lambda j: (0, 0, j, 0))
    rev = pl.BlockSpec((b, g, cb, w), lambda j: (0, 0, nblk - 1 - j, 0))
    ent = jax.ShapeDtypeStruct((b, g, nc, w), F32)
    return pl.pallas_call(
        functools.partial(_s5_scan_kernel, nblk, cb),
        grid=(nblk,),
        in_specs=[fwd, rev, _full((g, w)), _full((g, w)), _full((b, g, w))],
        out_specs=[fwd, rev, _full((b, g, w))],
        out_shape=[ent, ent, jax.ShapeDtypeStruct((b, g, w), F32)],
        scratch_shapes=[pltpu.VMEM((b, g, w), F32)],
        compiler_params=_params(("arbitrary",)),
        name="s5_scan",
    )(z, z, lam_re, lam_im, init)


def _s5_out_kernel(ug_ref, xf_ref, xr_ref, k_ref, e_ref, ya_ref, yb_ref):
    rb = ug_ref.shape[2]
    lane = _iota((1, 256), 1)
    groups = range(S5_G)
    x_ent = [_split2(jnp.where(lane < 128, xf_ref[0, g], xr_ref[0, g])) for g in groups]
    ys = [_dg(ug_ref[0, g], k_ref[g], "nn") + (_dg(x_ent[g][0], e_ref[g], "nn") + _dg(x_ent[g][1], e_ref[g], "nn"))
          for g in groups]
    yt = _block_transpose16(ys)
    for s in range(S5_T):
        ya_ref[0, pl.ds(s, rb, stride=S5_T), :] = yt[s][:, 0:128]
        yb_ref[0, pl.ds(s, rb, stride=S5_T), :] = yt[s][:, 128:256]


def _s5_out(ug, xf, xr, kmat, emat):
    b, g, nc, w = ug.shape
    rb = _s5_rows(nc)
    grp = pl.BlockSpec((1, g, rb, w), lambda i, j: (i, 0, j, 0))
    half = pl.BlockSpec((1, rb * S5_T, 128), lambda i, j: (i, j, 0))
    half_shape = jax.ShapeDtypeStruct((b, nc * S5_T, 128), F32)
    return pl.pallas_call(
        _s5_out_kernel,
        grid=(b, nc // rb),
        in_specs=[grp, grp, grp, _full(kmat.shape), _full(emat.shape)],
        out_specs=[half, half],
        out_shape=[half_shape, half_shape],
        compiler_params=_params(("parallel", "parallel")),
        name="s5_out",
    )(ug, xf, xr, kmat, emat)


def _s5_out_value(ys5a_ref, ys5b_ref, fs5_ref, s5d_ref, gluw_ref, glub_ref):
    u = _load(fs5_ref, slice(None), slice(0, 256))
    y = _cat([_load(ys5a_ref), _load(ys5b_ref)], 1) + s5d_ref[...] * u
    y = 0.5 * y * (1.0 + _tanh(math.sqrt(2.0 / math.pi) * (y + 0.044715 * (y * y * y))))
    y = y * _sigmoid(dot1(y, gluw_ref[...]) + glub_ref[...])
    return y * _silu(_load(fs5_ref, slice(None), slice(256, 512)))


def _layer_out_kernel(h_ref, mod_ref, w_ref, g_ref, *refs):
    rw_refs, s5_refs, ssd_refs, gla_refs, o_ref = refs[0:8], refs[8:14], refs[14:22], refs[22:28], refs[28]
    nb, rows, d = h_ref.shape
    ycat = _cat([_rw_out_value(*rw_refs), _s5_out_value(*s5_refs), _ssd_out_value(*ssd_refs),
                 _gla_out_value(*gla_refs)], 1)
    o = dot1(jnp.concatenate(ycat.xs, axis=0), w_ref[...])
    o = o * lax.rsqrt(jnp.mean(o * o, axis=-1, keepdims=True) + EPS) * g_ref[...]
    for s in range(nb):
        gate = mod_ref[min(s, mod_ref.shape[0] - 1)][:, 2 * d:3 * d]
        o_ref[s] = h_ref[s] + gate * o[s * rows:(s + 1) * rows]


def _layer_out(h, mod3, shared_mod_row, rw, s5, ssd, gla, p):
    b, L, d = h.shape
    nc = L // CH
    nb = _nb(b, NB)
    row = lambda w, col=0: pl.BlockSpec((nb, CH, w), lambda i, c: (i, c, col))
    ent = lambda r: pl.BlockSpec((nb, 1, r, W_BR), lambda i, c: (i, c, 0, 0))
    vec = lambda w: _full((1, w))
    if shared_mod_row is None:
        mod_spec = pl.BlockSpec((nb, 1, 3 * d), lambda i, c: (i, 0, 0))
    else:
        mod_spec = pl.BlockSpec((1, 1, 3 * d), lambda i, c: (shared_mod_row, 0, 0))
    tail = W_SSD - 768
    y0, bonus, rt, rw_sf, rw_sr, f_rw = rw
    in_specs = [row(d), mod_spec, _full((d, d)), vec(d),
                row(W_BR), row(W_BR), pl.BlockSpec((nb, 1, 2, HEAD, W_BR), lambda i, c: (i, c, 0, 0, 0)),
                ent(HEAD), ent(HEAD), row(W_BR, 4), vec(W_BR), vec(W_BR),
                row(128), row(128), row(W_S5), vec(W_BR), _full((W_BR, W_BR)), vec(W_BR),
                row(768), row(tail, 768 // tail), vec(8), vec(8), ent(HEAD), ent(HEAD), vec(W_BR), vec(W_BR),
                row(W_GLA), ent(GLA_DK), ent(GLA_DK), _full((2, GLA_RANK, 128)), _full((2, 128)), vec(W_BR)]
    args = [h, mod3, p["w_out"].astype(BF16), p["norm_post"].reshape(1, d),
            y0, bonus, rt, rw_sf, rw_sr, f_rw, p["rw_ln_w"].reshape(1, W_BR), p["rw_ln_b"].reshape(1, W_BR),
            s5[0], s5[1], s5[2], p["s5_d"].reshape(1, W_BR), p["s5_glu_w"].astype(BF16),
            p["s5_glu_b"].reshape(1, W_BR),
            ssd[0], ssd[1], p["ssd_dt_bias"].reshape(1, 8), p["ssd_a_neg"].reshape(1, 8), ssd[2], ssd[3],
            jnp.repeat(p["ssd_d"], HEAD).reshape(1, W_BR), p["ssd_norm"].reshape(1, W_BR),
            gla[0], gla[1], gla[2], p["gla_g2"], p["gla_gb"], p["gla_norm"].reshape(1, W_BR)]
    return pl.pallas_call(
        _layer_out_kernel,
        grid=(b // nb, nc),
        in_specs=in_specs,
        out_specs=row(d),
        out_shape=jax.ShapeDtypeStruct((b, L, d), F32),
        compiler_params=_params(("parallel", "parallel")),
        name="layer_out",
    )(*args)


def _permute_w_in(w_in):
    rw = w_in[:, 0:1280]
    s5 = w_in[:, 1280:1792]
    o = 1792
    x, bm, cm = w_in[:, o:o + 256], w_in[:, o + 256:o + 384], w_in[:, o + 384:o + 512]
    dt, z = w_in[:, o + 512:o + 520], w_in[:, o + 520:o + 776]
    rep = lambda m: jnp.concatenate([m[:, 0:64], m[:, 0:64], m[:, 64:128], m[:, 64:128]], axis=1)
    pad = lambda n: jnp.zeros((w_in.shape[0], n), w_in.dtype)
    ssd = jnp.concatenate([x, rep(bm), rep(cm), z, dt, pad(120)], axis=1)
    o = 1792 + 776
    q, k, v = w_in[:, o:o + 128], w_in[:, o + 128:o + 256], w_in[:, o + 256:o + 512]
    gl, gate = w_in[:, o + 512:o + 544], w_in[:, o + 544:o + 800]
    gla = jnp.concatenate([q, k, v, gate, gl, pad(96)], axis=1)
    return jnp.concatenate([rw, s5, ssd, gla], axis=1).astype(BF16)


def _layer_params(l, a):
    p = {k: v[l] for k, v in a.items()}
    rep = lambda m: jnp.concatenate([m[..., 0:64], m[..., 0:64], m[..., 64:128], m[..., 64:128]], axis=-1)
    cw, cb = p["ssd_conv_w"], p["ssd_conv_b"]
    p["conv_w_rep"] = jnp.concatenate([cw[:, 0:256], rep(cw[:, 256:384]), rep(cw[:, 384:512])], axis=1)
    p["conv_b_rep"] = jnp.concatenate([cb[0:256], rep(cb[256:384]), rep(cb[384:512])], axis=0)
    p["ssd_a_neg"] = -jnp.exp(p["ssd_a_log"])
    p["w_in_p"] = _permute_w_in(p["w_in"])
    p["s5_ops"] = _s5_weights(p)
    return p


def _mixer_layer(h, mod3, shared_mod_row, p, is_grid, init, need_out):
    b, L, d = h.shape
    tm = min(512, L)
    mod_row = (lambda i: i) if shared_mod_row is None else (lambda i: shared_mod_row)
    f_rw, f_s5, f_ssd, f_gla = _inproj(h, mod3, mod_row, p["norm_pre"], p["w_in_p"], tm)
    if init is None:
        init = (jnp.zeros((b, 2, HEAD, W_BR), F32), jnp.zeros((b, S5_G, 256), F32),
                jnp.zeros((b, 2, HEAD, W_BR), F32), jnp.zeros((b, 2, GLA_DK, W_BR), F32))
    (rt, y0, bonus, m_rw, c_rw), (ssd_c, ssd_d, ssd_xbc), (gla_c, gla_d) = _chunk_local(f_rw, f_ssd, f_gla, p, is_grid)
    rw_sf, rw_sr, rw_fin = _scan_call(_rw_scan_kernel, "rwkv_scan", m_rw, c_rw, init[0])
    kmat, fmat, emat, lam_re, lam_im = p["s5_ops"]
    ug, z5 = _s5_local(f_s5, fmat)
    s5_xf, s5_xr, s5_fin = _s5_scan(z5, lam_re, lam_im, init[1])
    ssd_sf, ssd_sr, ssd_fin = _scan_call(_ew_scan_kernel, "ssd_scan", ssd_c, ssd_d, init[2])
    gla_sf, gla_sr, gla_fin = _scan_call(_ew_scan_kernel, "gla_scan", gla_c, gla_d, init[3])
    states = (rw_fin, s5_fin, ssd_fin, gla_fin)
    if not need_out:
        return None, states
    y5a, y5b = _s5_out(ug, s5_xf, s5_xr, kmat, emat)
    h_new = _layer_out(h, mod3, shared_mod_row,
                       (y0, bonus, rt, rw_sf, rw_sr, f_rw), (y5a, y5b, f_s5),
                       (ssd_xbc, f_ssd, ssd_sf, ssd_sr), (f_gla, gla_sf, gla_sr), p)
    return h_new, states


def kernel(x, c, ctx, c_ctx, ada_w, ada_b, norm_pre, norm_post, w_in, w_out, rw_mu, rw_w0, rw_w2, rw_a0, rw_a2, rw_kk, rw_ka, rw_rk, rw_ln_w, rw_ln_b, s5_a_re, s5_a_im, s5_log_dt, s5_b_re, s5_b_im, s5_c_re, s5_c_im, s5_d, s5_glu_w, s5_glu_b, ssd_conv_w, ssd_conv_b, ssd_dt_bias, ssd_a_log, ssd_d, ssd_norm, gla_g2, gla_gb, gla_norm):
    stacked = dict(norm_pre=norm_pre, norm_post=norm_post, w_in=w_in, w_out=w_out, rw_mu=rw_mu, rw_w0=rw_w0,
                   rw_w2=rw_w2, rw_a0=rw_a0, rw_a2=rw_a2, rw_kk=rw_kk, rw_ka=rw_ka, rw_rk=rw_rk, rw_ln_w=rw_ln_w,
                   rw_ln_b=rw_ln_b, s5_a_re=s5_a_re, s5_a_im=s5_a_im, s5_log_dt=s5_log_dt, s5_b_re=s5_b_re,
                   s5_b_im=s5_b_im, s5_c_re=s5_c_re, s5_c_im=s5_c_im, s5_d=s5_d, s5_glu_w=s5_glu_w,
                   s5_glu_b=s5_glu_b, ssd_conv_w=ssd_conv_w, ssd_conv_b=ssd_conv_b, ssd_dt_bias=ssd_dt_bias,
                   ssd_a_log=ssd_a_log, ssd_d=ssd_d, ssd_norm=ssd_norm, gla_g2=gla_g2, gla_gb=gla_gb,
                   gla_norm=gla_norm)
    depth = ada_w.shape[0]
    b, d = c.shape
    rows = -(-(b + 1) // 8) * 8
    cvec = jnp.concatenate([c, c_ctx[None, :], jnp.zeros((rows - b - 1, d), F32)], axis=0)
    mod = _modulation(cvec, ada_w, ada_b)
    h, hc = x, ctx
    for l in range(depth):
        p = _layer_params(l, stacked)
        mod3 = mod[l].reshape(rows, 1, 3 * d)
        last = l == depth - 1
        hc_next, ctx_states = _mixer_layer(hc, mod3, b, p, False, None, not last)
        h, _ = _mixer_layer(h, mod3, None, p, True, ctx_states, True)
        hc = hc_next
    return h
```

```python
import functools
import math
import operator

import jax
import jax.numpy as jnp
from jax import lax
from jax.experimental import pallas as pl
from jax.experimental.pallas import tpu as pltpu

F32 = jnp.float32
BF16 = jnp.bfloat16

EPS = 1e-6
GRID_W = 64
W_BR = 256
CH = 64
HEAD = 64
RW_GN_EPS = 64e-5
RW_LORA = 64
S5_G, S5_CH, S5_P, S5_T = 16, 16, 64, 16
SSD_CONV = 5
GLA_RANK = 16
GLA_DK = 32
GLA_TAU = 16.0
W_RW, W_S5, W_SSD, W_GLA = 1280, 512, 1152, 896
N_PROJ = W_RW + W_S5 + W_SSD + W_GLA
VMEM_LIMIT = 48 * 1024 * 1024
NB_RW = 8
NB = 8
SCAN_CB = 8
SCAN_NB = 4


class _Lk:
    def __init__(self, xs):
        self.xs = list(xs)

    @property
    def shape(self):
        return self.xs[0].shape

    def __getitem__(self, idx):
        return _Lk([x[idx] for x in self.xs])

    def astype(self, dt):
        return _Lk([x.astype(dt) for x in self.xs])


def _pick(a, i):
    if isinstance(a, _Lk):
        return a.xs[i]
    if isinstance(a, (list, tuple)):
        return [_pick(e, i) for e in a]
    return a


def _width(a):
    if isinstance(a, _Lk):
        return len(a.xs)
    if isinstance(a, (list, tuple)):
        for e in a:
            w = _width(e)
            if w:
                return w
    return 0


def _lift(f):
    def g(*args, **kw):
        n = max(_width(a) for a in args)
        if not n:
            return f(*args, **kw)
        return _Lk([f(*[_pick(a, i) for a in args], **kw) for i in range(n)])
    return g


for _name, _op in (("add", operator.add), ("sub", operator.sub), ("mul", operator.mul), ("truediv", operator.truediv)):
    setattr(_Lk, f"__{_name}__", lambda s, o, _f=_lift(_op): _f(s, o))
    setattr(_Lk, f"__r{_name}__", lambda s, o, _f=_lift(_op): _f(o, s))
_Lk.__neg__ = lambda s: _lift(operator.neg)(s)

_exp = _lift(jnp.exp)
_log = _lift(jnp.log)
_tanh = _lift(jnp.tanh)
_abs = _lift(jnp.abs)
_maximum = _lift(jnp.maximum)
_rsqrt = _lift(lax.rsqrt)
_where = _lift(jnp.where)
_cat = _lift(lambda parts, axis: jnp.concatenate(parts, axis=axis))
_roll = _lift(pltpu.roll)
_bcast = _lift(jnp.broadcast_to)
_mean_last = _lift(lambda x: jnp.mean(x, axis=-1, keepdims=True))


def _load(ref, *idx):
    return _Lk([ref[(s,) + idx] for s in range(ref.shape[0])])


def _store(ref, idx, val):
    for s in range(ref.shape[0]):
        ref[(s,) + idx] = val.xs[s]


_dg = _lift(lambda a, b, mode: lax.dot_general(
    a, b, ((({"nn": 1, "nt": 1, "tn": 0}[mode],), ({"nn": 0, "nt": 1, "tn": 0}[mode],)), ((), ())),
    preferred_element_type=F32))


def _bf(a):
    return a.astype(BF16)


def _split2(a):
    hi = _bf(a)
    return hi, _bf(a - hi.astype(F32))


def _split3(a):
    hi = _bf(a)
    r = a - hi.astype(F32)
    mid = _bf(r)
    return hi, mid, _bf(r - mid.astype(F32))


def _hilo(a):
    hi = _bf(a).astype(F32)
    return hi, a - hi


def dot1(a, b, mode="nn"):
    return _dg(_bf(a), _bf(b), mode)


def dot3(a, b, mode="nn"):
    ah, al = _split2(a)
    bh, bl = _split2(b)
    return _dg(ah, bh, mode) + (_dg(ah, bl, mode) + _dg(al, bh, mode))


def dotk3(a, b, mode="nn"):
    ah, al = _hilo(a)
    bh, bl = _hilo(b)
    ax = 0 if mode == "tn" else 1
    bx = 1 if mode == "nt" else 0
    return _dg(_bf(_cat([ah, al, ah], ax)), _bf(_cat([bh, bh, bl], bx)), mode)


def dot_xl(a_exact, b, mode="nn"):
    b1, b2, b3 = _split3(b)
    return _dg(a_exact, b1, mode) + (_dg(a_exact, b2, mode) + _dg(a_exact, b3, mode))


def dot_xl2(a_exact, b, mode="nn"):
    b1, b2 = _split2(b)
    return _dg(a_exact, b1, mode) + _dg(a_exact, b2, mode)


def dot_xr(a, b_exact, mode="nn"):
    a1, a2, a3 = _split3(a)
    return _dg(a1, b_exact, mode) + (_dg(a2, b_exact, mode) + _dg(a3, b_exact, mode))


def _sigmoid(x):
    return 1.0 / (1.0 + _exp(-x))


def _silu(x):
    return x * _sigmoid(x)


def _softplus(x):
    return _maximum(x, 0.0) + _log(1.0 + _exp(-_abs(x)))


def _iota(shape, dim):
    return lax.broadcasted_iota(jnp.int32, shape, dim)


def _tri_mask(n, rev, strict):
    r, c = _iota((n, n), 0), _iota((n, n), 1)
    if rev:
        return (r < c) if strict else (r <= c)
    return (r > c) if strict else (r >= c)


def _block_ones(n, blk):
    return (_iota((n, n), 0) // blk == _iota((n, n), 1) // blk).astype(BF16)


def _expand_blockdiag(compact, n_heads=4):
    lane_head = _iota((1, compact.shape[1]), 1) // HEAD
    return _cat([_where(lane_head == h, compact, 0.0) for h in range(n_heads)], 0)


def _compact_blockdiag(blk, rows_per_head, n_heads=4):
    lane_head = _iota((1, blk.shape[1]), 1) // HEAD
    out = _where(lane_head == 0, blk[0:rows_per_head], 0.0)
    for h in range(1, n_heads):
        out = out + _where(lane_head == h, blk[h * rows_per_head:(h + 1) * rows_per_head], 0.0)
    return out


def _expand_heads(cols, first, n_heads=4):
    lane_head = _iota((1, n_heads * HEAD), 1) // HEAD
    out = _where(lane_head == 0, cols[:, first:first + 1], 0.0)
    for h in range(1, n_heads):
        out = _where(lane_head == h, cols[:, first + h:first + h + 1], out)
    return out


def _shift_rows(x, edge_row, down):
    n = x.shape[0]
    rows = _iota((n, 1), 0)
    if down:
        return _where(rows == 0, edge_row, _roll(x, 1, 0))
    return _where(rows == n - 1, edge_row, _roll(x, n - 1, 0))


def _params(sem):
    return pltpu.CompilerParams(dimension_semantics=sem, vmem_limit_bytes=VMEM_LIMIT)


def _full(shape):
    nd = len(shape)
    return pl.BlockSpec(shape, lambda *_: (0,) * nd)


def _nb(b, want):
    return want if b % want == 0 else 1


def _mod_kernel(c_ref, w_ref, b_ref, o_ref):
    s = _silu(c_ref[...])
    o_ref[0] = dot3(s, w_ref[0]) + b_ref[0]


def _modulation(cvec, ada_w, ada_b):
    depth, d, n = ada_w.shape
    rows = cvec.shape[0]
    tn = 1024
    return pl.pallas_call(
        _mod_kernel,
        grid=(depth, n // tn),
        in_specs=[pl.BlockSpec((rows, d), lambda l, j: (0, 0)),
                  pl.BlockSpec((1, d, tn), lambda l, j: (l, 0, j)),
                  pl.BlockSpec((1, 1, tn), lambda l, j: (l, 0, j))],
        out_specs=pl.BlockSpec((1, rows, tn), lambda l, j: (l, 0, j)),
        out_shape=jax.ShapeDtypeStruct((depth, rows, n), F32),
        compiler_params=_params(("parallel", "parallel")),
        name="adaln_mod",
    )(cvec, ada_w, ada_b.reshape(depth, 1, n))


def _inproj_kernel(x_ref, mod_ref, g_ref, w_ref, o_rw, o_s5, o_ssd, o_gla):
    x = x_ref[0]
    d = x.shape[1]
    hn = x * lax.rsqrt(jnp.mean(x * x, axis=-1, keepdims=True) + EPS) * g_ref[...]
    m = mod_ref[0]
    hn = hn * (1.0 + m[:, d:2 * d]) + m[:, 0:d]
    p = dot1(hn, w_ref[...])
    o_rw[0] = p[:, 0:W_RW]
    o_s5[0] = p[:, W_RW:W_RW + W_S5]
    o_ssd[0] = p[:, W_RW + W_S5:W_RW + W_S5 + W_SSD]
    o_gla[0] = p[:, W_RW + W_S5 + W_SSD:N_PROJ]


def _inproj(h, mod3, mod_row, norm_pre, w_in_p, tm):
    b, L, d = h.shape
    widths = (W_RW, W_S5, W_SSD, W_GLA)
    return pl.pallas_call(
        _inproj_kernel,
        grid=(b, L // tm),
        in_specs=[pl.BlockSpec((1, tm, d), lambda i, j: (i, j, 0)),
                  pl.BlockSpec((1, 1, 3 * d), lambda i, j: (mod_row(i), 0, 0)),
                  _full((1, d)),
                  _full(w_in_p.shape)],
        out_specs=[pl.BlockSpec((1, tm, w), lambda i, j: (i, j, 0)) for w in widths],
        out_shape=[jax.ShapeDtypeStruct((b, L, w), F32) for w in widths],
        compiler_params=_params(("parallel", "parallel")),
        name="in_proj",
    )(h, mod3, norm_pre.reshape(1, d), w_in_p)


def _tri_inv_stages(n_mats):
    r, c = _iota((CH, CH), 0), _iota((CH, CH), 1)
    blk16 = (r // 16) == (c // 16)
    blk32 = (r // 32) == (c // 32)
    eye = jnp.where(r == c, 1.0, 0.0)
    nd = [_where(blk16, n, 0.0) for n in n_mats]
    d = [eye + x for x in nd]
    p = [dot1(x, x) for x in nd]
    yield "h"
    for _ in range(2):
        d = [x + dot1(x, y) for x, y in zip(d, p)]
        p = [dot1(y, y) for y in p]
        yield "h"
    d = [x + dot1(x, y) for x, y in zip(d, p)]
    yield "h"
    for sel in (blk32 & jnp.logical_not(blk16), jnp.logical_not(blk32)):
        od = [dot1(_where(sel, n, 0.0), x) for n, x in zip(n_mats, d)]
        yield "h"
        d = [x + dot1(x, y) for x, y in zip(d, od)]
        yield "h"
    return d


def _run_staggered(gens, width):
    pending, active, values = list(enumerate(gens)), [], [None] * len(gens)
    while pending or active:
        while pending and len(active) < width:
            active.append(pending.pop(0))
        for item in list(active):
            try:
                next(item[1])
            except StopIteration as stop:
                values[item[0]] = stop.value
                active.remove(item)
    return values


def _member_io(members):
    def ld(ref, *idx):
        return _Lk([ref[(s,) + idx] for s in members])

    def st(ref, idx, val):
        for j, s in enumerate(members):
            ref[(s,) + idx] = val.xs[j]
    return ld, st


def _rw_local_stages(members, is_grid, nc, cur_ref, prev_ref, next_ref, mu_ref, w0_ref, w2_ref, a0_ref, a2_ref,
                     kk_ref, ka_ref, rk_ref, rt_ref, y0_ref, bonus_ref, m_ref, c_ref):
    _load, _store = _member_io(members)

    ci = pl.program_id(1)
    has_prev = (ci > 0).astype(F32)
    has_next = (ci < nc - 1).astype(F32)
    z = _load(cur_ref, slice(None), slice(0, 1024))
    if is_grid:
        zero_row = jnp.zeros((1, W_BR), F32)
        left = _shift_rows(z[:, 0:256], zero_row, True)
        right = _shift_rows(z[:, 256:512], zero_row, False)
        up = _load(prev_ref) * has_prev
        down = _load(next_ref) * has_next
        sh = _cat([left, right, up, down], 1)
    else:
        prow = _load(prev_ref, slice(7, 8)) * has_prev
        nrow = _load(next_ref, slice(0, 1)) * has_next
        sh = _cat([_shift_rows(z[:, 0:512], prow, True), _shift_rows(z[:, 512:1024], nrow, False)], 1)
    zs = z + mu_ref[...] * (sh - z)
    yield "p"
    r, k, v = zs[:, 0:256], zs[:, 256:512], zs[:, 512:768]
    wl, al = zs[:, 768:896], zs[:, 896:1024]

    bo = _block_ones(W_BR, HEAD)
    kk = k * kk_ref[...]
    kk = kk * _rsqrt(dot1(kk * kk, bo) + EPS)
    yield "p"
    eye = _iota((HEAD, HEAD), 0) == _iota((HEAD, HEAD), 1)
    heads = lambda t: [t[:, HEAD * h:HEAD * (h + 1)] for h in range(4)]
    v_h = heads(v)
    row2, col2 = _iota((CH, 2 * CH), 0), _iota((CH, 2 * CH), 1) % CH

    bonus = None
    pa_l, pam_l, prm_l, pr_l, nb_l, nk_l, eb_l, ek_l, dg_l, strict_l, incl2_l = ([] for _ in range(11))
    for d in range(2):
        rev = d == 1
        w_pre = w0_ref[d:d + 1, :] + dot1(_tanh(wl[:, RW_LORA * d:RW_LORA * (d + 1)]), w2_ref[d])
        lw = (-math.exp(-0.5)) * _sigmoid(w_pre)
        a = _sigmoid(a0_ref[d:d + 1, :] + dot1(al[:, RW_LORA * d:RW_LORA * (d + 1)], a2_ref[d]))
        k_d = k * (1.0 + (a - 1.0) * ka_ref[...])
        bterm = dot1(r * k_d * rk_ref[...], bo) * v
        bonus = bterm if bonus is None else bonus + bterm
        yield "p"

        cs = dot_xl2(_tri_mask(CH, rev, False).astype(BF16), lw)
        cs_ex = cs - lw
        cs_tot = cs[0:1] if rev else cs[CH - 1:CH]
        cs_mid = cs[CH // 2:CH // 2 + 1]
        kka = kk * a
        e_mid = _exp(cs_mid - cs)
        e_tot = _exp(cs_tot - cs)
        yield "p"
        pa_l += heads(-kk * _exp(cs_ex))
        pr_l += heads(r * _exp(cs))
        yield "p"
        pam_l += heads(-kk * _exp(cs_ex - cs_mid))
        prm_l += heads(r * _exp(cs - cs_mid))
        yield "p"
        nb_l += heads(kka * e_mid)
        nk_l += heads(k_d * e_mid)
        yield "p"
        eb_l += heads(kka * e_tot)
        ek_l += heads(k_d * e_tot)
        dg_l += [_where(eye, t, 0.0) for t in heads(_exp(cs_tot))]
        strict_l += [_tri_mask(CH, rev, True)] * 4
        incl2_l += [(row2 <= col2) if rev else (row2 >= col2)] * 4
        yield "p"
    vv = v_h + v_h
    idx = range(8)

    quad = [dot1(_cat([pam_l[i], prm_l[i]], 0), _cat([nb_l[i], nk_l[i]], 0), "nt") for i in idx]
    yield "h"
    n_l = [_where(strict_l[i], quad[i][0:CH, 0:CH], 0.0) for i in idx]
    aak = [_where(strict_l[i], quad[i][0:CH, CH:2 * CH], 0.0) for i in idx]
    g2 = [_where(incl2_l[i], quad[i][CH:2 * CH], 0.0) for i in idx]
    t_inv = yield from _tri_inv_stages(n_l)
    av = [dot1(aak[i], vv[i]) for i in idx]
    yield "h"
    wu = [dot1(t_inv[i], _cat([pa_l[i], av[i]], 1)) for i in idx]
    yield "h"
    zero = jnp.zeros((CH, HEAD), F32)
    rhs2 = [_cat([wu[i], _cat([zero, vv[i]], 1)], 0) for i in idx]
    gy = [dot1(g2[i], rhs2[i]) for i in idx]
    yield "h"
    mc = [dot1(_cat([eb_l[i], ek_l[i]], 0), rhs2[i], "tn") for i in idx]
    yield "h"
    y0_parts = []
    for d in range(2):
        sl = range(4 * d, 4 * d + 4)
        _store(rt_ref, (0, d), _cat([pr_l[i] + gy[i][:, 0:HEAD] for i in sl], 1))
        _store(m_ref, (0, d), _cat([dg_l[i] + mc[i][:, 0:HEAD] for i in sl], 1))
        _store(c_ref, (0, d), _cat([mc[i][:, HEAD:2 * HEAD] for i in sl], 1))
        y0_parts.append(_cat([gy[i][:, HEAD:2 * HEAD] for i in sl], 1))
    _store(y0_ref, (), y0_parts[0] + y0_parts[1])
    _store(bonus_ref, (), bonus)


def _halo_specs(nb, is_grid, nc):
    if is_grid:
        prev = pl.BlockSpec((nb, CH, W_BR), lambda b, c: (b, jnp.maximum(c - 1, 0), 2))
        nxt = pl.BlockSpec((nb, CH, W_BR), lambda b, c: (b, jnp.minimum(c + 1, nc - 1), 3))
    else:
        rb = CH // 8
        prev = pl.BlockSpec((nb, 8, 512), lambda b, c: (b, jnp.maximum(c * rb - 1, 0), 0))
        nxt = pl.BlockSpec((nb, 8, 512), lambda b, c: (b, jnp.minimum((c + 1) * rb, nc * rb - 1), 1))
    return prev, nxt


def _scan_specs(nb, cb, nblk, rows, width):
    in_f = pl.BlockSpec((nb, cb, 1, rows, width), lambda i, j: (i, j, 0, 0, 0))
    in_r = pl.BlockSpec((nb, cb, 1, rows, width), lambda i, j: (i, nblk - 1 - j, 1, 0, 0))
    out_f = pl.BlockSpec((nb, cb, rows, width), lambda i, j: (i, j, 0, 0))
    out_r = pl.BlockSpec((nb, cb, rows, width), lambda i, j: (i, nblk - 1 - j, 0, 0))
    st = pl.BlockSpec((nb, 2, rows, width), lambda i, j: (i, 0, 0, 0))
    return in_f, in_r, out_f, out_r, st


def _rw_scan_kernel(nblk, cb, mf_ref, mr_ref, cf_ref, cr_ref, init_ref, sf_ref, sr_ref, fin_ref, s_scr):
    ci = pl.program_id(1)

    @pl.when(ci == 0)
    def _():
        s_scr[...] = init_ref[...]

    heads = lambda t: [t[:, HEAD * h:HEAD * (h + 1)] for h in range(4)]
    s = [_load(s_scr, 0), _load(s_scr, 1)]
    for j in range(cb):
        new = []
        for d, (m_ref, c_ref, o_ref) in enumerate(((mf_ref, cf_ref, sf_ref), (mr_ref, cr_ref, sr_ref))):
            jj = cb - 1 - j if d else j
            _store(o_ref, (jj,), s[d])
            m_h = heads(_load(m_ref, jj, 0))
            s_h = heads(s[d])
            prod = [dotk3(m_h[h], s_h[h]) for h in range(4)]
            new.append(_cat(prod, 1) + _load(c_ref, jj, 0))
        s = new
    _store(s_scr, (0,), s[0])
    _store(s_scr, (1,), s[1])

    @pl.when(ci == nblk - 1)
    def _():
        _store(fin_ref, (0,), s[0])
        _store(fin_ref, (1,), s[1])


def _scan_call(kernel_fn, name, a, b_arr, init):
    b, nc, _, rows, width = a.shape
    nb = _nb(b, SCAN_NB)
    cb = SCAN_CB if nc % SCAN_CB == 0 else nc
    nblk = nc // cb
    in_f, in_r, out_f, out_r, st = _scan_specs(nb, cb, nblk, rows, width)
    b_f, b_r = _scan_specs(nb, cb, nblk, b_arr.shape[3], width)[0:2]
    ent_shape = jax.ShapeDtypeStruct((b, nc, rows, width), F32)
    return pl.pallas_call(
        functools.partial(kernel_fn, nblk, cb),
        grid=(b // nb, nblk),
        in_specs=[in_f, in_r, b_f, b_r, st],
        out_specs=[out_f, out_r, st],
        out_shape=[ent_shape, ent_shape, jax.ShapeDtypeStruct((b, 2, rows, width), F32)],
        scratch_shapes=[pltpu.VMEM((nb, 2, rows, width), F32)],
        compiler_params=_params(("parallel", "arbitrary")),
        name=name,
    )(a, a, b_arr, b_arr, init)


def _rw_out_stages(members, y0_ref, bonus_ref, rt_ref, sf_ref, sr_ref, gate_ref, lnw_ref, lnb_ref):
    _load = _member_io(members)[0]
    y = _load(y0_ref)
    y = y + dot1(_load(rt_ref, 0, 0), _expand_blockdiag(_load(sf_ref, 0)))
    y = y + dot1(_load(rt_ref, 0, 1), _expand_blockdiag(_load(sr_ref, 0)))
    yield
    bo = _block_ones(W_BR, HEAD)
    y_hi, y_lo = _split2(y)
    mu = (_dg(y_hi, bo, "nn") + _dg(y_lo, bo, "nn")) * (1.0 / HEAD)
    yield
    yc = y - mu
    var = dot1(yc * yc, bo) * (1.0 / HEAD)
    yield
    yn = yc * _rsqrt(var + RW_GN_EPS) * lnw_ref[...] + lnb_ref[...]
    return (yn + _load(bonus_ref)) * _silu(_load(gate_ref))


def _ew_scan_kernel(nblk, cb, cf_ref, cr_ref, df_ref, dr_ref, init_ref, sf_ref, sr_ref, fin_ref, s_scr):
    ci = pl.program_id(1)

    @pl.when(ci == 0)
    def _():
        s_scr[...] = init_ref[...]

    s = [_load(s_scr, 0), _load(s_scr, 1)]
    for j in range(cb):
        for d, (c_ref, d_ref, o_ref) in enumerate(((cf_ref, df_ref, sf_ref), (cr_ref, dr_ref, sr_ref))):
            jj = cb - 1 - j if d else j
            _store(o_ref, (jj,), s[d])
            dec = _load(d_ref, jj, 0)
            if dec.shape[0] != s[d].shape[0]:
                dec = dec[0:1]
            s[d] = dec * s[d] + _load(c_ref, jj, 0)
    _store(s_scr, (0,), s[0])
    _store(s_scr, (1,), s[1])

    @pl.when(ci == nblk - 1)
    def _():
        _store(fin_ref, (0,), s[0])
        _store(fin_ref, (1,), s[1])


def _gla_log_decay(gl, g2_ref, gb_ref, d):
    x = dot3(gl[:, GLA_RANK * d:GLA_RANK * (d + 1)], g2_ref[d]) + gb_ref[d:d + 1, :]
    return -_softplus(-x) * (1.0 / GLA_TAU)


def _gla_local_stages(members, f_ref, g2_ref, gb_ref, cst_ref, dec_ref):
    ld, st = _member_io(members)
    k = ld(f_ref, slice(None), slice(128, 256))
    v = ld(f_ref, slice(None), slice(256, 512))
    gl = ld(f_ref, slice(None), slice(768, 800))
    sel = (_iota((128, W_BR), 0) // GLA_DK) == (_iota((128, W_BR), 1) // HEAD)
    first_row = _iota((CH, 128), 0) == 0
    ones = jnp.ones((CH, W_BR), BF16)
    las = [_gla_log_decay(gl, g2_ref, gb_ref, d) for d in range(2)]
    yield "p"
    css = [dot_xl(_tri_mask(CH, d == 1, False).astype(BF16), las[d]) for d in range(2)]
    yield "p"
    tots = [css[0][CH - 1:CH], css[1][0:1]]
    kes = [k * _exp(tots[d] - css[d]) for d in range(2)]
    yield "p"
    c_blk = [_where(sel, dot1(kes[d], v, "tn"), 0.0) for d in range(2)]
    yield "p"
    d_blk = [dot_xr(_where(first_row, _exp(tots[d]), 0.0), ones, "tn") for d in range(2)]
    yield "p"
    for d in range(2):
        st(cst_ref, (0, d), _compact_blockdiag(c_blk[d], GLA_DK))
        st(dec_ref, (0, d), _compact_blockdiag(d_blk[d], GLA_DK))


def _gla_out_stages(members, f_ref, sf_ref, sr_ref, g2_ref, gb_ref, norm_ref):
    _load = _member_io(members)[0]
    q = _load(f_ref, slice(None), slice(0, 128)) * (GLA_DK ** -0.5)
    k = _load(f_ref, slice(None), slice(128, 256))
    v = _load(f_ref, slice(None), slice(256, 512))
    gate = _load(f_ref, slice(None), slice(512, 768))
    gl = _load(f_ref, slice(None), slice(768, 800))
    khead = _iota((1, 128), 1) // GLA_DK
    vhead = _iota((1, W_BR), 1) // HEAD
    las = [_gla_log_decay(gl, g2_ref, gb_ref, d) for d in range(2)]
    yield
    css = [dot_xl(_tri_mask(CH, d == 1, False).astype(BF16), las[d]) for d in range(2)]
    yield
    mids = [css[d][CH // 2:CH // 2 + 1] for d in range(2)]
    qes = [q * _exp(css[d] - mids[d]) for d in range(2)]
    kns = [k * _exp(mids[d] - css[d]) for d in range(2)]
    yield
    sc = []
    for d in range(2):
        sc.append([_where(_tri_mask(CH, d == 1, False), dot1(_where(khead == h, qes[d], 0.0), kns[d], "nt"), 0.0)
                   for h in range(4)])
        yield
    inter = [dot1(q * _exp(css[d]), _expand_blockdiag(_load(s_ref, 0)))
             for d, s_ref in enumerate((sf_ref, sr_ref))]
    yield
    pv = [dot1(sc[0][h] + sc[1][h], _where(vhead == h, v, 0.0)) for h in range(4)]
    yield
    y = (inter[0] + inter[1]) + ((pv[0] + pv[1]) + (pv[2] + pv[3]))
    ms = dot1(y * y, _block_ones(W_BR, HEAD)) * (1.0 / HEAD)
    yield
    y = y * _rsqrt(ms + EPS) * norm_ref[...]
    return y * _silu(gate)


def _ssd_steps(dt_raw, dtb_ref, aneg_ref):
    dt = _softplus(dt_raw + dtb_ref[...])
    return dt, dt * aneg_ref[...]


def _ssd_cumsum(la):
    lane = _iota((1, 8), 1)
    la3 = _split3(la)
    tril, triu = (_tri_mask(CH, r, False).astype(BF16) for r in (False, True))
    cs_f = _dg(tril, la3[0], "nn") + (_dg(tril, la3[1], "nn") + _dg(tril, la3[2], "nn"))
    cs_r = _dg(triu, la3[0], "nn") + (_dg(triu, la3[1], "nn") + _dg(triu, la3[2], "nn"))
    cs = _where(lane < 4, cs_f, cs_r)
    return cs, _where(lane < 4, cs[CH - 1:CH], cs[0:1]), la3


def _ssd_local_stages(members, nc, cur_ref, prev_ref, next_ref, cw_ref, cb_ref, dtb_ref, aneg_ref,
                      cst_ref, dec_ref, xbc_ref):
    ld, st = _member_io(members)
    ci = pl.program_id(1)
    has_prev = (ci > 0).astype(F32)
    has_next = (ci < nc - 1).astype(F32)
    xbc = ld(cur_ref, slice(None), slice(0, 768))
    ext = _cat([ld(prev_ref) * has_prev, xbc, ld(next_ref) * has_next], 0)
    acc = cb_ref[...] + cw_ref[0:1, :] * ext[6:6 + CH]
    for j in range(1, SSD_CONV):
        acc = acc + cw_ref[j:j + 1, :] * ext[6 + j:6 + j + CH]
        yield "p"
    xbc = _silu(acc)
    st(xbc_ref, (), xbc)
    yield "p"
    x, b_rep = xbc[:, 0:256], xbc[:, 256:512]
    dt, la = _ssd_steps(ld(cur_ref, slice(None), slice(1024, 1032)), dtb_ref, aneg_ref)
    cs, cs_tot, _ = _ssd_cumsum(la)
    yield "p"
    wgt = _exp(cs_tot - cs) * dt
    sel = (_iota((W_BR, W_BR), 0) // HEAD) == (_iota((W_BR, W_BR), 1) // HEAD)
    e_tot = _exp(cs_tot)
    xs = [x * _expand_heads(wgt, 4 * d) for d in range(2)]
    yield "p"
    c_blk = [_where(sel, dot1(b_rep, xs[d], "tn"), 0.0) for d in range(2)]
    yield "p"
    for d in range(2):
        st(cst_ref, (0, d), _compact_blockdiag(c_blk[d], HEAD))
        st(dec_ref, (0, d), _bcast(_expand_heads(e_tot, 4 * d), (8, W_BR)))


def _local_kernel(is_grid, nc, *refs):
    rw_in, ssd_in, gla_in = refs[0:11], refs[11:18], refs[18:21]
    rw_out, ssd_out, gla_out = refs[21:26], refs[26:29], refs[29:31]
    nb = rw_in[0].shape[0]
    pairs = [range(s, min(s + 2, nb)) for s in range(0, nb, 2)]
    order = []
    for g in pairs:
        order += [_rw_local_stages(g, is_grid, nc, *rw_in, *rw_out), _ssd_local_stages(g, nc, *ssd_in, *ssd_out),
                  _gla_local_stages(g, *gla_in, *gla_out)]
    _run_staggered(order, 8)


def _chunk_local(f_rw, f_ssd, f_gla, p, is_grid):
    b, L, _ = f_rw.shape
    nc = L // CH
    nb = _nb(b, NB_RW)
    rb = CH // 8
    blk = lambda *shape: pl.BlockSpec((nb,) + shape, lambda i, c: (i, c) + (0,) * (len(shape) - 1))
    rw_prev, rw_next = _halo_specs(nb, is_grid, nc)
    ssd_prev = pl.BlockSpec((nb, 8, 768), lambda i, c: (i, jnp.maximum(c * rb - 1, 0), 0))
    ssd_next = pl.BlockSpec((nb, 8, 768), lambda i, c: (i, jnp.minimum((c + 1) * rb, nc * rb - 1), 0))
    st = lambda rows: (blk(1, 2, rows, W_BR), jax.ShapeDtypeStruct((b, nc, 2, rows, W_BR), F32))
    rows = lambda w: (blk(CH, w), jax.ShapeDtypeStruct((b, L, w), F32))
    outs = [st(HEAD), rows(W_BR), rows(W_BR), st(HEAD), st(HEAD),
            st(HEAD), st(8), rows(768),
            st(GLA_DK), st(GLA_DK)]
    cw = jnp.concatenate([p["conv_w_rep"], jnp.zeros((8 - SSD_CONV, 768), F32)], axis=0)
    res = pl.pallas_call(
        functools.partial(_local_kernel, is_grid, nc),
        grid=(b // nb, nc),
        in_specs=[blk(CH, W_RW), rw_prev, rw_next,
                  _full((1, 1024)), _full((2, W_BR)), _full((2, RW_LORA, W_BR)), _full((2, W_BR)),
                  _full((2, RW_LORA, W_BR)), _full((1, W_BR)), _full((1, W_BR)), _full((1, W_BR)),
                  blk(CH, W_SSD), ssd_prev, ssd_next, _full((8, 768)), _full((1, 768)), _full((1, 8)), _full((1, 8)),
                  blk(CH, W_GLA), _full((2, GLA_RANK, 128)), _full((2, 128))],
        out_specs=[o[0] for o in outs],
        out_shape=[o[1] for o in outs],
        compiler_params=_params(("parallel", "parallel")),
        name="chunk_local",
    )(f_rw, f_rw, f_rw, p["rw_mu"].reshape(1, 1024), p["rw_w0"], p["rw_w2"], p["rw_a0"], p["rw_a2"],
      p["rw_kk"].reshape(1, W_BR), p["rw_ka"].reshape(1, W_BR), p["rw_rk"].reshape(1, W_BR),
      f_ssd, f_ssd, f_ssd, cw, p["conv_b_rep"].reshape(1, 768), p["ssd_dt_bias"].reshape(1, 8),
      p["ssd_a_neg"].reshape(1, 8), f_gla, p["gla_g2"], p["gla_gb"])
    return res[0:5], res[5:8], res[8:10]


def _ssd_out_stages(members, xbc_ref, tail_ref, dtb_ref, aneg_ref, sf_ref, sr_ref, dskip_ref, norm_ref):
    _load = _member_io(members)[0]
    x = _load(xbc_ref, slice(None), slice(0, 256))
    b_rep = _load(xbc_ref, slice(None), slice(256, 512))
    c_rep = _load(xbc_ref, slice(None), slice(512, 768))
    z = _load(tail_ref, slice(None), slice(0, 256))
    dt, la = _ssd_steps(_load(tail_ref, slice(None), slice(256, 264)), dtb_ref, aneg_ref)
    cs, _, la3 = _ssd_cumsum(la)
    yield
    lane_head = _iota((1, W_BR), 1) // HEAD
    tril, triu = (_tri_mask(CH, r, False).astype(BF16) for r in (False, True))
    cst_f = _dg(la3[0], triu, "tn") + (_dg(la3[1], triu, "tn") + _dg(la3[2], triu, "tn"))
    cst_r = _dg(la3[0], tril, "tn") + (_dg(la3[1], tril, "tn") + _dg(la3[2], tril, "tn"))
    cst = _where(_iota((8, 1), 0) < 4, cst_f, cst_r)
    yield
    e_cs = _exp(cs)
    g = [dot1(c_rep[:, 2 * HEAD * grp:2 * HEAD * grp + HEAD], b_rep[:, 2 * HEAD * grp:2 * HEAD * grp + HEAD], "nt")
         for grp in range(2)]
    yield
    inter = [dot1(c_rep * _expand_heads(e_cs, 4 * d), _expand_blockdiag(_load(s_ref, 0)))
             for d, s_ref in enumerate((sf_ref, sr_ref))]
    yield
    xs = [x * _expand_heads(dt, 4 * d) for d in range(2)]
    intra = []
    for d in range(2):
        incl = _tri_mask(CH, d == 1, False)
        for h in range(4):
            col = 4 * d + h
            seg = _exp(_where(incl, cs[:, col:col + 1] - cst[col:col + 1, :], -1e30))
            intra.append(dot1(g[h // 2] * seg, _where(lane_head == h, xs[d], 0.0)))
            if h % 2:
                yield
    y = dskip_ref[...] * x + (inter[0] + inter[1])
    y = y + (((intra[0] + intra[1]) + (intra[2] + intra[3])) + ((intra[4] + intra[5]) + (intra[6] + intra[7])))
    y = y * _silu(z)
    return y * _rsqrt(_mean_last(y * y) + EPS) * norm_ref[...]


def _s5_weights(p):
    t = S5_T
    steps = jnp.arange(t + 1, dtype=F32)
    kmats, fmats, emats, lam_re, lam_im = [], [], [], [], []
    tt = jnp.arange(t)
    tau = tt[None, :] - tt[:, None]
    cmul = lambda x, y: (x[0] * y[0] - x[1] * y[1], x[0] * y[1] + x[1] * y[0])
    for d in range(2):
        a_re, a_im = p["s5_a_re"][d], p["s5_a_im"][d]
        dt = jnp.exp(p["s5_log_dt"][d])[:, None]
        mag = jnp.exp((a_re * dt)[None] * steps[:, None, None])
        ang = (a_im * dt)[None] * steps[:, None, None]
        pw = (mag * jnp.cos(ang), mag * jnp.sin(ang))
        num = (pw[0][1] - 1.0, pw[1][1])
        den = a_re * a_re + a_im * a_im
        quo = ((num[0] * a_re + num[1] * a_im) / den, (num[1] * a_re - num[0] * a_im) / den)
        b_bar = cmul((quo[0][..., None], quo[1][..., None]), (p["s5_b_re"][d], p["s5_b_im"][d]))
        c_c = (p["s5_c_re"][d], p["s5_c_im"][d])
        qb = cmul((pw[0][:t, :, :, None], pw[1][:t, :, :, None]), (b_bar[0][None], b_bar[1][None]))
        hp = lax.Precision.HIGHEST
        kd = (jnp.einsum("gcp,jgpe->gcje", c_c[0], qb[0], precision=hp)
              - jnp.einsum("gcp,jgpe->gcje", c_c[1], qb[1], precision=hp))
        lag = -tau if d else tau
        sel = (lag[:, :, None] == jnp.arange(t)[None, None, :]).astype(F32)
        kmats.append(jnp.einsum("stj,gcje->gsetc", sel, kd, precision=hp).reshape(S5_G, t * S5_CH, t * S5_CH))
        fsel = (lambda x: x) if d else (lambda x: x[::-1])
        fmats += [jnp.transpose(fsel(x), (1, 0, 3, 2)).reshape(S5_G, t * S5_CH, S5_P) for x in qb]
        esel = (lambda x: x[1:][::-1]) if d else (lambda x: x[1:])
        ec = cmul((c_c[0][None], c_c[1][None]), (esel(pw[0])[:, :, None, :], esel(pw[1])[:, :, None, :]))
        ec = [jnp.transpose(x, (1, 3, 0, 2)).reshape(S5_G, S5_P, t * S5_CH) for x in ec]
        emats += [ec[0], -ec[1]]
        lam_re += [pw[0][t], pw[0][t]]
        lam_im += [-pw[1][t], pw[1][t]]
    kmat = kmats[0] + kmats[1]
    fmat = jnp.concatenate(fmats, axis=2)
    emat = jnp.concatenate(emats, axis=1)
    return (kmat.astype(BF16), fmat.astype(BF16), emat.astype(BF16),
            jnp.concatenate(lam_re, axis=1), jnp.concatenate(lam_im, axis=1))


def _s5_local_kernel(ua_ref, ub_ref, f_ref, ug_ref, z_ref):
    rb = z_ref.shape[2]
    step_rows = lambda s: jnp.concatenate([ua_ref[0, pl.ds(s, rb, stride=S5_T), :],
                                           ub_ref[0, pl.ds(s, rb, stride=S5_T), :]], axis=1)
    ug = _block_transpose16([step_rows(s) for s in range(S5_T)])
    ub = [_bf(x) for x in ug]
    zs = [_dg(ub[g], f_ref[g], "nn") for g in range(S5_G)]
    for g in range(S5_G):
        ug_ref[0, g] = ub[g]
        z_ref[0, g] = zs[g]


def _block_transpose16(arrs):
    lane_blk = _iota((1, 256), 1) // 16
    arrs = list(arrs)
    for k in range(4):
        b = 1 << k
        bit = (lane_blk & b) != 0
        new = list(arrs)
        for i in range(16):
            if not i & b:
                lo, hi = arrs[i], arrs[i | b]
                new[i] = jnp.where(bit, pltpu.roll(hi, 16 * b, 1), lo)
                new[i | b] = jnp.where(bit, hi, pltpu.roll(lo, 256 - 16 * b, 1))
        arrs = new
    return arrs


def _s5_rows(nc):
    return 64 if nc % 64 == 0 else nc


def _s5_local(f_s5, fmat):
    b, L, _ = f_s5.shape
    nc = L // S5_T
    rb = _s5_rows(nc)
    grp = pl.BlockSpec((1, S5_G, rb, 256), lambda i, j: (i, 0, j, 0))
    return pl.pallas_call(
        _s5_local_kernel,
        grid=(b, nc // rb),
        in_specs=[pl.BlockSpec((1, rb * S5_T, 128), lambda i, j: (i, j, 0)),
                  pl.BlockSpec((1, rb * S5_T, 128), lambda i, j: (i, j, 1)), _full(fmat.shape)],
        out_specs=[grp, grp],
        out_shape=[jax.ShapeDtypeStruct((b, S5_G, nc, 256), BF16), jax.ShapeDtypeStruct((b, S5_G, nc, 256), F32)],
        compiler_params=_params(("parallel", "parallel")),
        name="s5_local",
    )(f_s5, f_s5, fmat)


def _s5_scan_kernel(nblk, cb, zf_ref, zr_ref, lre_ref, lim_ref, init_ref, xf_ref, xr_ref, fin_ref, x_scr):
    ci = pl.program_id(0)

    @pl.when(ci == 0)
    def _():
        x_scr[...] = init_ref[...]

    lane = _iota((1, 256), 1)
    is_fwd = lane < 128
    even_q = (lane // S5_P) % 2 == 0
    lre, lim = lre_ref[...], lim_ref[...]
    x = _load(x_scr)
    for i in range(cb):
        for s in range(len(x.xs)):
            xf_ref[s, :, i, :] = x.xs[s]
            xr_ref[s, :, cb - 1 - i, :] = x.xs[s]
        z = _Lk([jnp.where(is_fwd, zf_ref[s, :, i, :], zr_ref[s, :, cb - 1 - i, :]) for s in range(len(x.xs))])
        swapped = _where(even_q, _roll(x, 192, 1), _roll(x, 64, 1))
        x = lre * x + lim * swapped + z
    _store(x_scr, (), x)

    @pl.when(ci == nblk - 1)
    def _():
        _store(fin_ref, (), x)


def _s5_scan(z, lam_re, lam_im, init):
    b, g, nc, w = z.shape
    cb = 8
    nblk = nc // cb
    fwd = pl.BlockSpec((b, g, cb, w), lambda j: (0, 0, j, 0))
    rev = pl.BlockSpec((b, g, cb, w), lambda j: (0, 0, nblk - 1 - j, 0))
    ent = jax.ShapeDtypeStruct((b, g, nc, w), F32)
    return pl.pallas_call(
        functools.partial(_s5_scan_kernel, nblk, cb),
        grid=(nblk,),
        in_specs=[fwd, rev, _full((g, w)), _full((g, w)), _full((b, g, w))],
        out_specs=[fwd, rev, _full((b, g, w))],
        out_shape=[ent, ent, jax.ShapeDtypeStruct((b, g, w), F32)],
        scratch_shapes=[pltpu.VMEM((b, g, w), F32)],
        compiler_params=_params(("arbitrary",)),
        name="s5_scan",
    )(z, z, lam_re, lam_im, init)


def _s5_out_kernel(ug_ref, xf_ref, xr_ref, k_ref, e_ref, ya_ref, yb_ref):
    rb = ug_ref.shape[2]
    lane = _iota((1, 256), 1)
    groups = range(S5_G)
    x_ent = [_split2(jnp.where(lane < 128, xf_ref[0, g], xr_ref[0, g])) for g in groups]
    ys = [_dg(ug_ref[0, g], k_ref[g], "nn") + (_dg(x_ent[g][0], e_ref[g], "nn") + _dg(x_ent[g][1], e_ref[g], "nn"))
          for g in groups]
    yt = _block_transpose16(ys)
    for s in range(S5_T):
        ya_ref[0, pl.ds(s, rb, stride=S5_T), :] = yt[s][:, 0:128]
        yb_ref[0, pl.ds(s, rb, stride=S5_T), :] = yt[s][:, 128:256]


def _s5_out(ug, xf, xr, kmat, emat):
    b, g, nc, w = ug.shape
    rb = _s5_rows(nc)
    grp = pl.BlockSpec((1, g, rb, w), lambda i, j: (i, 0, j, 0))
    half = pl.BlockSpec((1, rb * S5_T, 128), lambda i, j: (i, j, 0))
    half_shape = jax.ShapeDtypeStruct((b, nc * S5_T, 128), F32)
    return pl.pallas_call(
        _s5_out_kernel,
        grid=(b, nc // rb),
        in_specs=[grp, grp, grp, _full(kmat.shape), _full(emat.shape)],
        out_specs=[half, half],
        out_shape=[half_shape, half_shape],
        compiler_params=_params(("parallel", "parallel")),
        name="s5_out",
    )(ug, xf, xr, kmat, emat)


def _s5_out_stages(members, ys5a_ref, ys5b_ref, fs5_ref, s5d_ref, gluw_ref, glub_ref):
    _load = _member_io(members)[0]
    u = _load(fs5_ref, slice(None), slice(0, 256))
    y = _cat([_load(ys5a_ref), _load(ys5b_ref)], 1) + s5d_ref[...] * u
    y = 0.5 * y * (1.0 + _tanh(math.sqrt(2.0 / math.pi) * (y + 0.044715 * (y * y * y))))
    yield
    y = y * _sigmoid(dot1(y, gluw_ref[...]) + glub_ref[...])
    yield
    return y * _silu(_load(fs5_ref, slice(None), slice(256, 512)))


def _layer_out_kernel(h_ref, mod_ref, w_ref, g_ref, *refs):
    rw_refs, s5_refs, ssd_refs, gla_refs, o_ref = refs[0:8], refs[8:14], refs[14:22], refs[22:28], refs[28]
    nb, rows, d = h_ref.shape
    pairs = [range(s, min(s + 2, nb)) for s in range(0, nb, 2)]
    gens = []
    for g in pairs:
        gens += [_rw_out_stages(g, *rw_refs), _s5_out_stages(g, *s5_refs), _ssd_out_stages(g, *ssd_refs),
                 _gla_out_stages(g, *gla_refs)]
    vals = _run_staggered(gens, 8)
    tiles = []
    for k in range(len(pairs)):
        ycat = _cat(vals[4 * k:4 * k + 4], 1)
        tiles += ycat.xs
    o = dot1(jnp.concatenate(tiles, axis=0), w_ref[...])
    o = o * lax.rsqrt(jnp.mean(o * o, axis=-1, keepdims=True) + EPS) * g_ref[...]
    for s in range(nb):
        gate = mod_ref[min(s, mod_ref.shape[0] - 1)][:, 2 * d:3 * d]
        o_ref[s] = h_ref[s] + gate * o[s * rows:(s + 1) * rows]


def _layer_out(h, mod3, shared_mod_row, rw, s5, ssd, gla, p):
    b, L, d = h.shape
    nc = L // CH
    nb = _nb(b, NB)
    row = lambda w, col=0: pl.BlockSpec((nb, CH, w), lambda i, c: (i, c, col))
    ent = lambda r: pl.BlockSpec((nb, 1, r, W_BR), lambda i, c: (i, c, 0, 0))
    vec = lambda w: _full((1, w))
    if shared_mod_row is None:
        mod_spec = pl.BlockSpec((nb, 1, 3 * d), lambda i, c: (i, 0, 0))
    else:
        mod_spec = pl.BlockSpec((1, 1, 3 * d), lambda i, c: (shared_mod_row, 0, 0))
    tail = W_SSD - 768
    y0, bonus, rt, rw_sf, rw_sr, f_rw = rw
    in_specs = [row(d), mod_spec, _full((d, d)), vec(d),
                row(W_BR), row(W_BR), pl.BlockSpec((nb, 1, 2, HEAD, W_BR), lambda i, c: (i, c, 0, 0, 0)),
                ent(HEAD), ent(HEAD), row(W_BR, 4), vec(W_BR), vec(W_BR),
                row(128), row(128), row(W_S5), vec(W_BR), _full((W_BR, W_BR)), vec(W_BR),
                row(768), row(tail, 768 // tail), vec(8), vec(8), ent(HEAD), ent(HEAD), vec(W_BR), vec(W_BR),
                row(W_GLA), ent(GLA_DK), ent(GLA_DK), _full((2, GLA_RANK, 128)), _full((2, 128)), vec(W_BR)]
    args = [h, mod3, p["w_out"].astype(BF16), p["norm_post"].reshape(1, d),
            y0, bonus, rt, rw_sf, rw_sr, f_rw, p["rw_ln_w"].reshape(1, W_BR), p["rw_ln_b"].reshape(1, W_BR),
            s5[0], s5[1], s5[2], p["s5_d"].reshape(1, W_BR), p["s5_glu_w"].astype(BF16),
            p["s5_glu_b"].reshape(1, W_BR),
            ssd[0], ssd[1], p["ssd_dt_bias"].reshape(1, 8), p["ssd_a_neg"].reshape(1, 8), ssd[2], ssd[3],
            jnp.repeat(p["ssd_d"], HEAD).reshape(1, W_BR), p["ssd_norm"].reshape(1, W_BR),
            gla[0], gla[1], gla[2], p["gla_g2"], p["gla_gb"], p["gla_norm"].reshape(1, W_BR)]
    return pl.pallas_call(
        _layer_out_kernel,
        grid=(b // nb, nc),
        in_specs=in_specs,
        out_specs=row(d),
        out_shape=jax.ShapeDtypeStruct((b, L, d), F32),
        compiler_params=_params(("parallel", "parallel")),
        name="layer_out",
    )(*args)


def _permute_w_in(w_in):
    rw = w_in[:, 0:1280]
    s5 = w_in[:, 1280:1792]
    o = 1792
    x, bm, cm = w_in[:, o:o + 256], w_in[:, o + 256:o + 384], w_in[:, o + 384:o + 512]
    dt, z = w_in[:, o + 512:o + 520], w_in[:, o + 520:o + 776]
    rep = lambda m: jnp.concatenate([m[:, 0:64], m[:, 0:64], m[:, 64:128], m[:, 64:128]], axis=1)
    pad = lambda n: jnp.zeros((w_in.shape[0], n), w_in.dtype)
    ssd = jnp.concatenate([x, rep(bm), rep(cm), z, dt, pad(120)], axis=1)
    o = 1792 + 776
    q, k, v = w_in[:, o:o + 128], w_in[:, o + 128:o + 256], w_in[:, o + 256:o + 512]
    gl, gate = w_in[:, o + 512:o + 544], w_in[:, o + 544:o + 800]
    gla = jnp.concatenate([q, k, v, gate, gl, pad(96)], axis=1)
    return jnp.concatenate([rw, s5, ssd, gla], axis=1).astype(BF16)


def _layer_params(l, a):
    p = {k: v[l] for k, v in a.items()}
    rep = lambda m: jnp.concatenate([m[..., 0:64], m[..., 0:64], m[..., 64:128], m[..., 64:128]], axis=-1)
    cw, cb = p["ssd_conv_w"], p["ssd_conv_b"]
    p["conv_w_rep"] = jnp.concatenate([cw[:, 0:256], rep(cw[:, 256:384]), rep(cw[:, 384:512])], axis=1)
    p["conv_b_rep"] = jnp.concatenate([cb[0:256], rep(cb[256:384]), rep(cb[384:512])], axis=0)
    p["ssd_a_neg"] = -jnp.exp(p["ssd_a_log"])
    p["w_in_p"] = _permute_w_in(p["w_in"])
    p["s5_ops"] = _s5_weights(p)
    return p


def _mixer_layer(h, mod3, shared_mod_row, p, is_grid, init, need_out):
    b, L, d = h.shape
    tm = min(512, L)
    mod_row = (lambda i: i) if shared_mod_row is None else (lambda i: shared_mod_row)
    f_rw, f_s5, f_ssd, f_gla = _inproj(h, mod3, mod_row, p["norm_pre"], p["w_in_p"], tm)
    if init is None:
        init = (jnp.zeros((b, 2, HEAD, W_BR), F32), jnp.zeros((b, S5_G, 256), F32),
                jnp.zeros((b, 2, HEAD, W_BR), F32), jnp.zeros((b, 2, GLA_DK, W_BR), F32))
    (rt, y0, bonus, m_rw, c_rw), (ssd_c, ssd_d, ssd_xbc), (gla_c, gla_d) = _chunk_local(f_rw, f_ssd, f_gla, p, is_grid)
    rw_sf, rw_sr, rw_fin = _scan_call(_rw_scan_kernel, "rwkv_scan", m_rw, c_rw, init[0])
    kmat, fmat, emat, lam_re, lam_im = p["s5_ops"]
    ug, z5 = _s5_local(f_s5, fmat)
    s5_xf, s5_xr, s5_fin = _s5_scan(z5, lam_re, lam_im, init[1])
    ssd_sf, ssd_sr, ssd_fin = _scan_call(_ew_scan_kernel, "ssd_scan", ssd_c, ssd_d, init[2])
    gla_sf, gla_sr, gla_fin = _scan_call(_ew_scan_kernel, "gla_scan", gla_c, gla_d, init[3])
    states = (rw_fin, s5_fin, ssd_fin, gla_fin)
    if not need_out:
        return None, states
    y5a, y5b = _s5_out(ug, s5_xf, s5_xr, kmat, emat)
    h_new = _layer_out(h, mod3, shared_mod_row,
                       (y0, bonus, rt, rw_sf, rw_sr, f_rw), (y5a, y5b, f_s5),
                       (ssd_xbc, f_ssd, ssd_sf, ssd_sr), (f_gla, gla_sf, gla_sr), p)
    return h_new, states


def kernel(x, c, ctx, c_ctx, ada_w, ada_b, norm_pre, norm_post, w_in, w_out, rw_mu, rw_w0, rw_w2, rw_a0, rw_a2, rw_kk, rw_ka, rw_rk, rw_ln_w, rw_ln_b, s5_a_re, s5_a_im, s5_log_dt, s5_b_re, s5_b_im, s5_c_re, s5_c_im, s5_d, s5_glu_w, s5_glu_b, ssd_conv_w, ssd_conv_b, ssd_dt_bias, ssd_a_log, ssd_d, ssd_norm, gla_g2, gla_gb, gla_norm):
    stacked = dict(norm_pre=norm_pre, norm_post=norm_post, w_in=w_in, w_out=w_out, rw_mu=rw_mu, rw_w0=rw_w0,
                   rw_w2=rw_w2, rw_a0=rw_a0, rw_a2=rw_a2, rw_kk=rw_kk, rw_ka=rw_ka, rw_rk=rw_rk, rw_ln_w=rw_ln_w,
                   rw_ln_b=rw_ln_b, s5_a_re=s5_a_re, s5_a_im=s5_a_im, s5_log_dt=s5_log_dt, s5_b_re=s5_b_re,
                   s5_b_im=s5_b_im, s5_c_re=s5_c_re, s5_c_im=s5_c_im, s5_d=s5_d, s5_glu_w=s5_glu_w,
                   s5_glu_b=s5_glu_b, ssd_conv_w=ssd_conv_w, ssd_conv_b=ssd_conv_b, ssd_dt_bias=ssd_dt_bias,
                   ssd_a_log=ssd_a_log, ssd_d=ssd_d, ssd_norm=ssd_norm, gla_g2=gla_g2, gla_gb=gla_gb,
                   gla_norm=gla_norm)
    depth = ada_w.shape[0]
    b, d = c.shape
    rows = -(-(b + 1) // 8) * 8
    cvec = jnp.concatenate([c, c_ctx[None, :], jnp.zeros((rows - b - 1, d), F32)], axis=0)
    mod = _modulation(cvec, ada_w, ada_b)
    h, hc = x, ctx
    for l in range(depth):
        p = _layer_params(l, stacked)
        mod3 = mod[l].reshape(rows, 1, 3 * d)
        last = l == depth - 1
        hc_next, ctx_states = _mixer_layer(hc, mod3, b, p, False, None, not last)
        h, _ = _mixer_layer(h, mod3, None, p, True, ctx_states, True)
        hc = hc_next
    return h
```

```python
import functools
import math
import operator

import jax
import jax.numpy as jnp
from jax import lax
from jax.experimental import pallas as pl
from jax.experimental.pallas import tpu as pltpu

F32 = jnp.float32
BF16 = jnp.bfloat16

EPS = 1e-6
GRID_W = 64
W_BR = 256
CH = 64
HEAD = 64
RW_GN_EPS = 64e-5
RW_LORA = 64
S5_G, S5_CH, S5_P, S5_T = 16, 16, 64, 16
SSD_CONV = 5
GLA_RANK = 16
GLA_DK = 32
GLA_TAU = 16.0
W_RW, W_S5, W_SSD, W_GLA = 1280, 512, 1152, 896
N_PROJ = W_RW + W_S5 + W_SSD + W_GLA
VMEM_LIMIT = 48 * 1024 * 1024
NB_RW = 8
NB = 8
SCAN_CB = 8
SCAN_NB = 4


class _Lk:
    def __init__(self, xs):
        self.xs = list(xs)

    @property
    def shape(self):
        return self.xs[0].shape

    def __getitem__(self, idx):
        return _Lk([x[idx] for x in self.xs])

    def astype(self, dt):
        return _Lk([x.astype(dt) for x in self.xs])


def _pick(a, i):
    if isinstance(a, _Lk):
        return a.xs[i]
    if isinstance(a, (list, tuple)):
        return [_pick(e, i) for e in a]
    return a


def _width(a):
    if isinstance(a, _Lk):
        return len(a.xs)
    if isinstance(a, (list, tuple)):
        for e in a:
            w = _width(e)
            if w:
                return w
    return 0


def _lift(f):
    def g(*args, **kw):
        n = max(_width(a) for a in args)
        if not n:
            return f(*args, **kw)
        return _Lk([f(*[_pick(a, i) for a in args], **kw) for i in range(n)])
    return g


for _name, _op in (("add", operator.add), ("sub", operator.sub), ("mul", operator.mul), ("truediv", operator.truediv)):
    setattr(_Lk, f"__{_name}__", lambda s, o, _f=_lift(_op): _f(s, o))
    setattr(_Lk, f"__r{_name}__", lambda s, o, _f=_lift(_op): _f(o, s))
_Lk.__neg__ = lambda s: _lift(operator.neg)(s)

_exp = _lift(jnp.exp)
_log = _lift(jnp.log)
_tanh = _lift(jnp.tanh)
_abs = _lift(jnp.abs)
_maximum = _lift(jnp.maximum)
_rsqrt = _lift(lax.rsqrt)
_where = _lift(jnp.where)
_cat = _lift(lambda parts, axis: jnp.concatenate(parts, axis=axis))
_roll = _lift(pltpu.roll)
_bcast = _lift(jnp.broadcast_to)
_mean_last = _lift(lambda x: jnp.mean(x, axis=-1, keepdims=True))


def _load(ref, *idx):
    return _Lk([ref[(s,) + idx] for s in range(ref.shape[0])])


def _store(ref, idx, val):
    for s in range(ref.shape[0]):
        ref[(s,) + idx] = val.xs[s]


_dg = _lift(lambda a, b, mode: lax.dot_general(
    a, b, ((({"nn": 1, "nt": 1, "tn": 0}[mode],), ({"nn": 0, "nt": 1, "tn": 0}[mode],)), ((), ())),
    preferred_element_type=F32))


def _bf(a):
    return a.astype(BF16)


def _split2(a):
    hi = _bf(a)
    return hi, _bf(a - hi.astype(F32))


def _split3(a):
    hi = _bf(a)
    r = a - hi.astype(F32)
    mid = _bf(r)
    return hi, mid, _bf(r - mid.astype(F32))


def _hilo(a):
    hi = _bf(a).astype(F32)
    return hi, a - hi


def dot1(a, b, mode="nn"):
    return _dg(_bf(a), _bf(b), mode)


def dot3(a, b, mode="nn"):
    ah, al = _split2(a)
    bh, bl = _split2(b)
    return _dg(ah, bh, mode) + (_dg(ah, bl, mode) + _dg(al, bh, mode))


def dotk3(a, b, mode="nn"):
    ah, al = _hilo(a)
    bh, bl = _hilo(b)
    ax = 0 if mode == "tn" else 1
    bx = 1 if mode == "nt" else 0
    return _dg(_bf(_cat([ah, al, ah], ax)), _bf(_cat([bh, bh, bl], bx)), mode)


def dot_xl(a_exact, b, mode="nn"):
    b1, b2, b3 = _split3(b)
    return _dg(a_exact, b1, mode) + (_dg(a_exact, b2, mode) + _dg(a_exact, b3, mode))


def dot_xl2(a_exact, b, mode="nn"):
    b1, b2 = _split2(b)
    return _dg(a_exact, b1, mode) + _dg(a_exact, b2, mode)


def dot_xr(a, b_exact, mode="nn"):
    a1, a2, a3 = _split3(a)
    return _dg(a1, b_exact, mode) + (_dg(a2, b_exact, mode) + _dg(a3, b_exact, mode))


def _sigmoid(x):
    return 1.0 / (1.0 + _exp(-x))


def _silu(x):
    return x * _sigmoid(x)


def _softplus(x):
    return _maximum(x, 0.0) + _log(1.0 + _exp(-_abs(x)))


def _iota(shape, dim):
    return lax.broadcasted_iota(jnp.int32, shape, dim)


def _tri_mask(n, rev, strict):
    r, c = _iota((n, n), 0), _iota((n, n), 1)
    if rev:
        return (r < c) if strict else (r <= c)
    return (r > c) if strict else (r >= c)


def _block_ones(n, blk):
    return (_iota((n, n), 0) // blk == _iota((n, n), 1) // blk).astype(BF16)


def _expand_blockdiag(compact, n_heads=4):
    lane_head = _iota((1, compact.shape[1]), 1) // HEAD
    return _cat([_where(lane_head == h, compact, 0.0) for h in range(n_heads)], 0)


def _compact_blockdiag(blk, rows_per_head, n_heads=4):
    lane_head = _iota((1, blk.shape[1]), 1) // HEAD
    out = _where(lane_head == 0, blk[0:rows_per_head], 0.0)
    for h in range(1, n_heads):
        out = out + _where(lane_head == h, blk[h * rows_per_head:(h + 1) * rows_per_head], 0.0)
    return out


def _expand_heads(cols, first, n_heads=4):
    lane_head = _iota((1, n_heads * HEAD), 1) // HEAD
    out = _where(lane_head == 0, cols[:, first:first + 1], 0.0)
    for h in range(1, n_heads):
        out = _where(lane_head == h, cols[:, first + h:first + h + 1], out)
    return out


def _shift_rows(x, edge_row, down):
    n = x.shape[0]
    rows = _iota((n, 1), 0)
    if down:
        return _where(rows == 0, edge_row, _roll(x, 1, 0))
    return _where(rows == n - 1, edge_row, _roll(x, n - 1, 0))


def _params(sem):
    return pltpu.CompilerParams(dimension_semantics=sem, vmem_limit_bytes=VMEM_LIMIT)


def _full(shape):
    nd = len(shape)
    return pl.BlockSpec(shape, lambda *_: (0,) * nd)


def _nb(b, want):
    return want if b % want == 0 else 1


def _mod_kernel(c_ref, w_ref, b_ref, o_ref):
    s = _silu(c_ref[...])
    o_ref[0] = dot3(s, w_ref[0]) + b_ref[0]


def _modulation(cvec, ada_w, ada_b):
    depth, d, n = ada_w.shape
    rows = cvec.shape[0]
    tn = 1024
    return pl.pallas_call(
        _mod_kernel,
        grid=(depth, n // tn),
        in_specs=[pl.BlockSpec((rows, d), lambda l, j: (0, 0)),
                  pl.BlockSpec((1, d, tn), lambda l, j: (l, 0, j)),
                  pl.BlockSpec((1, 1, tn), lambda l, j: (l, 0, j))],
        out_specs=pl.BlockSpec((1, rows, tn), lambda l, j: (l, 0, j)),
        out_shape=jax.ShapeDtypeStruct((depth, rows, n), F32),
        compiler_params=_params(("parallel", "parallel")),
        name="adaln_mod",
    )(cvec, ada_w, ada_b.reshape(depth, 1, n))


def _inproj_kernel(x_ref, mod_ref, g_ref, w_ref, o_rw, o_s5, o_ssd, o_gla):
    x = x_ref[0]
    d = x.shape[1]
    hn = x * lax.rsqrt(jnp.mean(x * x, axis=-1, keepdims=True) + EPS) * g_ref[...]
    m = mod_ref[0]
    hn = hn * (1.0 + m[:, d:2 * d]) + m[:, 0:d]
    p = dot1(hn, w_ref[...])
    o_rw[0] = p[:, 0:W_RW]
    o_s5[0] = p[:, W_RW:W_RW + W_S5]
    o_ssd[0] = p[:, W_RW + W_S5:W_RW + W_S5 + W_SSD]
    o_gla[0] = p[:, W_RW + W_S5 + W_SSD:N_PROJ]


def _inproj(h, mod3, mod_row, norm_pre, w_in_p, tm):
    b, L, d = h.shape
    widths = (W_RW, W_S5, W_SSD, W_GLA)
    return pl.pallas_call(
        _inproj_kernel,
        grid=(b, L // tm),
        in_specs=[pl.BlockSpec((1, tm, d), lambda i, j: (i, j, 0)),
                  pl.BlockSpec((1, 1, 3 * d), lambda i, j: (mod_row(i), 0, 0)),
                  _full((1, d)),
                  _full(w_in_p.shape)],
        out_specs=[pl.BlockSpec((1, tm, w), lambda i, j: (i, j, 0)) for w in widths],
        out_shape=[jax.ShapeDtypeStruct((b, L, w), F32) for w in widths],
        compiler_params=_params(("parallel", "parallel")),
        name="in_proj",
    )(h, mod3, norm_pre.reshape(1, d), w_in_p)


def _pair_blockdiag(t):
    lane_head = _iota((1, 2 * HEAD), 1) // HEAD
    return _cat([_where(lane_head == 0, t, 0.0), _where(lane_head == 1, t, 0.0)], 0)


def _pair_dot(x, y):
    return dot1(x, _pair_blockdiag(y))


def _tri_inv_stages(n_mats):
    r, c = _iota((CH, 2 * CH), 0), _iota((CH, 2 * CH), 1) % CH
    blk16 = (r // 16) == (c // 16)
    blk32 = (r // 32) == (c // 32)
    eye = jnp.where(r == c, 1.0, 0.0)
    nd = [_where(blk16, n, 0.0) for n in n_mats]
    d = [eye + x for x in nd]
    p = [_pair_dot(x, x) for x in nd]
    yield "h"
    for _ in range(2):
        d = [x + _pair_dot(x, y) for x, y in zip(d, p)]
        p = [_pair_dot(y, y) for y in p]
        yield "h"
    d = [x + _pair_dot(x, y) for x, y in zip(d, p)]
    yield "h"
    for sel in (blk32 & jnp.logical_not(blk16), jnp.logical_not(blk32)):
        od = [_pair_dot(_where(sel, n, 0.0), x) for n, x in zip(n_mats, d)]
        yield "h"
        d = [x + _pair_dot(x, y) for x, y in zip(d, od)]
        yield "h"
    return d


def _run_staggered(gens, width):
    pending, active, values = list(enumerate(gens)), [], [None] * len(gens)
    while pending or active:
        while pending and len(active) < width:
            active.append(pending.pop(0))
        for item in list(active):
            try:
                next(item[1])
            except StopIteration as stop:
                values[item[0]] = stop.value
                active.remove(item)
    return values


def _member_io(members):
    def ld(ref, *idx):
        return _Lk([ref[(s,) + idx] for s in members])

    def st(ref, idx, val):
        for j, s in enumerate(members):
            ref[(s,) + idx] = val.xs[j]
    return ld, st


def _rw_local_stages(members, is_grid, nc, cur_ref, prev_ref, next_ref, mu_ref, w0_ref, w2_ref, a0_ref, a2_ref,
                     kk_ref, ka_ref, rk_ref, rt_ref, y0_ref, bonus_ref, m_ref, c_ref):
    _load, _store = _member_io(members)

    ci = pl.program_id(1)
    has_prev = (ci > 0).astype(F32)
    has_next = (ci < nc - 1).astype(F32)
    z = _load(cur_ref, slice(None), slice(0, 1024))
    if is_grid:
        zero_row = jnp.zeros((1, W_BR), F32)
        left = _shift_rows(z[:, 0:256], zero_row, True)
        right = _shift_rows(z[:, 256:512], zero_row, False)
        up = _load(prev_ref) * has_prev
        down = _load(next_ref) * has_next
        sh = _cat([left, right, up, down], 1)
    else:
        prow = _load(prev_ref, slice(7, 8)) * has_prev
        nrow = _load(next_ref, slice(0, 1)) * has_next
        sh = _cat([_shift_rows(z[:, 0:512], prow, True), _shift_rows(z[:, 512:1024], nrow, False)], 1)
    zs = z + mu_ref[...] * (sh - z)
    yield "p"
    r, k, v = zs[:, 0:256], zs[:, 256:512], zs[:, 512:768]
    wl, al = zs[:, 768:896], zs[:, 896:1024]

    bo = _block_ones(W_BR, HEAD)
    kk = k * kk_ref[...]
    kk = kk * _rsqrt(dot1(kk * kk, bo) + EPS)
    yield "p"
    heads = lambda t: [t[:, 0:2 * HEAD], t[:, 2 * HEAD:4 * HEAD]]
    v_h = heads(v)
    row2, col2 = _iota((CH, 2 * CH), 0), _iota((CH, 2 * CH), 1) % CH
    row4, col4 = _iota((CH, 4 * CH), 0), _iota((CH, 4 * CH), 1) % CH
    eye = row2 == col2

    bonus = None
    pa_l, pam_l, prm_l, pr_l, nb_l, nk_l, eb_l, ek_l, dg_l, strict_l, incl4_l = ([] for _ in range(11))
    for d in range(2):
        rev = d == 1
        w_pre = w0_ref[d:d + 1, :] + dot1(_tanh(wl[:, RW_LORA * d:RW_LORA * (d + 1)]), w2_ref[d])
        lw = (-math.exp(-0.5)) * _sigmoid(w_pre)
        a = _sigmoid(a0_ref[d:d + 1, :] + dot1(al[:, RW_LORA * d:RW_LORA * (d + 1)], a2_ref[d]))
        k_d = k * (1.0 + (a - 1.0) * ka_ref[...])
        bterm = dot1(r * k_d * rk_ref[...], bo) * v
        bonus = bterm if bonus is None else bonus + bterm
        yield "p"

        cs = dot_xl2(_tri_mask(CH, rev, False).astype(BF16), lw)
        cs_ex = cs - lw
        cs_tot = cs[0:1] if rev else cs[CH - 1:CH]
        cs_mid = cs[CH // 2:CH // 2 + 1]
        kka = kk * a
        e_mid = _exp(cs_mid - cs)
        e_tot = _exp(cs_tot - cs)
        yield "p"
        pa_l += heads(-kk * _exp(cs_ex))
        pr_l += heads(r * _exp(cs))
        yield "p"
        pam_l += heads(-kk * _exp(cs_ex - cs_mid))
        prm_l += heads(r * _exp(cs - cs_mid))
        yield "p"
        nb_l += heads(kka * e_mid)
        nk_l += heads(k_d * e_mid)
        yield "p"
        eb_l += heads(kka * e_tot)
        ek_l += heads(k_d * e_tot)
        dg_l += [_where(eye, t, 0.0) for t in heads(_exp(cs_tot))]
        strict_l += [(row2 < col2) if rev else (row2 > col2)] * 2
        incl4_l += [(row4 <= col4) if rev else (row4 >= col4)] * 2
        yield "p"
    vv = v_h + v_h
    idx = range(4)
    bd = _pair_blockdiag

    quad = [dot1(_cat([pam_l[i], prm_l[i]], 0), _cat([bd(nb_l[i]), bd(nk_l[i])], 0), "nt") for i in idx]
    yield "h"
    w2 = 2 * HEAD
    n_l = [_where(strict_l[i], quad[i][0:CH, 0:w2], 0.0) for i in idx]
    aak = [_where(strict_l[i], quad[i][0:CH, w2:2 * w2], 0.0) for i in idx]
    g2 = [_where(incl4_l[i], quad[i][CH:2 * CH], 0.0) for i in idx]
    t_inv = yield from _tri_inv_stages(n_l)
    av = [_pair_dot(aak[i], vv[i]) for i in idx]
    yield "h"
    wu = [dot1(t_inv[i], _cat([bd(pa_l[i]), bd(av[i])], 1)) for i in idx]
    yield "h"
    zero2 = jnp.zeros((2 * CH, w2), F32)
    rhs2 = [_cat([_cat([bd(wu[i][:, 0:w2]), bd(wu[i][:, w2:2 * w2])], 1), _cat([zero2, bd(vv[i])], 1)], 0)
            for i in idx]
    gy = [dot1(g2[i], rhs2[i]) for i in idx]
    yield "h"
    zero1 = jnp.zeros((CH, w2), F32)
    same_head = (_iota((w2, 2 * w2), 0) // HEAD) == ((_iota((w2, 2 * w2), 1) % w2) // HEAD)
    lane_head = (_iota((1, 2 * w2), 1) % w2) // HEAD
    mcb = [_where(same_head, dot1(_cat([eb_l[i], ek_l[i]], 0), _cat([wu[i], _cat([zero1, vv[i]], 1)], 0), "tn"), 0.0)
           for i in idx]
    mc = [_where(lane_head == 0, x[0:HEAD], 0.0) + _where(lane_head == 1, x[HEAD:w2], 0.0) for x in mcb]
    yield "h"
    y0_parts = []
    for d in range(2):
        sl = range(2 * d, 2 * d + 2)
        _store(rt_ref, (0, d), _cat([pr_l[i] + gy[i][:, 0:w2] for i in sl], 1))
        _store(m_ref, (0, d), _cat([dg_l[i] + mc[i][:, 0:w2] for i in sl], 1))
        _store(c_ref, (0, d), _cat([mc[i][:, w2:2 * w2] for i in sl], 1))
        y0_parts.append(_cat([gy[i][:, w2:2 * w2] for i in sl], 1))
    _store(y0_ref, (), y0_parts[0] + y0_parts[1])
    _store(bonus_ref, (), bonus)


def _halo_specs(nb, is_grid, nc):
    if is_grid:
        prev = pl.BlockSpec((nb, CH, W_BR), lambda b, c: (b, jnp.maximum(c - 1, 0), 2))
        nxt = pl.BlockSpec((nb, CH, W_BR), lambda b, c: (b, jnp.minimum(c + 1, nc - 1), 3))
    else:
        rb = CH // 8
        prev = pl.BlockSpec((nb, 8, 512), lambda b, c: (b, jnp.maximum(c * rb - 1, 0), 0))
        nxt = pl.BlockSpec((nb, 8, 512), lambda b, c: (b, jnp.minimum((c + 1) * rb, nc * rb - 1), 1))
    return prev, nxt


def _scan_specs(nb, cb, nblk, rows, width):
    in_f = pl.BlockSpec((nb, cb, 1, rows, width), lambda i, j: (i, j, 0, 0, 0))
    in_r = pl.BlockSpec((nb, cb, 1, rows, width), lambda i, j: (i, nblk - 1 - j, 1, 0, 0))
    out_f = pl.BlockSpec((nb, cb, rows, width), lambda i, j: (i, j, 0, 0))
    out_r = pl.BlockSpec((nb, cb, rows, width), lambda i, j: (i, nblk - 1 - j, 0, 0))
    st = pl.BlockSpec((nb, 2, rows, width), lambda i, j: (i, 0, 0, 0))
    return in_f, in_r, out_f, out_r, st


def _rw_scan_kernel(nblk, cb, mf_ref, mr_ref, cf_ref, cr_ref, init_ref, sf_ref, sr_ref, fin_ref, s_scr):
    ci = pl.program_id(1)

    @pl.when(ci == 0)
    def _():
        s_scr[...] = init_ref[...]

    heads = lambda t: [t[:, HEAD * h:HEAD * (h + 1)] for h in range(4)]
    s = [_load(s_scr, 0), _load(s_scr, 1)]
    for j in range(cb):
        new = []
        for d, (m_ref, c_ref, o_ref) in enumerate(((mf_ref, cf_ref, sf_ref), (mr_ref, cr_ref, sr_ref))):
            jj = cb - 1 - j if d else j
            _store(o_ref, (jj,), s[d])
            m_h = heads(_load(m_ref, jj, 0))
            s_h = heads(s[d])
            prod = [dotk3(m_h[h], s_h[h]) for h in range(4)]
            new.append(_cat(prod, 1) + _load(c_ref, jj, 0))
        s = new
    _store(s_scr, (0,), s[0])
    _store(s_scr, (1,), s[1])

    @pl.when(ci == nblk - 1)
    def _():
        _store(fin_ref, (0,), s[0])
        _store(fin_ref, (1,), s[1])


def _scan_call(kernel_fn, name, a, b_arr, init):
    b, nc, _, rows, width = a.shape
    nb = _nb(b, SCAN_NB)
    cb = SCAN_CB if nc % SCAN_CB == 0 else nc
    nblk = nc // cb
    in_f, in_r, out_f, out_r, st = _scan_specs(nb, cb, nblk, rows, width)
    b_f, b_r = _scan_specs(nb, cb, nblk, b_arr.shape[3], width)[0:2]
    ent_shape = jax.ShapeDtypeStruct((b, nc, rows, width), F32)
    return pl.pallas_call(
        functools.partial(kernel_fn, nblk, cb),
        grid=(b // nb, nblk),
        in_specs=[in_f, in_r, b_f, b_r, st],
        out_specs=[out_f, out_r, st],
        out_shape=[ent_shape, ent_shape, jax.ShapeDtypeStruct((b, 2, rows, width), F32)],
        scratch_shapes=[pltpu.VMEM((nb, 2, rows, width), F32)],
        compiler_params=_params(("parallel", "arbitrary")),
        name=name,
    )(a, a, b_arr, b_arr, init)


def _rw_out_stages(members, y0_ref, bonus_ref, rt_ref, sf_ref, sr_ref, gate_ref, lnw_ref, lnb_ref):
    _load = _member_io(members)[0]
    y = _load(y0_ref)
    y = y + dot1(_load(rt_ref, 0, 0), _expand_blockdiag(_load(sf_ref, 0)))
    y = y + dot1(_load(rt_ref, 0, 1), _expand_blockdiag(_load(sr_ref, 0)))
    yield
    bo = _block_ones(W_BR, HEAD)
    y_hi, y_lo = _split2(y)
    mu = (_dg(y_hi, bo, "nn") + _dg(y_lo, bo, "nn")) * (1.0 / HEAD)
    yield
    yc = y - mu
    var = dot1(yc * yc, bo) * (1.0 / HEAD)
    yield
    yn = yc * _rsqrt(var + RW_GN_EPS) * lnw_ref[...] + lnb_ref[...]
    return (yn + _load(bonus_ref)) * _silu(_load(gate_ref))


def _ew_scan_kernel(nblk, cb, cf_ref, cr_ref, df_ref, dr_ref, init_ref, sf_ref, sr_ref, fin_ref, s_scr):
    ci = pl.program_id(1)

    @pl.when(ci == 0)
    def _():
        s_scr[...] = init_ref[...]

    s = [_load(s_scr, 0), _load(s_scr, 1)]
    for j in range(cb):
        for d, (c_ref, d_ref, o_ref) in enumerate(((cf_ref, df_ref, sf_ref), (cr_ref, dr_ref, sr_ref))):
            jj = cb - 1 - j if d else j
            _store(o_ref, (jj,), s[d])
            dec = _load(d_ref, jj, 0)
            if dec.shape[0] != s[d].shape[0]:
                dec = dec[0:1]
            s[d] = dec * s[d] + _load(c_ref, jj, 0)
    _store(s_scr, (0,), s[0])
    _store(s_scr, (1,), s[1])

    @pl.when(ci == nblk - 1)
    def _():
        _store(fin_ref, (0,), s[0])
        _store(fin_ref, (1,), s[1])


def _gla_log_decay(gl, g2_ref, gb_ref, d):
    x = dot3(gl[:, GLA_RANK * d:GLA_RANK * (d + 1)], g2_ref[d]) + gb_ref[d:d + 1, :]
    return -_softplus(-x) * (1.0 / GLA_TAU)


def _gla_local_stages(members, f_ref, g2_ref, gb_ref, cst_ref, dec_ref):
    ld, st = _member_io(members)
    k = ld(f_ref, slice(None), slice(128, 256))
    v = ld(f_ref, slice(None), slice(256, 512))
    gl = ld(f_ref, slice(None), slice(768, 800))
    sel = (_iota((128, W_BR), 0) // GLA_DK) == (_iota((128, W_BR), 1) // HEAD)
    first_row = _iota((CH, 128), 0) == 0
    ones = jnp.ones((CH, W_BR), BF16)
    las = [_gla_log_decay(gl, g2_ref, gb_ref, d) for d in range(2)]
    yield "p"
    css = [dot_xl(_tri_mask(CH, d == 1, False).astype(BF16), las[d]) for d in range(2)]
    yield "p"
    tots = [css[0][CH - 1:CH], css[1][0:1]]
    kes = [k * _exp(tots[d] - css[d]) for d in range(2)]
    yield "p"
    c_blk = [_where(sel, dot1(kes[d], v, "tn"), 0.0) for d in range(2)]
    yield "p"
    d_blk = [dot_xr(_where(first_row, _exp(tots[d]), 0.0), ones, "tn") for d in range(2)]
    yield "p"
    for d in range(2):
        st(cst_ref, (0, d), _compact_blockdiag(c_blk[d], GLA_DK))
        st(dec_ref, (0, d), _compact_blockdiag(d_blk[d], GLA_DK))


def _gla_out_stages(members, f_ref, sf_ref, sr_ref, g2_ref, gb_ref, norm_ref):
    _load = _member_io(members)[0]
    q = _load(f_ref, slice(None), slice(0, 128)) * (GLA_DK ** -0.5)
    k = _load(f_ref, slice(None), slice(128, 256))
    v = _load(f_ref, slice(None), slice(256, 512))
    gate = _load(f_ref, slice(None), slice(512, 768))
    gl = _load(f_ref, slice(None), slice(768, 800))
    khead = _iota((1, 128), 1) // GLA_DK
    vhead = _iota((1, W_BR), 1) // HEAD
    las = [_gla_log_decay(gl, g2_ref, gb_ref, d) for d in range(2)]
    yield
    css = [dot_xl(_tri_mask(CH, d == 1, False).astype(BF16), las[d]) for d in range(2)]
    yield
    mids = [css[d][CH // 2:CH // 2 + 1] for d in range(2)]
    qes = [q * _exp(css[d] - mids[d]) for d in range(2)]
    kns = [k * _exp(mids[d] - css[d]) for d in range(2)]
    yield
    sc = []
    for d in range(2):
        sc.append([_where(_tri_mask(CH, d == 1, False), dot1(_where(khead == h, qes[d], 0.0), kns[d], "nt"), 0.0)
                   for h in range(4)])
        yield
    inter = [dot1(q * _exp(css[d]), _expand_blockdiag(_load(s_ref, 0)))
             for d, s_ref in enumerate((sf_ref, sr_ref))]
    yield
    pv = [dot1(sc[0][h] + sc[1][h], _where(vhead == h, v, 0.0)) for h in range(4)]
    yield
    y = (inter[0] + inter[1]) + ((pv[0] + pv[1]) + (pv[2] + pv[3]))
    ms = dot1(y * y, _block_ones(W_BR, HEAD)) * (1.0 / HEAD)
    yield
    y = y * _rsqrt(ms + EPS) * norm_ref[...]
    return y * _silu(gate)


def _ssd_steps(dt_raw, dtb_ref, aneg_ref):
    dt = _softplus(dt_raw + dtb_ref[...])
    return dt, dt * aneg_ref[...]


def _ssd_cumsum(la):
    lane = _iota((1, 8), 1)
    la3 = _split3(la)
    tril, triu = (_tri_mask(CH, r, False).astype(BF16) for r in (False, True))
    cs_f = _dg(tril, la3[0], "nn") + (_dg(tril, la3[1], "nn") + _dg(tril, la3[2], "nn"))
    cs_r = _dg(triu, la3[0], "nn") + (_dg(triu, la3[1], "nn") + _dg(triu, la3[2], "nn"))
    cs = _where(lane < 4, cs_f, cs_r)
    return cs, _where(lane < 4, cs[CH - 1:CH], cs[0:1]), la3


def _ssd_local_stages(members, nc, cur_ref, prev_ref, next_ref, cw_ref, cb_ref, dtb_ref, aneg_ref,
                      cst_ref, dec_ref, xbc_ref):
    ld, st = _member_io(members)
    ci = pl.program_id(1)
    has_prev = (ci > 0).astype(F32)
    has_next = (ci < nc - 1).astype(F32)
    xbc = ld(cur_ref, slice(None), slice(0, 768))
    ext = _cat([ld(prev_ref) * has_prev, xbc, ld(next_ref) * has_next], 0)
    acc = cb_ref[...] + cw_ref[0:1, :] * ext[6:6 + CH]
    for j in range(1, SSD_CONV):
        acc = acc + cw_ref[j:j + 1, :] * ext[6 + j:6 + j + CH]
        yield "p"
    xbc = _silu(acc)
    st(xbc_ref, (), xbc)
    yield "p"
    x, b_rep = xbc[:, 0:256], xbc[:, 256:512]
    dt, la = _ssd_steps(ld(cur_ref, slice(None), slice(1024, 1032)), dtb_ref, aneg_ref)
    cs, cs_tot, _ = _ssd_cumsum(la)
    yield "p"
    wgt = _exp(cs_tot - cs) * dt
    sel = (_iota((W_BR, W_BR), 0) // HEAD) == (_iota((W_BR, W_BR), 1) // HEAD)
    e_tot = _exp(cs_tot)
    xs = [x * _expand_heads(wgt, 4 * d) for d in range(2)]
    yield "p"
    c_blk = [_where(sel, dot1(b_rep, xs[d], "tn"), 0.0) for d in range(2)]
    yield "p"
    for d in range(2):
        st(cst_ref, (0, d), _compact_blockdiag(c_blk[d], HEAD))
        st(dec_ref, (0, d), _bcast(_expand_heads(e_tot, 4 * d), (8, W_BR)))


def _local_kernel(is_grid, nc, *refs):
    rw_in, ssd_in, gla_in = refs[0:11], refs[11:18], refs[18:21]
    rw_out, ssd_out, gla_out = refs[21:26], refs[26:29], refs[29:31]
    nb = rw_in[0].shape[0]
    pairs = [range(s, min(s + 2, nb)) for s in range(0, nb, 2)]
    order = []
    for g in pairs:
        order += [_rw_local_stages(g, is_grid, nc, *rw_in, *rw_out), _ssd_local_stages(g, nc, *ssd_in, *ssd_out),
                  _gla_local_stages(g, *gla_in, *gla_out)]
    _run_staggered(order, 8)


def _chunk_local(f_rw, f_ssd, f_gla, p, is_grid):
    b, L, _ = f_rw.shape
    nc = L // CH
    nb = _nb(b, NB_RW)
    rb = CH // 8
    blk = lambda *shape: pl.BlockSpec((nb,) + shape, lambda i, c: (i, c) + (0,) * (len(shape) - 1))
    rw_prev, rw_next = _halo_specs(nb, is_grid, nc)
    ssd_prev = pl.BlockSpec((nb, 8, 768), lambda i, c: (i, jnp.maximum(c * rb - 1, 0), 0))
    ssd_next = pl.BlockSpec((nb, 8, 768), lambda i, c: (i, jnp.minimum((c + 1) * rb, nc * rb - 1), 0))
    st = lambda rows: (blk(1, 2, rows, W_BR), jax.ShapeDtypeStruct((b, nc, 2, rows, W_BR), F32))
    rows = lambda w: (blk(CH, w), jax.ShapeDtypeStruct((b, L, w), F32))
    outs = [st(HEAD), rows(W_BR), rows(W_BR), st(HEAD), st(HEAD),
            st(HEAD), st(8), rows(768),
            st(GLA_DK), st(GLA_DK)]
    cw = jnp.concatenate([p["conv_w_rep"], jnp.zeros((8 - SSD_CONV, 768), F32)], axis=0)
    res = pl.pallas_call(
        functools.partial(_local_kernel, is_grid, nc),
        grid=(b // nb, nc),
        in_specs=[blk(CH, W_RW), rw_prev, rw_next,
                  _full((1, 1024)), _full((2, W_BR)), _full((2, RW_LORA, W_BR)), _full((2, W_BR)),
                  _full((2, RW_LORA, W_BR)), _full((1, W_BR)), _full((1, W_BR)), _full((1, W_BR)),
                  blk(CH, W_SSD), ssd_prev, ssd_next, _full((8, 768)), _full((1, 768)), _full((1, 8)), _full((1, 8)),
                  blk(CH, W_GLA), _full((2, GLA_RANK, 128)), _full((2, 128))],
        out_specs=[o[0] for o in outs],
        out_shape=[o[1] for o in outs],
        compiler_params=_params(("parallel", "parallel")),
        name="chunk_local",
    )(f_rw, f_rw, f_rw, p["rw_mu"].reshape(1, 1024), p["rw_w0"], p["rw_w2"], p["rw_a0"], p["rw_a2"],
      p["rw_kk"].reshape(1, W_BR), p["rw_ka"].reshape(1, W_BR), p["rw_rk"].reshape(1, W_BR),
      f_ssd, f_ssd, f_ssd, cw, p["conv_b_rep"].reshape(1, 768), p["ssd_dt_bias"].reshape(1, 8),
      p["ssd_a_neg"].reshape(1, 8), f_gla, p["gla_g2"], p["gla_gb"])
    return res[0:5], res[5:8], res[8:10]


def _ssd_out_stages(members, xbc_ref, tail_ref, dtb_ref, aneg_ref, sf_ref, sr_ref, dskip_ref, norm_ref):
    _load = _member_io(members)[0]
    x = _load(xbc_ref, slice(None), slice(0, 256))
    b_rep = _load(xbc_ref, slice(None), slice(256, 512))
    c_rep = _load(xbc_ref, slice(None), slice(512, 768))
    z = _load(tail_ref, slice(None), slice(0, 256))
    dt, la = _ssd_steps(_load(tail_ref, slice(None), slice(256, 264)), dtb_ref, aneg_ref)
    cs, _, la3 = _ssd_cumsum(la)
    yield
    lane_head = _iota((1, W_BR), 1) // HEAD
    tril, triu = (_tri_mask(CH, r, False).astype(BF16) for r in (False, True))
    cst_f = _dg(la3[0], triu, "tn") + (_dg(la3[1], triu, "tn") + _dg(la3[2], triu, "tn"))
    cst_r = _dg(la3[0], tril, "tn") + (_dg(la3[1], tril, "tn") + _dg(la3[2], tril, "tn"))
    cst = _where(_iota((8, 1), 0) < 4, cst_f, cst_r)
    yield
    e_cs = _exp(cs)
    g = [dot1(c_rep[:, 2 * HEAD * grp:2 * HEAD * grp + HEAD], b_rep[:, 2 * HEAD * grp:2 * HEAD * grp + HEAD], "nt")
         for grp in range(2)]
    yield
    inter = [dot1(c_rep * _expand_heads(e_cs, 4 * d), _expand_blockdiag(_load(s_ref, 0)))
             for d, s_ref in enumerate((sf_ref, sr_ref))]
    yield
    xs = [x * _expand_heads(dt, 4 * d) for d in range(2)]
    intra = []
    for d in range(2):
        incl = _tri_mask(CH, d == 1, False)
        for h in range(4):
            col = 4 * d + h
            seg = _exp(_where(incl, cs[:, col:col + 1] - cst[col:col + 1, :], -1e30))
            intra.append(dot1(g[h // 2] * seg, _where(lane_head == h, xs[d], 0.0)))
            if h % 2:
                yield
    y = dskip_ref[...] * x + (inter[0] + inter[1])
    y = y + (((intra[0] + intra[1]) + (intra[2] + intra[3])) + ((intra[4] + intra[5]) + (intra[6] + intra[7])))
    y = y * _silu(z)
    return y * _rsqrt(_mean_last(y * y) + EPS) * norm_ref[...]


def _s5_weights(p):
    t = S5_T
    steps = jnp.arange(t + 1, dtype=F32)
    kmats, fmats, emats, lam_re, lam_im = [], [], [], [], []
    tt = jnp.arange(t)
    tau = tt[None, :] - tt[:, None]
    cmul = lambda x, y: (x[0] * y[0] - x[1] * y[1], x[0] * y[1] + x[1] * y[0])
    for d in range(2):
        a_re, a_im = p["s5_a_re"][d], p["s5_a_im"][d]
        dt = jnp.exp(p["s5_log_dt"][d])[:, None]
        mag = jnp.exp((a_re * dt)[None] * steps[:, None, None])
        ang = (a_im * dt)[None] * steps[:, None, None]
        pw = (mag * jnp.cos(ang), mag * jnp.sin(ang))
        num = (pw[0][1] - 1.0, pw[1][1])
        den = a_re * a_re + a_im * a_im
        quo = ((num[0] * a_re + num[1] * a_im) / den, (num[1] * a_re - num[0] * a_im) / den)
        b_bar = cmul((quo[0][..., None], quo[1][..., None]), (p["s5_b_re"][d], p["s5_b_im"][d]))
        c_c = (p["s5_c_re"][d], p["s5_c_im"][d])
        qb = cmul((pw[0][:t, :, :, None], pw[1][:t, :, :, None]), (b_bar[0][None], b_bar[1][None]))
        hp = lax.Precision.HIGHEST
        kd = (jnp.einsum("gcp,jgpe->gcje", c_c[0], qb[0], precision=hp)
              - jnp.einsum("gcp,jgpe->gcje", c_c[1], qb[1], precision=hp))
        lag = -tau if d else tau
        sel = (lag[:, :, None] == jnp.arange(t)[None, None, :]).astype(F32)
        kmats.append(jnp.einsum("stj,gcje->gsetc", sel, kd, precision=hp).reshape(S5_G, t * S5_CH, t * S5_CH))
        fsel = (lambda x: x) if d else (lambda x: x[::-1])
        fmats += [jnp.transpose(fsel(x), (1, 0, 3, 2)).reshape(S5_G, t * S5_CH, S5_P) for x in qb]
        esel = (lambda x: x[1:][::-1]) if d else (lambda x: x[1:])
        ec = cmul((c_c[0][None], c_c[1][None]), (esel(pw[0])[:, :, None, :], esel(pw[1])[:, :, None, :]))
        ec = [jnp.transpose(x, (1, 3, 0, 2)).reshape(S5_G, S5_P, t * S5_CH) for x in ec]
        emats += [ec[0], -ec[1]]
        lam_re += [pw[0][t], pw[0][t]]
        lam_im += [-pw[1][t], pw[1][t]]
    kmat = kmats[0] + kmats[1]
    fmat = jnp.concatenate(fmats, axis=2)
    emat = jnp.concatenate(emats, axis=1)
    return (kmat.astype(BF16), fmat.astype(BF16), emat.astype(BF16),
            jnp.concatenate(lam_re, axis=1), jnp.concatenate(lam_im, axis=1))


def _s5_local_kernel(ua_ref, ub_ref, f_ref, ug_ref, z_ref):
    rb = z_ref.shape[2]
    step_rows = lambda s: jnp.concatenate([ua_ref[0, pl.ds(s, rb, stride=S5_T), :],
                                           ub_ref[0, pl.ds(s, rb, stride=S5_T), :]], axis=1)
    ug = _block_transpose16([step_rows(s) for s in range(S5_T)])
    ub = [_bf(x) for x in ug]
    zs = [_dg(ub[g], f_ref[g], "nn") for g in range(S5_G)]
    for g in range(S5_G):
        ug_ref[0, g] = ub[g]
        z_ref[0, g] = zs[g]


def _block_transpose16(arrs):
    lane_blk = _iota((1, 256), 1) // 16
    arrs = list(arrs)
    for k in range(4):
        b = 1 << k
        bit = (lane_blk & b) != 0
        new = list(arrs)
        for i in range(16):
            if not i & b:
                lo, hi = arrs[i], arrs[i | b]
                new[i] = jnp.where(bit, pltpu.roll(hi, 16 * b, 1), lo)
                new[i | b] = jnp.where(bit, hi, pltpu.roll(lo, 256 - 16 * b, 1))
        arrs = new
    return arrs


def _s5_rows(nc):
    return 64 if nc % 64 == 0 else nc


def _s5_local(f_s5, fmat):
    b, L, _ = f_s5.shape
    nc = L // S5_T
    rb = _s5_rows(nc)
    grp = pl.BlockSpec((1, S5_G, rb, 256), lambda i, j: (i, 0, j, 0))
    return pl.pallas_call(
        _s5_local_kernel,
        grid=(b, nc // rb),
        in_specs=[pl.BlockSpec((1, rb * S5_T, 128), lambda i, j: (i, j, 0)),
                  pl.BlockSpec((1, rb * S5_T, 128), lambda i, j: (i, j, 1)), _full(fmat.shape)],
        out_specs=[grp, grp],
        out_shape=[jax.ShapeDtypeStruct((b, S5_G, nc, 256), BF16), jax.ShapeDtypeStruct((b, S5_G, nc, 256), F32)],
        compiler_params=_params(("parallel", "parallel")),
        name="s5_local",
    )(f_s5, f_s5, fmat)


def _s5_scan_kernel(nblk, cb, zf_ref, zr_ref, lre_ref, lim_ref, init_ref, xf_ref, xr_ref, fin_ref, x_scr):
    ci = pl.program_id(0)

    @pl.when(ci == 0)
    def _():
        x_scr[...] = init_ref[...]

    lane = _iota((1, 256), 1)
    is_fwd = lane < 128
    even_q = (lane // S5_P) % 2 == 0
    lre, lim = lre_ref[...], lim_ref[...]
    x = _load(x_scr)
    for i in range(cb):
        for s in range(len(x.xs)):
            xf_ref[s, :, i, :] = x.xs[s]
            xr_ref[s, :, cb - 1 - i, :] = x.xs[s]
        z = _Lk([jnp.where(is_fwd, zf_ref[s, :, i, :], zr_ref[s, :, cb - 1 - i, :]) for s in range(len(x.xs))])
        swapped = _where(even_q, _roll(x, 192, 1), _roll(x, 64, 1))
        x = lre * x + lim * swapped + z
    _store(x_scr, (), x)

    @pl.when(ci == nblk - 1)
    def _():
        _store(fin_ref, (), x)


def _s5_scan(z, lam_re, lam_im, init):
    b, g, nc, w = z.shape
    cb = 8
    nblk = nc // cb
    fwd = pl.BlockSpec((b, g, cb, w), lambda j: (0, 0, j, 0))
    rev = pl.BlockSpec((b, g, cb, w), lambda j: (0, 0, nblk - 1 - j, 0))
    ent = jax.ShapeDtypeStruct((b, g, nc, w), F32)
    return pl.pallas_call(
        functools.partial(_s5_scan_kernel, nblk, cb),
        grid=(nblk,),
        in_specs=[fwd, rev, _full((g, w)), _full((g, w)), _full((b, g, w))],
        out_specs=[fwd, rev, _full((b, g, w))],
        out_shape=[ent, ent, jax.ShapeDtypeStruct((b, g, w), F32)],
        scratch_shapes=[pltpu.VMEM((b, g, w), F32)],
        compiler_params=_params(("arbitrary",)),
        name="s5_scan",
    )(z, z, lam_re, lam_im, init)


def _s5_out_kernel(ug_ref, xf_ref, xr_ref, k_ref, e_ref, ya_ref, yb_ref):
    rb = ug_ref.shape[2]
    lane = _iota((1, 256), 1)
    groups = range(S5_G)
    x_ent = [_split2(jnp.where(lane < 128, xf_ref[0, g], xr_ref[0, g])) for g in groups]
    ys = [_dg(ug_ref[0, g], k_ref[g], "nn") + (_dg(x_ent[g][0], e_ref[g], "nn") + _dg(x_ent[g][1], e_ref[g], "nn"))
          for g in groups]
    yt = _block_transpose16(ys)
    for s in range(S5_T):
        ya_ref[0, pl.ds(s, rb, stride=S5_T), :] = yt[s][:, 0:128]
        yb_ref[0, pl.ds(s, rb, stride=S5_T), :] = yt[s][:, 128:256]


def _s5_out(ug, xf, xr, kmat, emat):
    b, g, nc, w = ug.shape
    rb = _s5_rows(nc)
    grp = pl.BlockSpec((1, g, rb, w), lambda i, j: (i, 0, j, 0))
    half = pl.BlockSpec((1, rb * S5_T, 128), lambda i, j: (i, j, 0))
    half_shape = jax.ShapeDtypeStruct((b, nc * S5_T, 128), F32)
    return pl.pallas_call(
        _s5_out_kernel,
        grid=(b, nc // rb),
        in_specs=[grp, grp, grp, _full(kmat.shape), _full(emat.shape)],
        out_specs=[half, half],
        out_shape=[half_shape, half_shape],
        compiler_params=_params(("parallel", "parallel")),
        name="s5_out",
    )(ug, xf, xr, kmat, emat)


def _s5_out_stages(members, ys5a_ref, ys5b_ref, fs5_ref, s5d_ref, gluw_ref, glub_ref):
    _load = _member_io(members)[0]
    u = _load(fs5_ref, slice(None), slice(0, 256))
    y = _cat([_load(ys5a_ref), _load(ys5b_ref)], 1) + s5d_ref[...] * u
    y = 0.5 * y * (1.0 + _tanh(math.sqrt(2.0 / math.pi) * (y + 0.044715 * (y * y * y))))
    yield
    y = y * _sigmoid(dot1(y, gluw_ref[...]) + glub_ref[...])
    yield
    return y * _silu(_load(fs5_ref, slice(None), slice(256, 512)))


def _layer_out_kernel(h_ref, mod_ref, w_ref, g_ref, *refs):
    rw_refs, s5_refs, ssd_refs, gla_refs, o_ref = refs[0:8], refs[8:14], refs[14:22], refs[22:28], refs[28]
    nb, rows, d = h_ref.shape
    pairs = [range(s, min(s + 2, nb)) for s in range(0, nb, 2)]
    gens = []
    for g in pairs:
        gens += [_rw_out_stages(g, *rw_refs), _s5_out_stages(g, *s5_refs), _ssd_out_stages(g, *ssd_refs),
                 _gla_out_stages(g, *gla_refs)]
    vals = _run_staggered(gens, 8)
    tiles = []
    for k in range(len(pairs)):
        ycat = _cat(vals[4 * k:4 * k + 4], 1)
        tiles += ycat.xs
    o = dot1(jnp.concatenate(tiles, axis=0), w_ref[...])
    o = o * lax.rsqrt(jnp.mean(o * o, axis=-1, keepdims=True) + EPS) * g_ref[...]
    for s in range(nb):
        gate = mod_ref[min(s, mod_ref.shape[0] - 1)][:, 2 * d:3 * d]
        o_ref[s] = h_ref[s] + gate * o[s * rows:(s + 1) * rows]


def _layer_out(h, mod3, shared_mod_row, rw, s5, ssd, gla, p):
    b, L, d = h.shape
    nc = L // CH
    nb = _nb(b, NB)
    row = lambda w, col=0: pl.BlockSpec((nb, CH, w), lambda i, c: (i, c, col))
    ent = lambda r: pl.BlockSpec((nb, 1, r, W_BR), lambda i, c: (i, c, 0, 0))
    vec = lambda w: _full((1, w))
    if shared_mod_row is None:
        mod_spec = pl.BlockSpec((nb, 1, 3 * d), lambda i, c: (i, 0, 0))
    else:
        mod_spec = pl.BlockSpec((1, 1, 3 * d), lambda i, c: (shared_mod_row, 0, 0))
    tail = W_SSD - 768
    y0, bonus, rt, rw_sf, rw_sr, f_rw = rw
    in_specs = [row(d), mod_spec, _full((d, d)), vec(d),
                row(W_BR), row(W_BR), pl.BlockSpec((nb, 1, 2, HEAD, W_BR), lambda i, c: (i, c, 0, 0, 0)),
                ent(HEAD), ent(HEAD), row(W_BR, 4), vec(W_BR), vec(W_BR),
                row(128), row(128), row(W_S5), vec(W_BR), _full((W_BR, W_BR)), vec(W_BR),
                row(768), row(tail, 768 // tail), vec(8), vec(8), ent(HEAD), ent(HEAD), vec(W_BR), vec(W_BR),
                row(W_GLA), ent(GLA_DK), ent(GLA_DK), _full((2, GLA_RANK, 128)), _full((2, 128)), vec(W_BR)]
    args = [h, mod3, p["w_out"].astype(BF16), p["norm_post"].reshape(1, d),
            y0, bonus, rt, rw_sf, rw_sr, f_rw, p["rw_ln_w"].reshape(1, W_BR), p["rw_ln_b"].reshape(1, W_BR),
            s5[0], s5[1], s5[2], p["s5_d"].reshape(1, W_BR), p["s5_glu_w"].astype(BF16),
            p["s5_glu_b"].reshape(1, W_BR),
            ssd[0], ssd[1], p["ssd_dt_bias"].reshape(1, 8), p["ssd_a_neg"].reshape(1, 8), ssd[2], ssd[3],
            jnp.repeat(p["ssd_d"], HEAD).reshape(1, W_BR), p["ssd_norm"].reshape(1, W_BR),
            gla[0], gla[1], gla[2], p["gla_g2"], p["gla_gb"], p["gla_norm"].reshape(1, W_BR)]
    return pl.pallas_call(
        _layer_out_kernel,
        grid=(b // nb, nc),
        in_specs=in_specs,
        out_specs=row(d),
        out_shape=jax.ShapeDtypeStruct((b, L, d), F32),
        compiler_params=_params(("parallel", "parallel")),
        name="layer_out",
    )(*args)


def _permute_w_in(w_in):
    rw = w_in[:, 0:1280]
    s5 = w_in[:, 1280:1792]
    o = 1792
    x, bm, cm = w_in[:, o:o + 256], w_in[:, o + 256:o + 384], w_in[:, o + 384:o + 512]
    dt, z = w_in[:, o + 512:o + 520], w_in[:, o + 520:o + 776]
    rep = lambda m: jnp.concatenate([m[:, 0:64], m[:, 0:64], m[:, 64:128], m[:, 64:128]], axis=1)
    pad = lambda n: jnp.zeros((w_in.shape[0], n), w_in.dtype)
    ssd = jnp.concatenate([x, rep(bm), rep(cm), z, dt, pad(120)], axis=1)
    o = 1792 + 776
    q, k, v = w_in[:, o:o + 128], w_in[:, o + 128:o + 256], w_in[:, o + 256:o + 512]
    gl, gate = w_in[:, o + 512:o + 544], w_in[:, o + 544:o + 800]
    gla = jnp.concatenate([q, k, v, gate, gl, pad(96)], axis=1)
    return jnp.concatenate([rw, s5, ssd, gla], axis=1).astype(BF16)


def _layer_params(l, a):
    p = {k: v[l] for k, v in a.items()}
    rep = lambda m: jnp.concatenate([m[..., 0:64], m[..., 0:64], m[..., 64:128], m[..., 64:128]], axis=-1)
    cw, cb = p["ssd_conv_w"], p["ssd_conv_b"]
    p["conv_w_rep"] = jnp.concatenate([cw[:, 0:256], rep(cw[:, 256:384]), rep(cw[:, 384:512])], axis=1)
    p["conv_b_rep"] = jnp.concatenate([cb[0:256], rep(cb[256:384]), rep(cb[384:512])], axis=0)
    p["ssd_a_neg"] = -jnp.exp(p["ssd_a_log"])
    p["w_in_p"] = _permute_w_in(p["w_in"])
    p["s5_ops"] = _s5_weights(p)
    return p


def _mixer_layer(h, mod3, shared_mod_row, p, is_grid, init, need_out):
    b, L, d = h.shape
    tm = min(512, L)
    mod_row = (lambda i: i) if shared_mod_row is None else (lambda i: shared_mod_row)
    f_rw, f_s5, f_ssd, f_gla = _inproj(h, mod3, mod_row, p["norm_pre"], p["w_in_p"], tm)
    if init is None:
        init = (jnp.zeros((b, 2, HEAD, W_BR), F32), jnp.zeros((b, S5_G, 256), F32),
                jnp.zeros((b, 2, HEAD, W_BR), F32), jnp.zeros((b, 2, GLA_DK, W_BR), F32))
    (rt, y0, bonus, m_rw, c_rw), (ssd_c, ssd_d, ssd_xbc), (gla_c, gla_d) = _chunk_local(f_rw, f_ssd, f_gla, p, is_grid)
    rw_sf, rw_sr, rw_fin = _scan_call(_rw_scan_kernel, "rwkv_scan", m_rw, c_rw, init[0])
    kmat, fmat, emat, lam_re, lam_im = p["s5_ops"]
    ug, z5 = _s5_local(f_s5, fmat)
    s5_xf, s5_xr, s5_fin = _s5_scan(z5, lam_re, lam_im, init[1])
    ssd_sf, ssd_sr, ssd_fin = _scan_call(_ew_scan_kernel, "ssd_scan", ssd_c, ssd_d, init[2])
    gla_sf, gla_sr, gla_fin = _scan_call(_ew_scan_kernel, "gla_scan", gla_c, gla_d, init[3])
    states = (rw_fin, s5_fin, ssd_fin, gla_fin)
    if not need_out:
        return None, states
    y5a, y5b = _s5_out(ug, s5_xf, s5_xr, kmat, emat)
    h_new = _layer_out(h, mod3, shared_mod_row,
                       (y0, bonus, rt, rw_sf, rw_sr, f_rw), (y5a, y5b, f_s5),
                       (ssd_xbc, f_ssd, ssd_sf, ssd_sr), (f_gla, gla_sf, gla_sr), p)
    return h_new, states


def kernel(x, c, ctx, c_ctx, ada_w, ada_b, norm_pre, norm_post, w_in, w_out, rw_mu, rw_w0, rw_w2, rw_a0, rw_a2, rw_kk, rw_ka, rw_rk, rw_ln_w, rw_ln_b, s5_a_re, s5_a_im, s5_log_dt, s5_b_re, s5_b_im, s5_c_re, s5_c_im, s5_d, s5_glu_w, s5_glu_b, ssd_conv_w, ssd_conv_b, ssd_dt_bias, ssd_a_log, ssd_d, ssd_norm, gla_g2, gla_gb, gla_norm):
    stacked = dict(norm_pre=norm_pre, norm_post=norm_post, w_in=w_in, w_out=w_out, rw_mu=rw_mu, rw_w0=rw_w0,
                   rw_w2=rw_w2, rw_a0=rw_a0, rw_a2=rw_a2, rw_kk=rw_kk, rw_ka=rw_ka, rw_rk=rw_rk, rw_ln_w=rw_ln_w,
                   rw_ln_b=rw_ln_b, s5_a_re=s5_a_re, s5_a_im=s5_a_im, s5_log_dt=s5_log_dt, s5_b_re=s5_b_re,
                   s5_b_im=s5_b_im, s5_c_re=s5_c_re, s5_c_im=s5_c_im, s5_d=s5_d, s5_glu_w=s5_glu_w,
                   s5_glu_b=s5_glu_b, ssd_conv_w=ssd_conv_w, ssd_conv_b=ssd_conv_b, ssd_dt_bias=ssd_dt_bias,
                   ssd_a_log=ssd_a_log, ssd_d=ssd_d, ssd_norm=ssd_norm, gla_g2=gla_g2, gla_gb=gla_gb,
                   gla_norm=gla_norm)
    depth = ada_w.shape[0]
    b, d = c.shape
    rows = -(-(b + 1) // 8) * 8
    cvec = jnp.concatenate([c, c_ctx[None, :], jnp.zeros((rows - b - 1, d), F32)], axis=0)
    mod = _modulation(cvec, ada_w, ada_b)
    h, hc = x, ctx
    for l in range(depth):
        p = _layer_params(l, stacked)
        mod3 = mod[l].reshape(rows, 1, 3 * d)
        last = l == depth - 1
        hc_next, ctx_states = _mixer_layer(hc, mod3, b, p, False, None, not last)
        h, _ = _mixer_layer(h, mod3, None, p, True, ctx_states, True)
        hc = hc_next
    return h
```

```python
import functools
import math
import operator

import jax
import jax.numpy as jnp
from jax import lax
from jax.experimental import pallas as pl
from jax.experimental.pallas import tpu as pltpu

F32 = jnp.float32
BF16 = jnp.bfloat16

EPS = 1e-6
GRID_W = 64
W_BR = 256
CH = 64
HEAD = 64
RW_GN_EPS = 64e-5
RW_LORA = 64
S5_G, S5_CH, S5_P, S5_T = 16, 16, 64, 16
SSD_CONV = 5
GLA_RANK = 16
GLA_DK = 32
GLA_TAU = 16.0
W_RW, W_S5, W_SSD, W_GLA = 1280, 512, 1152, 896
N_PROJ = W_RW + W_S5 + W_SSD + W_GLA
VMEM_LIMIT = 48 * 1024 * 1024
NB_RW = 8
NB = 8
SCAN_CB = 8
SCAN_NB = 4


class _Lk:
    def __init__(self, xs):
        self.xs = list(xs)

    @property
    def shape(self):
        return self.xs[0].shape

    def __getitem__(self, idx):
        return _Lk([x[idx] for x in self.xs])

    def astype(self, dt):
        return _Lk([x.astype(dt) for x in self.xs])


def _pick(a, i):
    if isinstance(a, _Lk):
        return a.xs[i]
    if isinstance(a, (list, tuple)):
        return [_pick(e, i) for e in a]
    return a


def _width(a):
    if isinstance(a, _Lk):
        return len(a.xs)
    if isinstance(a, (list, tuple)):
        for e in a:
            w = _width(e)
            if w:
                return w
    return 0


def _lift(f):
    def g(*args, **kw):
        n = max(_width(a) for a in args)
        if not n:
            return f(*args, **kw)
        return _Lk([f(*[_pick(a, i) for a in args], **kw) for i in range(n)])
    return g


for _name, _op in (("add", operator.add), ("sub", operator.sub), ("mul", operator.mul), ("truediv", operator.truediv)):
    setattr(_Lk, f"__{_name}__", lambda s, o, _f=_lift(_op): _f(s, o))
    setattr(_Lk, f"__r{_name}__", lambda s, o, _f=_lift(_op): _f(o, s))
_Lk.__neg__ = lambda s: _lift(operator.neg)(s)

_exp = _lift(jnp.exp)
_log = _lift(jnp.log)
_tanh = _lift(jnp.tanh)
_abs = _lift(jnp.abs)
_maximum = _lift(jnp.maximum)
_rsqrt = _lift(lax.rsqrt)
_where = _lift(jnp.where)
_cat = _lift(lambda parts, axis: jnp.concatenate(parts, axis=axis))
_roll = _lift(pltpu.roll)
_bcast = _lift(jnp.broadcast_to)
_mean_last = _lift(lambda x: jnp.mean(x, axis=-1, keepdims=True))


def _load(ref, *idx):
    return _Lk([ref[(s,) + idx] for s in range(ref.shape[0])])


def _store(ref, idx, val):
    for s in range(ref.shape[0]):
        ref[(s,) + idx] = val.xs[s]


_dg = _lift(lambda a, b, mode: lax.dot_general(
    a, b, ((({"nn": 1, "nt": 1, "tn": 0}[mode],), ({"nn": 0, "nt": 1, "tn": 0}[mode],)), ((), ())),
    preferred_element_type=F32))


def _bf(a):
    return a.astype(BF16)


def _split2(a):
    hi = _bf(a)
    return hi, _bf(a - hi.astype(F32))


def _split3(a):
    hi = _bf(a)
    r = a - hi.astype(F32)
    mid = _bf(r)
    return hi, mid, _bf(r - mid.astype(F32))


def _hilo(a):
    hi = _bf(a).astype(F32)
    return hi, a - hi


def dot1(a, b, mode="nn"):
    return _dg(_bf(a), _bf(b), mode)


def dot3(a, b, mode="nn"):
    ah, al = _split2(a)
    bh, bl = _split2(b)
    return _dg(ah, bh, mode) + (_dg(ah, bl, mode) + _dg(al, bh, mode))


def dotk3(a, b, mode="nn"):
    ah, al = _hilo(a)
    bh, bl = _hilo(b)
    ax = 0 if mode == "tn" else 1
    bx = 1 if mode == "nt" else 0
    return _dg(_bf(_cat([ah, al, ah], ax)), _bf(_cat([bh, bh, bl], bx)), mode)


def dot_xl(a_exact, b, mode="nn"):
    b1, b2, b3 = _split3(b)
    return _dg(a_exact, b1, mode) + (_dg(a_exact, b2, mode) + _dg(a_exact, b3, mode))


def dot_xl2(a_exact, b, mode="nn"):
    b1, b2 = _split2(b)
    return _dg(a_exact, b1, mode) + _dg(a_exact, b2, mode)


def dot_xr(a, b_exact, mode="nn"):
    a1, a2, a3 = _split3(a)
    return _dg(a1, b_exact, mode) + (_dg(a2, b_exact, mode) + _dg(a3, b_exact, mode))


def _sigmoid(x):
    return 1.0 / (1.0 + _exp(-x))


def _silu(x):
    return x * _sigmoid(x)


def _softplus(x):
    return _maximum(x, 0.0) + _log(1.0 + _exp(-_abs(x)))


def _iota(shape, dim):
    return lax.broadcasted_iota(jnp.int32, shape, dim)


def _tri_mask(n, rev, strict):
    r, c = _iota((n, n), 0), _iota((n, n), 1)
    if rev:
        return (r < c) if strict else (r <= c)
    return (r > c) if strict else (r >= c)


def _block_ones(n, blk):
    return (_iota((n, n), 0) // blk == _iota((n, n), 1) // blk).astype(BF16)


def _expand_blockdiag(compact, n_heads=4):
    lane_head = _iota((1, compact.shape[1]), 1) // HEAD
    return _cat([_where(lane_head == h, compact, 0.0) for h in range(n_heads)], 0)


def _compact_blockdiag(blk, rows_per_head, n_heads=4):
    lane_head = _iota((1, blk.shape[1]), 1) // HEAD
    out = _where(lane_head == 0, blk[0:rows_per_head], 0.0)
    for h in range(1, n_heads):
        out = out + _where(lane_head == h, blk[h * rows_per_head:(h + 1) * rows_per_head], 0.0)
    return out


def _expand_heads(cols, first, n_heads=4):
    lane_head = _iota((1, n_heads * HEAD), 1) // HEAD
    out = _where(lane_head == 0, cols[:, first:first + 1], 0.0)
    for h in range(1, n_heads):
        out = _where(lane_head == h, cols[:, first + h:first + h + 1], out)
    return out


def _shift_rows(x, edge_row, down):
    n = x.shape[0]
    rows = _iota((n, 1), 0)
    if down:
        return _where(rows == 0, edge_row, _roll(x, 1, 0))
    return _where(rows == n - 1, edge_row, _roll(x, n - 1, 0))


def _params(sem):
    return pltpu.CompilerParams(dimension_semantics=sem, vmem_limit_bytes=VMEM_LIMIT)


def _full(shape):
    nd = len(shape)
    return pl.BlockSpec(shape, lambda *_: (0,) * nd)


def _nb(b, want):
    return want if b % want == 0 else 1


def _mod_kernel(c_ref, w_ref, b_ref, o_ref):
    s = _silu(c_ref[...])
    o_ref[0] = dot3(s, w_ref[0]) + b_ref[0]


def _modulation(cvec, ada_w, ada_b):
    depth, d, n = ada_w.shape
    rows = cvec.shape[0]
    tn = 1024
    return pl.pallas_call(
        _mod_kernel,
        grid=(depth, n // tn),
        in_specs=[pl.BlockSpec((rows, d), lambda l, j: (0, 0)),
                  pl.BlockSpec((1, d, tn), lambda l, j: (l, 0, j)),
                  pl.BlockSpec((1, 1, tn), lambda l, j: (l, 0, j))],
        out_specs=pl.BlockSpec((1, rows, tn), lambda l, j: (l, 0, j)),
        out_shape=jax.ShapeDtypeStruct((depth, rows, n), F32),
        compiler_params=_params(("parallel", "parallel")),
        name="adaln_mod",
    )(cvec, ada_w, ada_b.reshape(depth, 1, n))


def _inproj_kernel(x_ref, mod_ref, g_ref, w_ref, o_rw, o_s5, o_ssd, o_gla):
    x = x_ref[0]
    d = x.shape[1]
    hn = x * lax.rsqrt(jnp.mean(x * x, axis=-1, keepdims=True) + EPS) * g_ref[...]
    m = mod_ref[0]
    hn = hn * (1.0 + m[:, d:2 * d]) + m[:, 0:d]
    p = dot1(hn, w_ref[...])
    o_rw[0] = p[:, 0:W_RW]
    o_s5[0] = p[:, W_RW:W_RW + W_S5]
    o_ssd[0] = p[:, W_RW + W_S5:W_RW + W_S5 + W_SSD]
    o_gla[0] = p[:, W_RW + W_S5 + W_SSD:N_PROJ]


def _inproj(h, mod3, mod_row, norm_pre, w_in_p, tm):
    b, L, d = h.shape
    widths = (W_RW, W_S5, W_SSD, W_GLA)
    return pl.pallas_call(
        _inproj_kernel,
        grid=(b, L // tm),
        in_specs=[pl.BlockSpec((1, tm, d), lambda i, j: (i, j, 0)),
                  pl.BlockSpec((1, 1, 3 * d), lambda i, j: (mod_row(i), 0, 0)),
                  _full((1, d)),
                  _full(w_in_p.shape)],
        out_specs=[pl.BlockSpec((1, tm, w), lambda i, j: (i, j, 0)) for w in widths],
        out_shape=[jax.ShapeDtypeStruct((b, L, w), F32) for w in widths],
        compiler_params=_params(("parallel", "parallel")),
        name="in_proj",
    )(h, mod3, norm_pre.reshape(1, d), w_in_p)


def _pair_blockdiag(t):
    lane_head = _iota((1, 2 * HEAD), 1) // HEAD
    return _cat([_where(lane_head == 0, t, 0.0), _where(lane_head == 1, t, 0.0)], 0)


def _pair_dot(x, y):
    return dot1(x, _pair_blockdiag(y))


def _tri_inv_stages(n_mats):
    r, c = _iota((CH, 2 * CH), 0), _iota((CH, 2 * CH), 1) % CH
    blk16 = (r // 16) == (c // 16)
    blk32 = (r // 32) == (c // 32)
    eye = jnp.where(r == c, 1.0, 0.0)
    nd = [_where(blk16, n, 0.0) for n in n_mats]
    d = [eye + x for x in nd]
    p = [_pair_dot(x, x) for x in nd]
    yield "h"
    for _ in range(2):
        d = [x + _pair_dot(x, y) for x, y in zip(d, p)]
        p = [_pair_dot(y, y) for y in p]
        yield "h"
    d = [x + _pair_dot(x, y) for x, y in zip(d, p)]
    yield "h"
    for sel in (blk32 & jnp.logical_not(blk16), jnp.logical_not(blk32)):
        od = [_pair_dot(_where(sel, n, 0.0), x) for n, x in zip(n_mats, d)]
        yield "h"
        d = [x + _pair_dot(x, y) for x, y in zip(d, od)]
        yield "h"
    return d


def _run_staggered(gens, width):
    pending, active, values = list(enumerate(gens)), [], [None] * len(gens)
    while pending or active:
        while pending and len(active) < width:
            active.append(pending.pop(0))
        for item in list(active):
            try:
                next(item[1])
            except StopIteration as stop:
                values[item[0]] = stop.value
                active.remove(item)
    return values


def _member_io(members):
    def ld(ref, *idx):
        return _Lk([ref[(s,) + idx] for s in members])

    def st(ref, idx, val):
        for j, s in enumerate(members):
            ref[(s,) + idx] = val.xs[j]
    return ld, st


def _rw_local_stages(members, is_grid, nc, cur_ref, prev_ref, next_ref, mu_ref, w0_ref, w2_ref, a0_ref, a2_ref,
                     kk_ref, ka_ref, rk_ref, rt_ref, y0_ref, bonus_ref, m_ref, c_ref):
    _load, _store = _member_io(members)

    ci = pl.program_id(1)
    has_prev = (ci > 0).astype(F32)
    has_next = (ci < nc - 1).astype(F32)
    z = _load(cur_ref, slice(None), slice(0, 1024))
    if is_grid:
        zero_row = jnp.zeros((1, W_BR), F32)
        left = _shift_rows(z[:, 0:256], zero_row, True)
        right = _shift_rows(z[:, 256:512], zero_row, False)
        up = _load(prev_ref) * has_prev
        down = _load(next_ref) * has_next
        sh = _cat([left, right, up, down], 1)
    else:
        prow = _load(prev_ref, slice(7, 8)) * has_prev
        nrow = _load(next_ref, slice(0, 1)) * has_next
        sh = _cat([_shift_rows(z[:, 0:512], prow, True), _shift_rows(z[:, 512:1024], nrow, False)], 1)
    zs = z + mu_ref[...] * (sh - z)
    yield "p"
    r, k, v = zs[:, 0:256], zs[:, 256:512], zs[:, 512:768]
    wl, al = zs[:, 768:896], zs[:, 896:1024]

    bo = _block_ones(W_BR, HEAD)
    kk = k * kk_ref[...]
    kk = kk * _rsqrt(dot1(kk * kk, bo) + EPS)
    yield "p"
    heads = lambda t: [t[:, 0:2 * HEAD], t[:, 2 * HEAD:4 * HEAD]]
    v_h = heads(v)
    row2, col2 = _iota((CH, 2 * CH), 0), _iota((CH, 2 * CH), 1) % CH
    row4, col4 = _iota((CH, 4 * CH), 0), _iota((CH, 4 * CH), 1) % CH
    eye = row2 == col2

    bonus = None
    pa_l, pam_l, prm_l, pr_l, nb_l, nk_l, eb_l, ek_l, dg_l, strict_l, incl4_l = ([] for _ in range(11))
    for d in range(2):
        rev = d == 1
        w_pre = w0_ref[d:d + 1, :] + dot1(_tanh(wl[:, RW_LORA * d:RW_LORA * (d + 1)]), w2_ref[d])
        lw = (-math.exp(-0.5)) * _sigmoid(w_pre)
        a = _sigmoid(a0_ref[d:d + 1, :] + dot1(al[:, RW_LORA * d:RW_LORA * (d + 1)], a2_ref[d]))
        k_d = k * (1.0 + (a - 1.0) * ka_ref[...])
        bterm = dot1(r * k_d * rk_ref[...], bo) * v
        bonus = bterm if bonus is None else bonus + bterm
        yield "p"

        cs = dot_xl2(_tri_mask(CH, rev, False).astype(BF16), lw)
        cs_ex = cs - lw
        cs_tot = cs[0:1] if rev else cs[CH - 1:CH]
        cs_mid = cs[CH // 2:CH // 2 + 1]
        kka = kk * a
        e_mid = _exp(cs_mid - cs)
        e_tot = _exp(cs_tot - cs)
        yield "p"
        pa_l += heads(-kk * _exp(cs_ex))
        pr_l += heads(r * _exp(cs))
        yield "p"
        pam_l += heads(-kk * _exp(cs_ex - cs_mid))
        prm_l += heads(r * _exp(cs - cs_mid))
        yield "p"
        nb_l += heads(kka * e_mid)
        nk_l += heads(k_d * e_mid)
        yield "p"
        eb_l += heads(kka * e_tot)
        ek_l += heads(k_d * e_tot)
        dg_l += [_where(eye, t, 0.0) for t in heads(_exp(cs_tot))]
        strict_l += [(row2 < col2) if rev else (row2 > col2)] * 2
        incl4_l += [(row4 <= col4) if rev else (row4 >= col4)] * 2
        yield "p"
    vv = v_h + v_h
    idx = range(4)
    bd = _pair_blockdiag

    quad = [dot1(_cat([pam_l[i], prm_l[i]], 0), _cat([bd(nb_l[i]), bd(nk_l[i])], 0), "nt") for i in idx]
    yield "h"
    w2 = 2 * HEAD
    n_l = [_where(strict_l[i], quad[i][0:CH, 0:w2], 0.0) for i in idx]
    aak = [_where(strict_l[i], quad[i][0:CH, w2:2 * w2], 0.0) for i in idx]
    g2 = [_where(incl4_l[i], quad[i][CH:2 * CH], 0.0) for i in idx]
    t_inv = yield from _tri_inv_stages(n_l)
    av = [_pair_dot(aak[i], vv[i]) for i in idx]
    yield "h"
    wu = [dot1(t_inv[i], _cat([bd(pa_l[i]), bd(av[i])], 1)) for i in idx]
    yield "h"
    zero2 = jnp.zeros((2 * CH, w2), F32)
    rhs2 = [_cat([_cat([bd(wu[i][:, 0:w2]), bd(wu[i][:, w2:2 * w2])], 1), _cat([zero2, bd(vv[i])], 1)], 0)
            for i in idx]
    gy = [dot1(g2[i], rhs2[i]) for i in idx]
    yield "h"
    zero1 = jnp.zeros((CH, w2), F32)
    same_head = (_iota((w2, 2 * w2), 0) // HEAD) == ((_iota((w2, 2 * w2), 1) % w2) // HEAD)
    lane_head = (_iota((1, 2 * w2), 1) % w2) // HEAD
    mcb = [_where(same_head, dot1(_cat([eb_l[i], ek_l[i]], 0), _cat([wu[i], _cat([zero1, vv[i]], 1)], 0), "tn"), 0.0)
           for i in idx]
    mc = [_where(lane_head == 0, x[0:HEAD], 0.0) + _where(lane_head == 1, x[HEAD:w2], 0.0) for x in mcb]
    yield "h"
    y0_parts = []
    for d in range(2):
        sl = range(2 * d, 2 * d + 2)
        _store(rt_ref, (0, d), _cat([pr_l[i] + gy[i][:, 0:w2] for i in sl], 1))
        _store(m_ref, (0, d), _cat([dg_l[i] + mc[i][:, 0:w2] for i in sl], 1))
        _store(c_ref, (0, d), _cat([mc[i][:, w2:2 * w2] for i in sl], 1))
        y0_parts.append(_cat([gy[i][:, w2:2 * w2] for i in sl], 1))
    _store(y0_ref, (), y0_parts[0] + y0_parts[1])
    _store(bonus_ref, (), bonus)


def _halo_specs(nb, is_grid, nc):
    if is_grid:
        prev = pl.BlockSpec((nb, CH, W_BR), lambda b, c: (b, jnp.maximum(c - 1, 0), 2))
        nxt = pl.BlockSpec((nb, CH, W_BR), lambda b, c: (b, jnp.minimum(c + 1, nc - 1), 3))
    else:
        rb = CH // 8
        prev = pl.BlockSpec((nb, 8, 512), lambda b, c: (b, jnp.maximum(c * rb - 1, 0), 0))
        nxt = pl.BlockSpec((nb, 8, 512), lambda b, c: (b, jnp.minimum((c + 1) * rb, nc * rb - 1), 1))
    return prev, nxt


def _scan_specs(nb, cb, nblk, rows, width):
    in_f = pl.BlockSpec((nb, cb, 1, rows, width), lambda i, j: (i, j, 0, 0, 0))
    in_r = pl.BlockSpec((nb, cb, 1, rows, width), lambda i, j: (i, nblk - 1 - j, 1, 0, 0))
    out_f = pl.BlockSpec((nb, cb, rows, width), lambda i, j: (i, j, 0, 0))
    out_r = pl.BlockSpec((nb, cb, rows, width), lambda i, j: (i, nblk - 1 - j, 0, 0))
    st = pl.BlockSpec((nb, 2, rows, width), lambda i, j: (i, 0, 0, 0))
    return in_f, in_r, out_f, out_r, st


def _rw_scan_kernel(nblk, cb, mf_ref, mr_ref, cf_ref, cr_ref, init_ref, sf_ref, sr_ref, fin_ref, s_scr):
    ci = pl.program_id(1)

    @pl.when(ci == 0)
    def _():
        s_scr[...] = init_ref[...]

    heads = lambda t: [t[:, HEAD * h:HEAD * (h + 1)] for h in range(4)]
    s = [_load(s_scr, 0), _load(s_scr, 1)]
    for j in range(cb):
        new = []
        for d, (m_ref, c_ref, o_ref) in enumerate(((mf_ref, cf_ref, sf_ref), (mr_ref, cr_ref, sr_ref))):
            jj = cb - 1 - j if d else j
            _store(o_ref, (jj,), s[d])
            m_h = heads(_load(m_ref, jj, 0))
            s_h = heads(s[d])
            prod = [dotk3(m_h[h], s_h[h]) for h in range(4)]
            new.append(_cat(prod, 1) + _load(c_ref, jj, 0))
        s = new
    _store(s_scr, (0,), s[0])
    _store(s_scr, (1,), s[1])

    @pl.when(ci == nblk - 1)
    def _():
        _store(fin_ref, (0,), s[0])
        _store(fin_ref, (1,), s[1])


def _scan_call(kernel_fn, name, a, b_arr, init):
    b, nc, _, rows, width = a.shape
    nb = _nb(b, SCAN_NB)
    cb = SCAN_CB if nc % SCAN_CB == 0 else nc
    nblk = nc // cb
    in_f, in_r, out_f, out_r, st = _scan_specs(nb, cb, nblk, rows, width)
    b_f, b_r = _scan_specs(nb, cb, nblk, b_arr.shape[3], width)[0:2]
    ent_shape = jax.ShapeDtypeStruct((b, nc, rows, width), F32)
    return pl.pallas_call(
        functools.partial(kernel_fn, nblk, cb),
        grid=(b // nb, nblk),
        in_specs=[in_f, in_r, b_f, b_r, st],
        out_specs=[out_f, out_r, st],
        out_shape=[ent_shape, ent_shape, jax.ShapeDtypeStruct((b, 2, rows, width), F32)],
        scratch_shapes=[pltpu.VMEM((nb, 2, rows, width), F32)],
        compiler_params=_params(("parallel", "arbitrary")),
        name=name,
    )(a, a, b_arr, b_arr, init)


def _rw_out_stages(members, y0_ref, bonus_ref, rt_ref, sf_ref, sr_ref, gate_ref, lnw_ref, lnb_ref):
    _load = _member_io(members)[0]
    y = _load(y0_ref)
    y = y + dot1(_load(rt_ref, 0, 0), _expand_blockdiag(_load(sf_ref, 0)))
    y = y + dot1(_load(rt_ref, 0, 1), _expand_blockdiag(_load(sr_ref, 0)))
    yield
    bo = _block_ones(W_BR, HEAD)
    y_hi, y_lo = _split2(y)
    mu = (_dg(y_hi, bo, "nn") + _dg(y_lo, bo, "nn")) * (1.0 / HEAD)
    yield
    yc = y - mu
    var = dot1(yc * yc, bo) * (1.0 / HEAD)
    yield
    yn = yc * _rsqrt(var + RW_GN_EPS) * lnw_ref[...] + lnb_ref[...]
    return (yn + _load(bonus_ref)) * _silu(_load(gate_ref))


def _ew_scan_kernel(nblk, cb, cf_ref, cr_ref, df_ref, dr_ref, init_ref, sf_ref, sr_ref, fin_ref, s_scr):
    ci = pl.program_id(1)

    @pl.when(ci == 0)
    def _():
        s_scr[...] = init_ref[...]

    s = [_load(s_scr, 0), _load(s_scr, 1)]
    for j in range(cb):
        for d, (c_ref, d_ref, o_ref) in enumerate(((cf_ref, df_ref, sf_ref), (cr_ref, dr_ref, sr_ref))):
            jj = cb - 1 - j if d else j
            _store(o_ref, (jj,), s[d])
            dec = _load(d_ref, jj, 0)
            if dec.shape[0] != s[d].shape[0]:
                dec = dec[0:1]
            s[d] = dec * s[d] + _load(c_ref, jj, 0)
    _store(s_scr, (0,), s[0])
    _store(s_scr, (1,), s[1])

    @pl.when(ci == nblk - 1)
    def _():
        _store(fin_ref, (0,), s[0])
        _store(fin_ref, (1,), s[1])


def _gla_log_decay(gl, g2_ref, gb_ref, d):
    x = dot3(gl[:, GLA_RANK * d:GLA_RANK * (d + 1)], g2_ref[d]) + gb_ref[d:d + 1, :]
    return -_softplus(-x) * (1.0 / GLA_TAU)


def _gla_local_stages(members, f_ref, g2_ref, gb_ref, cst_ref, dec_ref):
    ld, st = _member_io(members)
    k = ld(f_ref, slice(None), slice(128, 256))
    v = ld(f_ref, slice(None), slice(256, 512))
    gl = ld(f_ref, slice(None), slice(768, 800))
    sel = (_iota((128, W_BR), 0) // GLA_DK) == (_iota((128, W_BR), 1) // HEAD)
    first_row = _iota((CH, 128), 0) == 0
    ones = jnp.ones((CH, W_BR), BF16)
    las = [_gla_log_decay(gl, g2_ref, gb_ref, d) for d in range(2)]
    yield "p"
    css = [dot_xl(_tri_mask(CH, d == 1, False).astype(BF16), las[d]) for d in range(2)]
    yield "p"
    tots = [css[0][CH - 1:CH], css[1][0:1]]
    kes = [k * _exp(tots[d] - css[d]) for d in range(2)]
    yield "p"
    c_blk = [_where(sel, dot1(kes[d], v, "tn"), 0.0) for d in range(2)]
    yield "p"
    d_blk = [dot_xr(_where(first_row, _exp(tots[d]), 0.0), ones, "tn") for d in range(2)]
    yield "p"
    for d in range(2):
        st(cst_ref, (0, d), _compact_blockdiag(c_blk[d], GLA_DK))
        st(dec_ref, (0, d), _compact_blockdiag(d_blk[d], GLA_DK))


def _gla_out_stages(members, f_ref, sf_ref, sr_ref, g2_ref, gb_ref, norm_ref):
    _load = _member_io(members)[0]
    q = _load(f_ref, slice(None), slice(0, 128)) * (GLA_DK ** -0.5)
    k = _load(f_ref, slice(None), slice(128, 256))
    v = _load(f_ref, slice(None), slice(256, 512))
    gate = _load(f_ref, slice(None), slice(512, 768))
    gl = _load(f_ref, slice(None), slice(768, 800))
    khead = _iota((1, 128), 1) // GLA_DK
    vhead = _iota((1, W_BR), 1) // HEAD
    las = [_gla_log_decay(gl, g2_ref, gb_ref, d) for d in range(2)]
    yield
    css = [dot_xl(_tri_mask(CH, d == 1, False).astype(BF16), las[d]) for d in range(2)]
    yield
    mids = [css[d][CH // 2:CH // 2 + 1] for d in range(2)]
    qes = [q * _exp(css[d] - mids[d]) for d in range(2)]
    kns = [k * _exp(mids[d] - css[d]) for d in range(2)]
    yield
    sc = []
    for d in range(2):
        sc.append([_where(_tri_mask(CH, d == 1, False), dot1(_where(khead == h, qes[d], 0.0), kns[d], "nt"), 0.0)
                   for h in range(4)])
        yield
    inter = [dot1(q * _exp(css[d]), _expand_blockdiag(_load(s_ref, 0)))
             for d, s_ref in enumerate((sf_ref, sr_ref))]
    yield
    pv = [dot1(sc[0][h] + sc[1][h], _where(vhead == h, v, 0.0)) for h in range(4)]
    yield
    y = (inter[0] + inter[1]) + ((pv[0] + pv[1]) + (pv[2] + pv[3]))
    ms = dot1(y * y, _block_ones(W_BR, HEAD)) * (1.0 / HEAD)
    yield
    y = y * _rsqrt(ms + EPS) * norm_ref[...]
    return y * _silu(gate)


def _ssd_steps(dt_raw, dtb_ref, aneg_ref):
    dt = _softplus(dt_raw + dtb_ref[...])
    return dt, dt * aneg_ref[...]


def _ssd_cumsum(la):
    lane = _iota((1, 8), 1)
    la3 = _split3(la)
    tril, triu = (_tri_mask(CH, r, False).astype(BF16) for r in (False, True))
    cs_f = _dg(tril, la3[0], "nn") + (_dg(tril, la3[1], "nn") + _dg(tril, la3[2], "nn"))
    cs_r = _dg(triu, la3[0], "nn") + (_dg(triu, la3[1], "nn") + _dg(triu, la3[2], "nn"))
    cs = _where(lane < 4, cs_f, cs_r)
    return cs, _where(lane < 4, cs[CH - 1:CH], cs[0:1]), la3


def _ssd_local_stages(members, nc, cur_ref, prev_ref, next_ref, cw_ref, cb_ref, dtb_ref, aneg_ref,
                      cst_ref, dec_ref, xbc_ref):
    ld, st = _member_io(members)
    ci = pl.program_id(1)
    has_prev = (ci > 0).astype(F32)
    has_next = (ci < nc - 1).astype(F32)
    xbc = ld(cur_ref, slice(None), slice(0, 768))
    ext = _cat([ld(prev_ref) * has_prev, xbc, ld(next_ref) * has_next], 0)
    acc = cb_ref[...] + cw_ref[0:1, :] * ext[6:6 + CH]
    for j in range(1, SSD_CONV):
        acc = acc + cw_ref[j:j + 1, :] * ext[6 + j:6 + j + CH]
        yield "p"
    xbc = _silu(acc)
    st(xbc_ref, (), xbc)
    yield "p"
    x, b_rep = xbc[:, 0:256], xbc[:, 256:512]
    dt, la = _ssd_steps(ld(cur_ref, slice(None), slice(1024, 1032)), dtb_ref, aneg_ref)
    cs, cs_tot, _ = _ssd_cumsum(la)
    yield "p"
    wgt = _exp(cs_tot - cs) * dt
    sel = (_iota((W_BR, W_BR), 0) // HEAD) == (_iota((W_BR, W_BR), 1) // HEAD)
    e_tot = _exp(cs_tot)
    xs = [x * _expand_heads(wgt, 4 * d) for d in range(2)]
    yield "p"
    c_blk = [_where(sel, dot1(b_rep, xs[d], "tn"), 0.0) for d in range(2)]
    yield "p"
    for d in range(2):
        st(cst_ref, (0, d), _compact_blockdiag(c_blk[d], HEAD))
        st(dec_ref, (0, d), _bcast(_expand_heads(e_tot, 4 * d), (8, W_BR)))


def _local_kernel(is_grid, nc, *refs):
    rw_in, ssd_in, gla_in = refs[0:11], refs[11:18], refs[18:21]
    rw_out, ssd_out, gla_out = refs[21:26], refs[26:29], refs[29:31]
    nb = rw_in[0].shape[0]
    pairs = [range(s, min(s + 2, nb)) for s in range(0, nb, 2)]
    order = []
    for g in pairs:
        order += [_rw_local_stages(g, is_grid, nc, *rw_in, *rw_out), _ssd_local_stages(g, nc, *ssd_in, *ssd_out),
                  _gla_local_stages(g, *gla_in, *gla_out)]
    _run_staggered(order, 8)


def _chunk_local(f_rw, f_ssd, f_gla, p, is_grid):
    b, L, _ = f_rw.shape
    nc = L // CH
    nb = _nb(b, NB_RW)
    rb = CH // 8
    blk = lambda *shape: pl.BlockSpec((nb,) + shape, lambda i, c: (i, c) + (0,) * (len(shape) - 1))
    rw_prev, rw_next = _halo_specs(nb, is_grid, nc)
    ssd_prev = pl.BlockSpec((nb, 8, 768), lambda i, c: (i, jnp.maximum(c * rb - 1, 0), 0))
    ssd_next = pl.BlockSpec((nb, 8, 768), lambda i, c: (i, jnp.minimum((c + 1) * rb, nc * rb - 1), 0))
    st = lambda rows: (blk(1, 2, rows, W_BR), jax.ShapeDtypeStruct((b, nc, 2, rows, W_BR), F32))
    rows = lambda w: (blk(CH, w), jax.ShapeDtypeStruct((b, L, w), F32))
    outs = [st(HEAD), rows(W_BR), rows(W_BR), st(HEAD), st(HEAD),
            st(HEAD), st(8), rows(768),
            st(GLA_DK), st(GLA_DK)]
    cw = jnp.concatenate([p["conv_w_rep"], jnp.zeros((8 - SSD_CONV, 768), F32)], axis=0)
    res = pl.pallas_call(
        functools.partial(_local_kernel, is_grid, nc),
        grid=(b // nb, nc),
        in_specs=[blk(CH, W_RW), rw_prev, rw_next,
                  _full((1, 1024)), _full((2, W_BR)), _full((2, RW_LORA, W_BR)), _full((2, W_BR)),
                  _full((2, RW_LORA, W_BR)), _full((1, W_BR)), _full((1, W_BR)), _full((1, W_BR)),
                  blk(CH, W_SSD), ssd_prev, ssd_next, _full((8, 768)), _full((1, 768)), _full((1, 8)), _full((1, 8)),
                  blk(CH, W_GLA), _full((2, GLA_RANK, 128)), _full((2, 128))],
        out_specs=[o[0] for o in outs],
        out_shape=[o[1] for o in outs],
        compiler_params=_params(("parallel", "parallel")),
        name="chunk_local",
    )(f_rw, f_rw, f_rw, p["rw_mu"].reshape(1, 1024), p["rw_w0"], p["rw_w2"], p["rw_a0"], p["rw_a2"],
      p["rw_kk"].reshape(1, W_BR), p["rw_ka"].reshape(1, W_BR), p["rw_rk"].reshape(1, W_BR),
      f_ssd, f_ssd, f_ssd, cw, p["conv_b_rep"].reshape(1, 768), p["ssd_dt_bias"].reshape(1, 8),
      p["ssd_a_neg"].reshape(1, 8), f_gla, p["gla_g2"], p["gla_gb"])
    return res[0:5], res[5:8], res[8:10]


def _ssd_out_stages(members, xbc_ref, tail_ref, dtb_ref, aneg_ref, sf_ref, sr_ref, dskip_ref, norm_ref):
    _load = _member_io(members)[0]
    x = _load(xbc_ref, slice(None), slice(0, 256))
    b_rep = _load(xbc_ref, slice(None), slice(256, 512))
    c_rep = _load(xbc_ref, slice(None), slice(512, 768))
    z = _load(tail_ref, slice(None), slice(0, 256))
    dt, la = _ssd_steps(_load(tail_ref, slice(None), slice(256, 264)), dtb_ref, aneg_ref)
    cs, _, la3 = _ssd_cumsum(la)
    yield
    lane_head = _iota((1, W_BR), 1) // HEAD
    tril, triu = (_tri_mask(CH, r, False).astype(BF16) for r in (False, True))
    cst_f = _dg(la3[0], triu, "tn") + (_dg(la3[1], triu, "tn") + _dg(la3[2], triu, "tn"))
    cst_r = _dg(la3[0], tril, "tn") + (_dg(la3[1], tril, "tn") + _dg(la3[2], tril, "tn"))
    cst = _where(_iota((8, 1), 0) < 4, cst_f, cst_r)
    yield
    e_cs = _exp(cs)
    g = [dot1(c_rep[:, 2 * HEAD * grp:2 * HEAD * grp + HEAD], b_rep[:, 2 * HEAD * grp:2 * HEAD * grp + HEAD], "nt")
         for grp in range(2)]
    yield
    inter = [dot1(c_rep * _expand_heads(e_cs, 4 * d), _expand_blockdiag(_load(s_ref, 0)))
             for d, s_ref in enumerate((sf_ref, sr_ref))]
    yield
    xs = [x * _expand_heads(dt, 4 * d) for d in range(2)]
    intra = []
    for d in range(2):
        incl = _tri_mask(CH, d == 1, False)
        for h in range(4):
            col = 4 * d + h
            seg = _exp(_where(incl, cs[:, col:col + 1] - cst[col:col + 1, :], -1e30))
            intra.append(dot1(g[h // 2] * seg, _where(lane_head == h, xs[d], 0.0)))
            if h % 2:
                yield
    y = dskip_ref[...] * x + (inter[0] + inter[1])
    y = y + (((intra[0] + intra[1]) + (intra[2] + intra[3])) + ((intra[4] + intra[5]) + (intra[6] + intra[7])))
    y = y * _silu(z)
    return y * _rsqrt(_mean_last(y * y) + EPS) * norm_ref[...]


def _s5_weights(p):
    t = S5_T
    steps = jnp.arange(t + 1, dtype=F32)
    kmats, fmats, emats, lam_re, lam_im = [], [], [], [], []
    tt = jnp.arange(t)
    cmul = lambda x, y: (x[0] * y[0] - x[1] * y[1], x[0] * y[1] + x[1] * y[0])
    for d in range(2):
        a_re, a_im = p["s5_a_re"][d], p["s5_a_im"][d]
        dt = jnp.exp(p["s5_log_dt"][d])[:, None]
        mag = jnp.exp((a_re * dt)[None] * steps[:, None, None])
        ang = (a_im * dt)[None] * steps[:, None, None]
        pw = (mag * jnp.cos(ang), mag * jnp.sin(ang))
        num = (pw[0][1] - 1.0, pw[1][1])
        den = a_re * a_re + a_im * a_im
        quo = ((num[0] * a_re + num[1] * a_im) / den, (num[1] * a_re - num[0] * a_im) / den)
        b_bar = cmul((quo[0][..., None], quo[1][..., None]), (p["s5_b_re"][d], p["s5_b_im"][d]))
        c_c = (p["s5_c_re"][d], p["s5_c_im"][d])
        qb = cmul((pw[0][:t, :, :, None], pw[1][:t, :, :, None]), (b_bar[0][None], b_bar[1][None]))
        hp = lax.Precision.HIGHEST
        kd = (jnp.einsum("gcp,jgpe->gejc", c_c[0], qb[0], precision=hp)
              - jnp.einsum("gcp,jgpe->gejc", c_c[1], qb[1], precision=hp))
        xi = jnp.arange(t * S5_CH)
        t_col = xi[None, None, :] // S5_CH
        t_src = tt[:, None, None] + (-1 if d else 1) * (xi[None, :, None] // S5_CH)
        shift = ((xi[None, :, None] % S5_CH == xi[None, None, :] % S5_CH) & (t_col == t_src)).astype(F32)
        kmats.append(jnp.einsum("gex,sxy->gsey", kd.reshape(S5_G, S5_CH, t * S5_CH), shift, precision=hp)
                     .reshape(S5_G, t * S5_CH, t * S5_CH))
        fsel = (lambda x: x) if d else (lambda x: x[::-1])
        fmats += [jnp.transpose(fsel(x), (1, 0, 3, 2)).reshape(S5_G, t * S5_CH, S5_P) for x in qb]
        esel = (lambda x: x[1:][::-1]) if d else (lambda x: x[1:])
        ec = cmul((c_c[0][None], c_c[1][None]), (esel(pw[0])[:, :, None, :], esel(pw[1])[:, :, None, :]))
        ec = [jnp.transpose(x, (1, 3, 0, 2)).reshape(S5_G, S5_P, t * S5_CH) for x in ec]
        emats += [ec[0], -ec[1]]
        lam_re += [pw[0][t], pw[0][t]]
        lam_im += [-pw[1][t], pw[1][t]]
    kmat = kmats[0] + kmats[1]
    fmat = jnp.concatenate(fmats, axis=2)
    emat = jnp.concatenate(emats, axis=1)
    return (kmat.astype(BF16), fmat.astype(BF16), emat.astype(BF16),
            jnp.concatenate(lam_re, axis=1), jnp.concatenate(lam_im, axis=1))


def _s5_local_kernel(ua_ref, ub_ref, f_ref, ug_ref, z_ref):
    rb = z_ref.shape[2]
    step_rows = lambda s: jnp.concatenate([ua_ref[0, pl.ds(s, rb, stride=S5_T), :],
                                           ub_ref[0, pl.ds(s, rb, stride=S5_T), :]], axis=1)
    ug = _block_transpose16([step_rows(s) for s in range(S5_T)])
    ub = [_bf(x) for x in ug]
    zs = [_dg(ub[g], f_ref[g], "nn") for g in range(S5_G)]
    for g in range(S5_G):
        ug_ref[0, g] = ub[g]
        z_ref[0, g] = zs[g]


def _block_transpose16(arrs):
    lane_blk = _iota((1, 256), 1) // 16
    arrs = list(arrs)
    for k in range(4):
        b = 1 << k
        bit = (lane_blk & b) != 0
        new = list(arrs)
        for i in range(16):
            if not i & b:
                lo, hi = arrs[i], arrs[i | b]
                new[i] = jnp.where(bit, pltpu.roll(hi, 16 * b, 1), lo)
                new[i | b] = jnp.where(bit, hi, pltpu.roll(lo, 256 - 16 * b, 1))
        arrs = new
    return arrs


def _s5_rows(nc):
    return 64 if nc % 64 == 0 else nc


def _s5_local(f_s5, fmat):
    b, L, _ = f_s5.shape
    nc = L // S5_T
    rb = _s5_rows(nc)
    grp = pl.BlockSpec((1, S5_G, rb, 256), lambda i, j: (i, 0, j, 0))
    return pl.pallas_call(
        _s5_local_kernel,
        grid=(b, nc // rb),
        in_specs=[pl.BlockSpec((1, rb * S5_T, 128), lambda i, j: (i, j, 0)),
                  pl.BlockSpec((1, rb * S5_T, 128), lambda i, j: (i, j, 1)), _full(fmat.shape)],
        out_specs=[grp, grp],
        out_shape=[jax.ShapeDtypeStruct((b, S5_G, nc, 256), BF16), jax.ShapeDtypeStruct((b, S5_G, nc, 256), F32)],
        compiler_params=_params(("parallel", "parallel")),
        name="s5_local",
    )(f_s5, f_s5, fmat)


def _s5_scan_kernel(nblk, cb, zf_ref, zr_ref, lre_ref, lim_ref, init_ref, xf_ref, xr_ref, fin_ref, x_scr):
    ci = pl.program_id(0)

    @pl.when(ci == 0)
    def _():
        x_scr[...] = init_ref[...]

    lane = _iota((1, 256), 1)
    is_fwd = lane < 128
    even_q = (lane // S5_P) % 2 == 0
    lre, lim = lre_ref[...], lim_ref[...]
    x = _load(x_scr)
    for i in range(cb):
        for s in range(len(x.xs)):
            xf_ref[s, :, i, :] = x.xs[s]
            xr_ref[s, :, cb - 1 - i, :] = x.xs[s]
        z = _Lk([jnp.where(is_fwd, zf_ref[s, :, i, :], zr_ref[s, :, cb - 1 - i, :]) for s in range(len(x.xs))])
        swapped = _where(even_q, _roll(x, 192, 1), _roll(x, 64, 1))
        x = lre * x + lim * swapped + z
    _store(x_scr, (), x)

    @pl.when(ci == nblk - 1)
    def _():
        _store(fin_ref, (), x)


def _s5_scan(z, lam_re, lam_im, init):
    b, g, nc, w = z.shape
    cb = 8
    nblk = nc // cb
    fwd = pl.BlockSpec((b, g, cb, w), lambda j: (0, 0, j, 0))
    rev = pl.BlockSpec((b, g, cb, w), lambda j: (0, 0, nblk - 1 - j, 0))
    ent = jax.ShapeDtypeStruct((b, g, nc, w), F32)
    return pl.pallas_call(
        functools.partial(_s5_scan_kernel, nblk, cb),
        grid=(nblk,),
        in_specs=[fwd, rev, _full((g, w)), _full((g, w)), _full((b, g, w))],
        out_specs=[fwd, rev, _full((b, g, w))],
        out_shape=[ent, ent, jax.ShapeDtypeStruct((b, g, w), F32)],
        scratch_shapes=[pltpu.VMEM((b, g, w), F32)],
        compiler_params=_params(("arbitrary",)),
        name="s5_scan",
    )(z, z, lam_re, lam_im, init)


def _s5_out_kernel(ug_ref, xf_ref, xr_ref, k_ref, e_ref, ya_ref, yb_ref):
    rb = ug_ref.shape[2]
    lane = _iota((1, 256), 1)
    groups = range(S5_G)
    x_ent = [_split2(jnp.where(lane < 128, xf_ref[0, g], xr_ref[0, g])) for g in groups]
    ys = [_dg(ug_ref[0, g], k_ref[g], "nn") + (_dg(x_ent[g][0], e_ref[g], "nn") + _dg(x_ent[g][1], e_ref[g], "nn"))
          for g in groups]
    yt = _block_transpose16(ys)
    for s in range(S5_T):
        ya_ref[0, pl.ds(s, rb, stride=S5_T), :] = yt[s][:, 0:128]
        yb_ref[0, pl.ds(s, rb, stride=S5_T), :] = yt[s][:, 128:256]


def _s5_out(ug, xf, xr, kmat, emat):
    b, g, nc, w = ug.shape
    rb = _s5_rows(nc)
    grp = pl.BlockSpec((1, g, rb, w), lambda i, j: (i, 0, j, 0))
    half = pl.BlockSpec((1, rb * S5_T, 128), lambda i, j: (i, j, 0))
    half_shape = jax.ShapeDtypeStruct((b, nc * S5_T, 128), F32)
    return pl.pallas_call(
        _s5_out_kernel,
        grid=(b, nc // rb),
        in_specs=[grp, grp, grp, _full(kmat.shape), _full(emat.shape)],
        out_specs=[half, half],
        out_shape=[half_shape, half_shape],
        compiler_params=_params(("parallel", "parallel")),
        name="s5_out",
    )(ug, xf, xr, kmat, emat)


def _s5_out_stages(members, ys5a_ref, ys5b_ref, fs5_ref, s5d_ref, gluw_ref, glub_ref):
    _load = _member_io(members)[0]
    u = _load(fs5_ref, slice(None), slice(0, 256))
    y = _cat([_load(ys5a_ref), _load(ys5b_ref)], 1) + s5d_ref[...] * u
    y = 0.5 * y * (1.0 + _tanh(math.sqrt(2.0 / math.pi) * (y + 0.044715 * (y * y * y))))
    yield
    y = y * _sigmoid(dot1(y, gluw_ref[...]) + glub_ref[...])
    yield
    return y * _silu(_load(fs5_ref, slice(None), slice(256, 512)))


def _layer_out_kernel(h_ref, mod_ref, w_ref, g_ref, *refs):
    rw_refs, s5_refs, ssd_refs, gla_refs, o_ref = refs[0:8], refs[8:14], refs[14:22], refs[22:28], refs[28]
    nb, rows, d = h_ref.shape
    pairs = [range(s, min(s + 2, nb)) for s in range(0, nb, 2)]
    gens = []
    for g in pairs:
        gens += [_rw_out_stages(g, *rw_refs), _s5_out_stages(g, *s5_refs), _ssd_out_stages(g, *ssd_refs),
                 _gla_out_stages(g, *gla_refs)]
    vals = _run_staggered(gens, 8)
    tiles = []
    for k in range(len(pairs)):
        ycat = _cat(vals[4 * k:4 * k + 4], 1)
        tiles += ycat.xs
    o = dot1(jnp.concatenate(tiles, axis=0), w_ref[...])
    o = o * lax.rsqrt(jnp.mean(o * o, axis=-1, keepdims=True) + EPS) * g_ref[...]
    for s in range(nb):
        gate = mod_ref[min(s, mod_ref.shape[0] - 1)][:, 2 * d:3 * d]
        o_ref[s] = h_ref[s] + gate * o[s * rows:(s + 1) * rows]


def _layer_out(h, mod3, shared_mod_row, rw, s5, ssd, gla, p):
    b, L, d = h.shape
    nc = L // CH
    nb = _nb(b, NB)
    row = lambda w, col=0: pl.BlockSpec((nb, CH, w), lambda i, c: (i, c, col))
    ent = lambda r: pl.BlockSpec((nb, 1, r, W_BR), lambda i, c: (i, c, 0, 0))
    vec = lambda w: _full((1, w))
    if shared_mod_row is None:
        mod_spec = pl.BlockSpec((nb, 1, 3 * d), lambda i, c: (i, 0, 0))
    else:
        mod_spec = pl.BlockSpec((1, 1, 3 * d), lambda i, c: (shared_mod_row, 0, 0))
    tail = W_SSD - 768
    y0, bonus, rt, rw_sf, rw_sr, f_rw = rw
    in_specs = [row(d), mod_spec, _full((d, d)), vec(d),
                row(W_BR), row(W_BR), pl.BlockSpec((nb, 1, 2, HEAD, W_BR), lambda i, c: (i, c, 0, 0, 0)),
                ent(HEAD), ent(HEAD), row(W_BR, 4), vec(W_BR), vec(W_BR),
                row(128), row(128), row(W_S5), vec(W_BR), _full((W_BR, W_BR)), vec(W_BR),
                row(768), row(tail, 768 // tail), vec(8), vec(8), ent(HEAD), ent(HEAD), vec(W_BR), vec(W_BR),
                row(W_GLA), ent(GLA_DK), ent(GLA_DK), _full((2, GLA_RANK, 128)), _full((2, 128)), vec(W_BR)]
    args = [h, mod3, p["w_out"].astype(BF16), p["norm_post"].reshape(1, d),
            y0, bonus, rt, rw_sf, rw_sr, f_rw, p["rw_ln_w"].reshape(1, W_BR), p["rw_ln_b"].reshape(1, W_BR),
            s5[0], s5[1], s5[2], p["s5_d"].reshape(1, W_BR), p["s5_glu_w"].astype(BF16),
            p["s5_glu_b"].reshape(1, W_BR),
            ssd[0], ssd[1], p["ssd_dt_bias"].reshape(1, 8), p["ssd_a_neg"].reshape(1, 8), ssd[2], ssd[3],
            jnp.repeat(p["ssd_d"], HEAD).reshape(1, W_BR), p["ssd_norm"].reshape(1, W_BR),
            gla[0], gla[1], gla[2], p["gla_g2"], p["gla_gb"], p["gla_norm"].reshape(1, W_BR)]
    return pl.pallas_call(
        _layer_out_kernel,
        grid=(b // nb, nc),
        in_specs=in_specs,
        out_specs=row(d),
        out_shape=jax.ShapeDtypeStruct((b, L, d), F32),
        compiler_params=_params(("parallel", "parallel")),
        name="layer_out",
    )(*args)


def _permute_w_in(w_in):
    rw = w_in[:, 0:1280]
    s5 = w_in[:, 1280:1792]
    o = 1792
    x, bm, cm = w_in[:, o:o + 256], w_in[:, o + 256:o + 384], w_in[:, o + 384:o + 512]
    dt, z = w_in[:, o + 512:o + 520], w_in[:, o + 520:o + 776]
    rep = lambda m: jnp.concatenate([m[:, 0:64], m[:, 0:64], m[:, 64:128], m[:, 64:128]], axis=1)
    pad = lambda n: jnp.zeros((w_in.shape[0], n), w_in.dtype)
    ssd = jnp.concatenate([x, rep(bm), rep(cm), z, dt, pad(120)], axis=1)
    o = 1792 + 776
    q, k, v = w_in[:, o:o + 128], w_in[:, o + 128:o + 256], w_in[:, o + 256:o + 512]
    gl, gate = w_in[:, o + 512:o + 544], w_in[:, o + 544:o + 800]
    gla = jnp.concatenate([q, k, v, gate, gl, pad(96)], axis=1)
    return jnp.concatenate([rw, s5, ssd, gla], axis=1).astype(BF16)


def _layer_params(l, a):
    p = {k: v[l] for k, v in a.items()}
    rep = lambda m: jnp.concatenate([m[..., 0:64], m[..., 0:64], m[..., 64:128], m[..., 64:128]], axis=-1)
    cw, cb = p["ssd_conv_w"], p["ssd_conv_b"]
    p["conv_w_rep"] = jnp.concatenate([cw[:, 0:256], rep(cw[:, 256:384]), rep(cw[:, 384:512])], axis=1)
    p["conv_b_rep"] = jnp.concatenate([cb[0:256], rep(cb[256:384]), rep(cb[384:512])], axis=0)
    p["ssd_a_neg"] = -jnp.exp(p["ssd_a_log"])
    p["w_in_p"] = _permute_w_in(p["w_in"])
    p["s5_ops"] = _s5_weights(p)
    return p


def _mixer_layer(h, mod3, shared_mod_row, p, is_grid, init, need_out):
    b, L, d = h.shape
    tm = min(512, L)
    mod_row = (lambda i: i) if shared_mod_row is None else (lambda i: shared_mod_row)
    f_rw, f_s5, f_ssd, f_gla = _inproj(h, mod3, mod_row, p["norm_pre"], p["w_in_p"], tm)
    if init is None:
        init = (jnp.zeros((b, 2, HEAD, W_BR), F32), jnp.zeros((b, S5_G, 256), F32),
                jnp.zeros((b, 2, HEAD, W_BR), F32), jnp.zeros((b, 2, GLA_DK, W_BR), F32))
    (rt, y0, bonus, m_rw, c_rw), (ssd_c, ssd_d, ssd_xbc), (gla_c, gla_d) = _chunk_local(f_rw, f_ssd, f_gla, p, is_grid)
    rw_sf, rw_sr, rw_fin = _scan_call(_rw_scan_kernel, "rwkv_scan", m_rw, c_rw, init[0])
    kmat, fmat, emat, lam_re, lam_im = p["s5_ops"]
    ug, z5 = _s5_local(f_s5, fmat)
    s5_xf, s5_xr, s5_fin = _s5_scan(z5, lam_re, lam_im, init[1])
    ssd_sf, ssd_sr, ssd_fin = _scan_call(_ew_scan_kernel, "ssd_scan", ssd_c, ssd_d, init[2])
    gla_sf, gla_sr, gla_fin = _scan_call(_ew_scan_kernel, "gla_scan", gla_c, gla_d, init[3])
    states = (rw_fin, s5_fin, ssd_fin, gla_fin)
    if not need_out:
        return None, states
    y5a, y5b = _s5_out(ug, s5_xf, s5_xr, kmat, emat)
    h_new = _layer_out(h, mod3, shared_mod_row,
                       (y0, bonus, rt, rw_sf, rw_sr, f_rw), (y5a, y5b, f_s5),
                       (ssd_xbc, f_ssd, ssd_sf, ssd_sr), (f_gla, gla_sf, gla_sr), p)
    return h_new, states


def kernel(x, c, ctx, c_ctx, ada_w, ada_b, norm_pre, norm_post, w_in, w_out, rw_mu, rw_w0, rw_w2, rw_a0, rw_a2, rw_kk, rw_ka, rw_rk, rw_ln_w, rw_ln_b, s5_a_re, s5_a_im, s5_log_dt, s5_b_re, s5_b_im, s5_c_re, s5_c_im, s5_d, s5_glu_w, s5_glu_b, ssd_conv_w, ssd_conv_b, ssd_dt_bias, ssd_a_log, ssd_d, ssd_norm, gla_g2, gla_gb, gla_norm):
    stacked = dict(norm_pre=norm_pre, norm_post=norm_post, w_in=w_in, w_out=w_out, rw_mu=rw_mu, rw_w0=rw_w0,
                   rw_w2=rw_w2, rw_a0=rw_a0, rw_a2=rw_a2, rw_kk=rw_kk, rw_ka=rw_ka, rw_rk=rw_rk, rw_ln_w=rw_ln_w,
                   rw_ln_b=rw_ln_b, s5_a_re=s5_a_re, s5_a_im=s5_a_im, s5_log_dt=s5_log_dt, s5_b_re=s5_b_re,
                   s5_b_im=s5_b_im, s5_c_re=s5_c_re, s5_c_im=s5_c_im, s5_d=s5_d, s5_glu_w=s5_glu_w,
                   s5_glu_b=s5_glu_b, ssd_conv_w=ssd_conv_w, ssd_conv_b=ssd_conv_b, ssd_dt_bias=ssd_dt_bias,
                   ssd_a_log=ssd_a_log, ssd_d=ssd_d, ssd_norm=ssd_norm, gla_g2=gla_g2, gla_gb=gla_gb,
                   gla_norm=gla_norm)
    depth = ada_w.shape[0]
    b, d = c.shape
    rows = -(-(b + 1) // 8) * 8
    cvec = jnp.concatenate([c, c_ctx[None, :], jnp.zeros((rows - b - 1, d), F32)], axis=0)
    mod = _modulation(cvec, ada_w, ada_b)
    h, hc = x, ctx
    for l in range(depth):
        p = _layer_params(l, stacked)
        mod3 = mod[l].reshape(rows, 1, 3 * d)
        last = l == depth - 1
        hc_next, ctx_states = _mixer_layer(hc, mod3, b, p, False, None, not last)
        h, _ = _mixer_layer(h, mod3, None, p, True, ctx_states, True)
        hc = hc_next
    return h
```

```python
import functools
import math
import operator

import jax
import jax.numpy as jnp
from jax import lax
from jax.experimental import pallas as pl
from jax.experimental.pallas import tpu as pltpu

F32 = jnp.float32
BF16 = jnp.bfloat16

EPS = 1e-6
GRID_W = 64
W_BR = 256
CH = 64
HEAD = 64
RW_GN_EPS = 64e-5
RW_LORA = 64
S5_G, S5_CH, S5_P, S5_T = 16, 16, 64, 16
SSD_CONV = 5
GLA_RANK = 16
GLA_DK = 32
GLA_TAU = 16.0
W_RW, W_S5, W_SSD, W_GLA = 1280, 512, 1152, 896
N_PROJ = W_RW + W_S5 + W_SSD + W_GLA
VMEM_LIMIT = 48 * 1024 * 1024
NB_RW = 8
NB = 8
SCAN_CB = 8
SCAN_NB = 4


class _Lk:
    def __init__(self, xs):
        self.xs = list(xs)

    @property
    def shape(self):
        return self.xs[0].shape

    def __getitem__(self, idx):
        return _Lk([x[idx] for x in self.xs])

    def astype(self, dt):
        return _Lk([x.astype(dt) for x in self.xs])


def _pick(a, i):
    if isinstance(a, _Lk):
        return a.xs[i]
    if isinstance(a, (list, tuple)):
        return [_pick(e, i) for e in a]
    return a


def _width(a):
    if isinstance(a, _Lk):
        return len(a.xs)
    if isinstance(a, (list, tuple)):
        for e in a:
            w = _width(e)
            if w:
                return w
    return 0


def _lift(f):
    def g(*args, **kw):
        n = max(_width(a) for a in args)
        if not n:
            return f(*args, **kw)
        return _Lk([f(*[_pick(a, i) for a in args], **kw) for i in range(n)])
    return g


for _name, _op in (("add", operator.add), ("sub", operator.sub), ("mul", operator.mul), ("truediv", operator.truediv)):
    setattr(_Lk, f"__{_name}__", lambda s, o, _f=_lift(_op): _f(s, o))
    setattr(_Lk, f"__r{_name}__", lambda s, o, _f=_lift(_op): _f(o, s))
_Lk.__neg__ = lambda s: _lift(operator.neg)(s)

_exp = _lift(jnp.exp)
_log = _lift(jnp.log)
_tanh = _lift(jnp.tanh)
_abs = _lift(jnp.abs)
_maximum = _lift(jnp.maximum)
_rsqrt = _lift(lax.rsqrt)
_where = _lift(jnp.where)
_cat = _lift(lambda parts, axis: jnp.concatenate(parts, axis=axis))
_roll = _lift(pltpu.roll)
_bcast = _lift(jnp.broadcast_to)
_mean_last = _lift(lambda x: jnp.mean(x, axis=-1, keepdims=True))


def _load(ref, *idx):
    return _Lk([ref[(s,) + idx] for s in range(ref.shape[0])])


def _store(ref, idx, val):
    for s in range(ref.shape[0]):
        ref[(s,) + idx] = val.xs[s]


_dg = _lift(lambda a, b, mode: lax.dot_general(
    a, b, ((({"nn": 1, "nt": 1, "tn": 0}[mode],), ({"nn": 0, "nt": 1, "tn": 0}[mode],)), ((), ())),
    preferred_element_type=F32))


def _bf(a):
    return a.astype(BF16)


def _split2(a):
    hi = _bf(a)
    return hi, _bf(a - hi.astype(F32))


def _split3(a):
    hi = _bf(a)
    r = a - hi.astype(F32)
    mid = _bf(r)
    return hi, mid, _bf(r - mid.astype(F32))


def _hilo(a):
    hi = _bf(a).astype(F32)
    return hi, a - hi


def dot1(a, b, mode="nn"):
    return _dg(_bf(a), _bf(b), mode)


def dot3(a, b, mode="nn"):
    ah, al = _split2(a)
    bh, bl = _split2(b)
    return _dg(ah, bh, mode) + (_dg(ah, bl, mode) + _dg(al, bh, mode))


def dotk3(a, b, mode="nn"):
    ah, al = _hilo(a)
    bh, bl = _hilo(b)
    ax = 0 if mode == "tn" else 1
    bx = 1 if mode == "nt" else 0
    return _dg(_bf(_cat([ah, al, ah], ax)), _bf(_cat([bh, bh, bl], bx)), mode)


def dot_xl(a_exact, b, mode="nn"):
    b1, b2, b3 = _split3(b)
    return _dg(a_exact, b1, mode) + (_dg(a_exact, b2, mode) + _dg(a_exact, b3, mode))


def dot_xl2(a_exact, b, mode="nn"):
    b1, b2 = _split2(b)
    return _dg(a_exact, b1, mode) + _dg(a_exact, b2, mode)


def dot_xr(a, b_exact, mode="nn"):
    a1, a2, a3 = _split3(a)
    return _dg(a1, b_exact, mode) + (_dg(a2, b_exact, mode) + _dg(a3, b_exact, mode))


def _sigmoid(x):
    return 1.0 / (1.0 + _exp(-x))


def _silu(x):
    return x * _sigmoid(x)


def _softplus(x):
    return _maximum(x, 0.0) + _log(1.0 + _exp(-_abs(x)))


def _iota(shape, dim):
    return lax.broadcasted_iota(jnp.int32, shape, dim)


def _tri_mask(n, rev, strict):
    r, c = _iota((n, n), 0), _iota((n, n), 1)
    if rev:
        return (r < c) if strict else (r <= c)
    return (r > c) if strict else (r >= c)


def _block_ones(n, blk):
    return (_iota((n, n), 0) // blk == _iota((n, n), 1) // blk).astype(BF16)


def _expand_blockdiag(compact, n_heads=4):
    lane_head = _iota((1, compact.shape[1]), 1) // HEAD
    return _cat([_where(lane_head == h, compact, 0.0) for h in range(n_heads)], 0)


def _compact_blockdiag(blk, rows_per_head, n_heads=4):
    lane_head = _iota((1, blk.shape[1]), 1) // HEAD
    out = _where(lane_head == 0, blk[0:rows_per_head], 0.0)
    for h in range(1, n_heads):
        out = out + _where(lane_head == h, blk[h * rows_per_head:(h + 1) * rows_per_head], 0.0)
    return out


def _expand_heads(cols, first, n_heads=4):
    lane_head = _iota((1, n_heads * HEAD), 1) // HEAD
    out = _where(lane_head == 0, cols[:, first:first + 1], 0.0)
    for h in range(1, n_heads):
        out = _where(lane_head == h, cols[:, first + h:first + h + 1], out)
    return out


def _shift_rows(x, edge_row, down):
    n = x.shape[0]
    rows = _iota((n, 1), 0)
    if down:
        return _where(rows == 0, edge_row, _roll(x, 1, 0))
    return _where(rows == n - 1, edge_row, _roll(x, n - 1, 0))


def _params(sem):
    return pltpu.CompilerParams(dimension_semantics=sem, vmem_limit_bytes=VMEM_LIMIT)


def _full(shape):
    nd = len(shape)
    return pl.BlockSpec(shape, lambda *_: (0,) * nd)


def _nb(b, want):
    return want if b % want == 0 else 1


def _mod_kernel(c_ref, w_ref, b_ref, o_ref):
    s = _silu(c_ref[...])
    o_ref[0] = dot3(s, w_ref[0]) + b_ref[0]


def _modulation(cvec, ada_w, ada_b):
    depth, d, n = ada_w.shape
    rows = cvec.shape[0]
    tn = 1024
    return pl.pallas_call(
        _mod_kernel,
        grid=(depth, n // tn),
        in_specs=[pl.BlockSpec((rows, d), lambda l, j: (0, 0)),
                  pl.BlockSpec((1, d, tn), lambda l, j: (l, 0, j)),
                  pl.BlockSpec((1, 1, tn), lambda l, j: (l, 0, j))],
        out_specs=pl.BlockSpec((1, rows, tn), lambda l, j: (l, 0, j)),
        out_shape=jax.ShapeDtypeStruct((depth, rows, n), F32),
        compiler_params=_params(("parallel", "parallel")),
        name="adaln_mod",
    )(cvec, ada_w, ada_b.reshape(depth, 1, n))


def _inproj_kernel(x_ref, mod_ref, g_ref, w_ref, o_rw, o_s5, o_ssd, o_gla):
    x = x_ref[0]
    d = x.shape[1]
    hn = x * lax.rsqrt(jnp.mean(x * x, axis=-1, keepdims=True) + EPS) * g_ref[...]
    m = mod_ref[0]
    hn = hn * (1.0 + m[:, d:2 * d]) + m[:, 0:d]
    p = dot1(hn, w_ref[...])
    o_rw[0] = p[:, 0:W_RW]
    o_s5[0] = p[:, W_RW:W_RW + W_S5]
    o_ssd[0] = p[:, W_RW + W_S5:W_RW + W_S5 + W_SSD]
    o_gla[0] = p[:, W_RW + W_S5 + W_SSD:N_PROJ]


def _inproj(h, mod3, mod_row, norm_pre, w_in_p, tm):
    b, L, d = h.shape
    widths = (W_RW, W_S5, W_SSD, W_GLA)
    return pl.pallas_call(
        _inproj_kernel,
        grid=(b, L // tm),
        in_specs=[pl.BlockSpec((1, tm, d), lambda i, j: (i, j, 0)),
                  pl.BlockSpec((1, 1, 3 * d), lambda i, j: (mod_row(i), 0, 0)),
                  _full((1, d)),
                  _full(w_in_p.shape)],
        out_specs=[pl.BlockSpec((1, tm, w), lambda i, j: (i, j, 0)) for w in widths],
        out_shape=[jax.ShapeDtypeStruct((b, L, w), F32) for w in widths],
        compiler_params=_params(("parallel", "parallel")),
        name="in_proj",
    )(h, mod3, norm_pre.reshape(1, d), w_in_p)


def _pair_blockdiag(t):
    lane_head = _iota((1, 2 * HEAD), 1) // HEAD
    return _cat([_where(lane_head == 0, t, 0.0), _where(lane_head == 1, t, 0.0)], 0)


def _pair_dot(x, y):
    return dot1(x, _pair_blockdiag(y))


def _tri_inv_stages(n_mats):
    r, c = _iota((CH, 2 * CH), 0), _iota((CH, 2 * CH), 1) % CH
    blk16 = (r // 16) == (c // 16)
    blk32 = (r // 32) == (c // 32)
    eye = jnp.where(r == c, 1.0, 0.0)
    nd = [_where(blk16, n, 0.0) for n in n_mats]
    d = [eye + x for x in nd]
    p = [_pair_dot(x, x) for x in nd]
    yield "h"
    for _ in range(2):
        d = [x + _pair_dot(x, y) for x, y in zip(d, p)]
        p = [_pair_dot(y, y) for y in p]
        yield "h"
    d = [x + _pair_dot(x, y) for x, y in zip(d, p)]
    yield "h"
    for sel in (blk32 & jnp.logical_not(blk16), jnp.logical_not(blk32)):
        od = [_pair_dot(_where(sel, n, 0.0), x) for n, x in zip(n_mats, d)]
        yield "h"
        d = [x + _pair_dot(x, y) for x, y in zip(d, od)]
        yield "h"
    return d


def _run_staggered(gens, width):
    pending, active, values = list(enumerate(gens)), [], [None] * len(gens)
    while pending or active:
        while pending and len(active) < width:
            active.append(pending.pop(0))
        for item in list(active):
            try:
                next(item[1])
            except StopIteration as stop:
                values[item[0]] = stop.value
                active.remove(item)
    return values


def _member_io(members):
    def ld(ref, *idx):
        return _Lk([ref[(s,) + idx] for s in members])

    def st(ref, idx, val):
        for j, s in enumerate(members):
            ref[(s,) + idx] = val.xs[j]
    return ld, st


def _rw_local_stages(members, is_grid, nc, cur_ref, prev_ref, next_ref, mu_ref, w0_ref, w2_ref, a0_ref, a2_ref,
                     kk_ref, ka_ref, rk_ref, rt_ref, y0_ref, bonus_ref, m_ref, c_ref):
    _load, _store = _member_io(members)

    ci = pl.program_id(1)
    has_prev = (ci > 0).astype(F32)
    has_next = (ci < nc - 1).astype(F32)
    z = _load(cur_ref, slice(None), slice(0, 1024))
    if is_grid:
        zero_row = jnp.zeros((1, W_BR), F32)
        left = _shift_rows(z[:, 0:256], zero_row, True)
        right = _shift_rows(z[:, 256:512], zero_row, False)
        up = _load(prev_ref) * has_prev
        down = _load(next_ref) * has_next
        sh = _cat([left, right, up, down], 1)
    else:
        prow = _load(prev_ref, slice(7, 8)) * has_prev
        nrow = _load(next_ref, slice(0, 1)) * has_next
        sh = _cat([_shift_rows(z[:, 0:512], prow, True), _shift_rows(z[:, 512:1024], nrow, False)], 1)
    zs = z + mu_ref[...] * (sh - z)
    yield "p"
    r, k, v = zs[:, 0:256], zs[:, 256:512], zs[:, 512:768]
    wl, al = zs[:, 768:896], zs[:, 896:1024]

    bo = _block_ones(W_BR, HEAD)
    kk = k * kk_ref[...]
    kk = kk * _rsqrt(dot1(kk * kk, bo) + EPS)
    yield "p"
    heads = lambda t: [t[:, 0:2 * HEAD], t[:, 2 * HEAD:4 * HEAD]]
    v_h = heads(v)
    row2, col2 = _iota((CH, 2 * CH), 0), _iota((CH, 2 * CH), 1) % CH
    row4, col4 = _iota((CH, 4 * CH), 0), _iota((CH, 4 * CH), 1) % CH
    eye = row2 == col2

    bonus = None
    pa_l, pam_l, prm_l, pr_l, nb_l, nk_l, eb_l, ek_l, dg_l, strict_l, incl4_l = ([] for _ in range(11))
    for d in range(2):
        rev = d == 1
        w_pre = w0_ref[d:d + 1, :] + dot1(_tanh(wl[:, RW_LORA * d:RW_LORA * (d + 1)]), w2_ref[d])
        lw = (-math.exp(-0.5)) * _sigmoid(w_pre)
        a = _sigmoid(a0_ref[d:d + 1, :] + dot1(al[:, RW_LORA * d:RW_LORA * (d + 1)], a2_ref[d]))
        k_d = k * (1.0 + (a - 1.0) * ka_ref[...])
        bterm = dot1(r * k_d * rk_ref[...], bo) * v
        bonus = bterm if bonus is None else bonus + bterm
        yield "p"

        cs = dot_xl2(_tri_mask(CH, rev, False).astype(BF16), lw)
        cs_ex = cs - lw
        cs_tot = cs[0:1] if rev else cs[CH - 1:CH]
        cs_mid = cs[CH // 2:CH // 2 + 1]
        kka = kk * a
        e_mid = _exp(cs_mid - cs)
        e_tot = _exp(cs_tot - cs)
        yield "p"
        pa_l += heads(-kk * _exp(cs_ex))
        pr_l += heads(r * _exp(cs))
        yield "p"
        pam_l += heads(-kk * _exp(cs_ex - cs_mid))
        prm_l += heads(r * _exp(cs - cs_mid))
        yield "p"
        nb_l += heads(kka * e_mid)
        nk_l += heads(k_d * e_mid)
        yield "p"
        eb_l += heads(kka * e_tot)
        ek_l += heads(k_d * e_tot)
        dg_l += [_where(eye, t, 0.0) for t in heads(_exp(cs_tot))]
        strict_l += [(row2 < col2) if rev else (row2 > col2)] * 2
        incl4_l += [(row4 <= col4) if rev else (row4 >= col4)] * 2
        yield "p"
    vv = v_h + v_h
    idx = range(4)
    bd = _pair_blockdiag

    quad = [dot1(_cat([pam_l[i], prm_l[i]], 0), _cat([bd(nb_l[i]), bd(nk_l[i])], 0), "nt") for i in idx]
    yield "h"
    w2 = 2 * HEAD
    n_l = [_where(strict_l[i], quad[i][0:CH, 0:w2], 0.0) for i in idx]
    aak = [_where(strict_l[i], quad[i][0:CH, w2:2 * w2], 0.0) for i in idx]
    g2 = [_where(incl4_l[i], quad[i][CH:2 * CH], 0.0) for i in idx]
    t_inv = yield from _tri_inv_stages(n_l)
    av = [_pair_dot(aak[i], vv[i]) for i in idx]
    yield "h"
    wu = [dot1(t_inv[i], _cat([bd(pa_l[i]), bd(av[i])], 1)) for i in idx]
    yield "h"
    zero2 = jnp.zeros((2 * CH, w2), F32)
    rhs2 = [_cat([_cat([bd(wu[i][:, 0:w2]), bd(wu[i][:, w2:2 * w2])], 1), _cat([zero2, bd(vv[i])], 1)], 0)
            for i in idx]
    gy = [dot1(g2[i], rhs2[i]) for i in idx]
    yield "h"
    zero1 = jnp.zeros((CH, w2), F32)
    same_head = (_iota((w2, 2 * w2), 0) // HEAD) == ((_iota((w2, 2 * w2), 1) % w2) // HEAD)
    lane_head = (_iota((1, 2 * w2), 1) % w2) // HEAD
    mcb = [_where(same_head, dot1(_cat([eb_l[i], ek_l[i]], 0), _cat([wu[i], _cat([zero1, vv[i]], 1)], 0), "tn"), 0.0)
           for i in idx]
    mc = [_where(lane_head == 0, x[0:HEAD], 0.0) + _where(lane_head == 1, x[HEAD:w2], 0.0) for x in mcb]
    yield "h"
    y0_parts = []
    for d in range(2):
        sl = range(2 * d, 2 * d + 2)
        _store(rt_ref, (0, d), _cat([pr_l[i] + gy[i][:, 0:w2] for i in sl], 1))
        _store(m_ref, (0, d), _cat([dg_l[i] + mc[i][:, 0:w2] for i in sl], 1))
        _store(c_ref, (0, d), _cat([mc[i][:, w2:2 * w2] for i in sl], 1))
        y0_parts.append(_cat([gy[i][:, w2:2 * w2] for i in sl], 1))
    _store(y0_ref, (), y0_parts[0] + y0_parts[1])
    _store(bonus_ref, (), bonus)


def _halo_specs(nb, is_grid, nc):
    if is_grid:
        prev = pl.BlockSpec((nb, CH, W_BR), lambda b, c: (b, jnp.maximum(c - 1, 0), 2))
        nxt = pl.BlockSpec((nb, CH, W_BR), lambda b, c: (b, jnp.minimum(c + 1, nc - 1), 3))
    else:
        rb = CH // 8
        prev = pl.BlockSpec((nb, 8, 512), lambda b, c: (b, jnp.maximum(c * rb - 1, 0), 0))
        nxt = pl.BlockSpec((nb, 8, 512), lambda b, c: (b, jnp.minimum((c + 1) * rb, nc * rb - 1), 1))
    return prev, nxt


def _scan_specs(nb, cb, nblk, rows, width):
    in_f = pl.BlockSpec((nb, cb, 1, rows, width), lambda i, j: (i, j, 0, 0, 0))
    in_r = pl.BlockSpec((nb, cb, 1, rows, width), lambda i, j: (i, nblk - 1 - j, 1, 0, 0))
    out_f = pl.BlockSpec((nb, cb, rows, width), lambda i, j: (i, j, 0, 0))
    out_r = pl.BlockSpec((nb, cb, rows, width), lambda i, j: (i, nblk - 1 - j, 0, 0))
    st = pl.BlockSpec((nb, 2, rows, width), lambda i, j: (i, 0, 0, 0))
    return in_f, in_r, out_f, out_r, st


def _rw_scan_kernel(nblk, cb, mf_ref, mr_ref, cf_ref, cr_ref, init_ref, sf_ref, sr_ref, fin_ref, s_scr):
    ci = pl.program_id(1)

    @pl.when(ci == 0)
    def _():
        s_scr[...] = init_ref[...]

    heads = lambda t: [t[:, HEAD * h:HEAD * (h + 1)] for h in range(4)]
    s = [_load(s_scr, 0), _load(s_scr, 1)]
    for j in range(cb):
        new = []
        for d, (m_ref, c_ref, o_ref) in enumerate(((mf_ref, cf_ref, sf_ref), (mr_ref, cr_ref, sr_ref))):
            jj = cb - 1 - j if d else j
            _store(o_ref, (jj,), s[d])
            m_h = heads(_load(m_ref, jj, 0))
            s_h = heads(s[d])
            prod = [dotk3(m_h[h], s_h[h]) for h in range(4)]
            new.append(_cat(prod, 1) + _load(c_ref, jj, 0))
        s = new
    _store(s_scr, (0,), s[0])
    _store(s_scr, (1,), s[1])

    @pl.when(ci == nblk - 1)
    def _():
        _store(fin_ref, (0,), s[0])
        _store(fin_ref, (1,), s[1])


def _scan_call(kernel_fn, name, a, b_arr, init):
    b, nc, _, rows, width = a.shape
    nb = _nb(b, SCAN_NB)
    cb = SCAN_CB if nc % SCAN_CB == 0 else nc
    nblk = nc // cb
    in_f, in_r, out_f, out_r, st = _scan_specs(nb, cb, nblk, rows, width)
    b_f, b_r = _scan_specs(nb, cb, nblk, b_arr.shape[3], width)[0:2]
    ent_shape = jax.ShapeDtypeStruct((b, nc, rows, width), F32)
    return pl.pallas_call(
        functools.partial(kernel_fn, nblk, cb),
        grid=(b // nb, nblk),
        in_specs=[in_f, in_r, b_f, b_r, st],
        out_specs=[out_f, out_r, st],
        out_shape=[ent_shape, ent_shape, jax.ShapeDtypeStruct((b, 2, rows, width), F32)],
        scratch_shapes=[pltpu.VMEM((nb, 2, rows, width), F32)],
        compiler_params=_params(("parallel", "arbitrary")),
        name=name,
    )(a, a, b_arr, b_arr, init)


def _rw_out_stages(members, y0_ref, bonus_ref, rt_ref, sf_ref, sr_ref, gate_ref, lnw_ref, lnb_ref):
    _load = _member_io(members)[0]
    y = _load(y0_ref)
    y = y + dot1(_load(rt_ref, 0, 0), _expand_blockdiag(_load(sf_ref, 0)))
    y = y + dot1(_load(rt_ref, 0, 1), _expand_blockdiag(_load(sr_ref, 0)))
    yield
    bo = _block_ones(W_BR, HEAD)
    y_hi, y_lo = _split2(y)
    mu = (_dg(y_hi, bo, "nn") + _dg(y_lo, bo, "nn")) * (1.0 / HEAD)
    yield
    yc = y - mu
    var = dot1(yc * yc, bo) * (1.0 / HEAD)
    yield
    yn = yc * _rsqrt(var + RW_GN_EPS) * lnw_ref[...] + lnb_ref[...]
    return (yn + _load(bonus_ref)) * _silu(_load(gate_ref))


def _ew_scan_kernel(nblk, cb, cf_ref, cr_ref, df_ref, dr_ref, init_ref, sf_ref, sr_ref, fin_ref, s_scr):
    ci = pl.program_id(1)

    @pl.when(ci == 0)
    def _():
        s_scr[...] = init_ref[...]

    s = [_load(s_scr, 0), _load(s_scr, 1)]
    for j in range(cb):
        for d, (c_ref, d_ref, o_ref) in enumerate(((cf_ref, df_ref, sf_ref), (cr_ref, dr_ref, sr_ref))):
            jj = cb - 1 - j if d else j
            _store(o_ref, (jj,), s[d])
            dec = _load(d_ref, jj, 0)
            if dec.shape[0] != s[d].shape[0]:
                dec = dec[0:1]
            s[d] = dec * s[d] + _load(c_ref, jj, 0)
    _store(s_scr, (0,), s[0])
    _store(s_scr, (1,), s[1])

    @pl.when(ci == nblk - 1)
    def _():
        _store(fin_ref, (0,), s[0])
        _store(fin_ref, (1,), s[1])


def _gla_log_decay(gl, g2_ref, gb_ref, d):
    x = dot3(gl[:, GLA_RANK * d:GLA_RANK * (d + 1)], g2_ref[d]) + gb_ref[d:d + 1, :]
    return -_softplus(-x) * (1.0 / GLA_TAU)


def _gla_local_stages(members, f_ref, g2_ref, gb_ref, cst_ref, dec_ref):
    ld, st = _member_io(members)
    k = ld(f_ref, slice(None), slice(128, 256))
    v = ld(f_ref, slice(None), slice(256, 512))
    gl = ld(f_ref, slice(None), slice(768, 800))
    sel = (_iota((128, W_BR), 0) // GLA_DK) == (_iota((128, W_BR), 1) // HEAD)
    first_row = _iota((CH, 128), 0) == 0
    ones = jnp.ones((CH, W_BR), BF16)
    las = [_gla_log_decay(gl, g2_ref, gb_ref, d) for d in range(2)]
    yield "p"
    css = [dot_xl(_tri_mask(CH, d == 1, False).astype(BF16), las[d]) for d in range(2)]
    yield "p"
    tots = [css[0][CH - 1:CH], css[1][0:1]]
    kes = [k * _exp(tots[d] - css[d]) for d in range(2)]
    yield "p"
    c_blk = [_where(sel, dot1(kes[d], v, "tn"), 0.0) for d in range(2)]
    yield "p"
    d_blk = [dot_xr(_where(first_row, _exp(tots[d]), 0.0), ones, "tn") for d in range(2)]
    yield "p"
    for d in range(2):
        st(cst_ref, (0, d), _compact_blockdiag(c_blk[d], GLA_DK))
        st(dec_ref, (0, d), _compact_blockdiag(d_blk[d], GLA_DK))


def _gla_out_stages(members, f_ref, sf_ref, sr_ref, g2_ref, gb_ref, norm_ref):
    _load = _member_io(members)[0]
    q = _load(f_ref, slice(None), slice(0, 128)) * (GLA_DK ** -0.5)
    k = _load(f_ref, slice(None), slice(128, 256))
    v = _load(f_ref, slice(None), slice(256, 512))
    gate = _load(f_ref, slice(None), slice(512, 768))
    gl = _load(f_ref, slice(None), slice(768, 800))
    khead = _iota((1, 128), 1) // GLA_DK
    vhead = _iota((1, W_BR), 1) // HEAD
    las = [_gla_log_decay(gl, g2_ref, gb_ref, d) for d in range(2)]
    yield
    css = [dot_xl(_tri_mask(CH, d == 1, False).astype(BF16), las[d]) for d in range(2)]
    yield
    mids = [css[d][CH // 2:CH // 2 + 1] for d in range(2)]
    qes = [q * _exp(css[d] - mids[d]) for d in range(2)]
    kns = [k * _exp(mids[d] - css[d]) for d in range(2)]
    yield
    sc = []
    for d in range(2):
        sc.append([_where(_tri_mask(CH, d == 1, False), dot1(_where(khead == h, qes[d], 0.0), kns[d], "nt"), 0.0)
                   for h in range(4)])
        yield
    inter = [dot1(q * _exp(css[d]), _expand_blockdiag(_load(s_ref, 0)))
             for d, s_ref in enumerate((sf_ref, sr_ref))]
    yield
    pv = [dot1(sc[0][h] + sc[1][h], _where(vhead == h, v, 0.0)) for h in range(4)]
    yield
    y = (inter[0] + inter[1]) + ((pv[0] + pv[1]) + (pv[2] + pv[3]))
    ms = dot1(y * y, _block_ones(W_BR, HEAD)) * (1.0 / HEAD)
    yield
    y = y * _rsqrt(ms + EPS) * norm_ref[...]
    return y * _silu(gate)


def _ssd_steps(dt_raw, dtb_ref, aneg_ref):
    dt = _softplus(dt_raw + dtb_ref[...])
    return dt, dt * aneg_ref[...]


def _ssd_cumsum(la):
    lane = _iota((1, 8), 1)
    la3 = _split3(la)
    tril, triu = (_tri_mask(CH, r, False).astype(BF16) for r in (False, True))
    cs_f = _dg(tril, la3[0], "nn") + (_dg(tril, la3[1], "nn") + _dg(tril, la3[2], "nn"))
    cs_r = _dg(triu, la3[0], "nn") + (_dg(triu, la3[1], "nn") + _dg(triu, la3[2], "nn"))
    cs = _where(lane < 4, cs_f, cs_r)
    return cs, _where(lane < 4, cs[CH - 1:CH], cs[0:1]), la3


def _ssd_local_stages(members, nc, cur_ref, prev_ref, next_ref, cw_ref, cb_ref, dtb_ref, aneg_ref,
                      cst_ref, dec_ref, xbc_ref):
    ld, st = _member_io(members)
    ci = pl.program_id(1)
    has_prev = (ci > 0).astype(F32)
    has_next = (ci < nc - 1).astype(F32)
    xbc = ld(cur_ref, slice(None), slice(0, 768))
    ext = _cat([ld(prev_ref) * has_prev, xbc, ld(next_ref) * has_next], 0)
    acc = cb_ref[...] + cw_ref[0:1, :] * ext[6:6 + CH]
    for j in range(1, SSD_CONV):
        acc = acc + cw_ref[j:j + 1, :] * ext[6 + j:6 + j + CH]
        yield "p"
    xbc = _silu(acc)
    st(xbc_ref, (), xbc)
    yield "p"
    x, b_rep = xbc[:, 0:256], xbc[:, 256:512]
    dt, la = _ssd_steps(ld(cur_ref, slice(None), slice(1024, 1032)), dtb_ref, aneg_ref)
    cs, cs_tot, _ = _ssd_cumsum(la)
    yield "p"
    wgt = _exp(cs_tot - cs) * dt
    sel = (_iota((W_BR, W_BR), 0) // HEAD) == (_iota((W_BR, W_BR), 1) // HEAD)
    e_tot = _exp(cs_tot)
    xs = [x * _expand_heads(wgt, 4 * d) for d in range(2)]
    yield "p"
    c_blk = [_where(sel, dot1(b_rep, xs[d], "tn"), 0.0) for d in range(2)]
    yield "p"
    for d in range(2):
        st(cst_ref, (0, d), _compact_blockdiag(c_blk[d], HEAD))
        st(dec_ref, (0, d), _bcast(_expand_heads(e_tot, 4 * d), (8, W_BR)))


def _local_kernel(is_grid, nc, *refs):
    rw_in, ssd_in, gla_in = refs[0:11], refs[11:18], refs[18:21]
    rw_out, ssd_out, gla_out = refs[21:26], refs[26:29], refs[29:31]
    nb = rw_in[0].shape[0]
    pairs = [range(s, min(s + 4, nb)) for s in range(0, nb, 4)]
    order = []
    for g in pairs:
        order += [_rw_local_stages(g, is_grid, nc, *rw_in, *rw_out), _ssd_local_stages(g, nc, *ssd_in, *ssd_out),
                  _gla_local_stages(g, *gla_in, *gla_out)]
    _run_staggered(order, 8)


def _chunk_local(f_rw, f_ssd, f_gla, p, is_grid):
    b, L, _ = f_rw.shape
    nc = L // CH
    nb = _nb(b, NB_RW)
    rb = CH // 8
    blk = lambda *shape: pl.BlockSpec((nb,) + shape, lambda i, c: (i, c) + (0,) * (len(shape) - 1))
    rw_prev, rw_next = _halo_specs(nb, is_grid, nc)
    ssd_prev = pl.BlockSpec((nb, 8, 768), lambda i, c: (i, jnp.maximum(c * rb - 1, 0), 0))
    ssd_next = pl.BlockSpec((nb, 8, 768), lambda i, c: (i, jnp.minimum((c + 1) * rb, nc * rb - 1), 0))
    st = lambda rows: (blk(1, 2, rows, W_BR), jax.ShapeDtypeStruct((b, nc, 2, rows, W_BR), F32))
    rows = lambda w: (blk(CH, w), jax.ShapeDtypeStruct((b, L, w), F32))
    outs = [st(HEAD), rows(W_BR), rows(W_BR), st(HEAD), st(HEAD),
            st(HEAD), st(8), rows(768),
            st(GLA_DK), st(GLA_DK)]
    cw = jnp.concatenate([p["conv_w_rep"], jnp.zeros((8 - SSD_CONV, 768), F32)], axis=0)
    res = pl.pallas_call(
        functools.partial(_local_kernel, is_grid, nc),
        grid=(b // nb, nc),
        in_specs=[blk(CH, W_RW), rw_prev, rw_next,
                  _full((1, 1024)), _full((2, W_BR)), _full((2, RW_LORA, W_BR)), _full((2, W_BR)),
                  _full((2, RW_LORA, W_BR)), _full((1, W_BR)), _full((1, W_BR)), _full((1, W_BR)),
                  blk(CH, W_SSD), ssd_prev, ssd_next, _full((8, 768)), _full((1, 768)), _full((1, 8)), _full((1, 8)),
                  blk(CH, W_GLA), _full((2, GLA_RANK, 128)), _full((2, 128))],
        out_specs=[o[0] for o in outs],
        out_shape=[o[1] for o in outs],
        compiler_params=_params(("parallel", "parallel")),
        name="chunk_local",
    )(f_rw, f_rw, f_rw, p["rw_mu"].reshape(1, 1024), p["rw_w0"], p["rw_w2"], p["rw_a0"], p["rw_a2"],
      p["rw_kk"].reshape(1, W_BR), p["rw_ka"].reshape(1, W_BR), p["rw_rk"].reshape(1, W_BR),
      f_ssd, f_ssd, f_ssd, cw, p["conv_b_rep"].reshape(1, 768), p["ssd_dt_bias"].reshape(1, 8),
      p["ssd_a_neg"].reshape(1, 8), f_gla, p["gla_g2"], p["gla_gb"])
    return res[0:5], res[5:8], res[8:10]


def _ssd_out_stages(members, xbc_ref, tail_ref, dtb_ref, aneg_ref, sf_ref, sr_ref, dskip_ref, norm_ref):
    _load = _member_io(members)[0]
    x = _load(xbc_ref, slice(None), slice(0, 256))
    b_rep = _load(xbc_ref, slice(None), slice(256, 512))
    c_rep = _load(xbc_ref, slice(None), slice(512, 768))
    z = _load(tail_ref, slice(None), slice(0, 256))
    dt, la = _ssd_steps(_load(tail_ref, slice(None), slice(256, 264)), dtb_ref, aneg_ref)
    cs, _, la3 = _ssd_cumsum(la)
    yield
    lane_head = _iota((1, W_BR), 1) // HEAD
    tril, triu = (_tri_mask(CH, r, False).astype(BF16) for r in (False, True))
    cst_f = _dg(la3[0], triu, "tn") + (_dg(la3[1], triu, "tn") + _dg(la3[2], triu, "tn"))
    cst_r = _dg(la3[0], tril, "tn") + (_dg(la3[1], tril, "tn") + _dg(la3[2], tril, "tn"))
    cst = _where(_iota((8, 1), 0) < 4, cst_f, cst_r)
    yield
    e_cs = _exp(cs)
    g = [dot1(c_rep[:, 2 * HEAD * grp:2 * HEAD * grp + HEAD], b_rep[:, 2 * HEAD * grp:2 * HEAD * grp + HEAD], "nt")
         for grp in range(2)]
    yield
    inter = [dot1(c_rep * _expand_heads(e_cs, 4 * d), _expand_blockdiag(_load(s_ref, 0)))
             for d, s_ref in enumerate((sf_ref, sr_ref))]
    yield
    xs = [x * _expand_heads(dt, 4 * d) for d in range(2)]
    intra = []
    for d in range(2):
        incl = _tri_mask(CH, d == 1, False)
        for h in range(4):
            col = 4 * d + h
            seg = _exp(_where(incl, cs[:, col:col + 1] - cst[col:col + 1, :], -1e30))
            intra.append(dot1(g[h // 2] * seg, _where(lane_head == h, xs[d], 0.0)))
            if h % 2:
                yield
    y = dskip_ref[...] * x + (inter[0] + inter[1])
    y = y + (((intra[0] + intra[1]) + (intra[2] + intra[3])) + ((intra[4] + intra[5]) + (intra[6] + intra[7])))
    y = y * _silu(z)
    return y * _rsqrt(_mean_last(y * y) + EPS) * norm_ref[...]


def _s5_weights(p):
    t = S5_T
    steps = jnp.arange(t + 1, dtype=F32)
    kmats, fmats, emats, lam_re, lam_im = [], [], [], [], []
    tt = jnp.arange(t)
    cmul = lambda x, y: (x[0] * y[0] - x[1] * y[1], x[0] * y[1] + x[1] * y[0])
    for d in range(2):
        a_re, a_im = p["s5_a_re"][d], p["s5_a_im"][d]
        dt = jnp.exp(p["s5_log_dt"][d])[:, None]
        mag = jnp.exp((a_re * dt)[None] * steps[:, None, None])
        ang = (a_im * dt)[None] * steps[:, None, None]
        pw = (mag * jnp.cos(ang), mag * jnp.sin(ang))
        num = (pw[0][1] - 1.0, pw[1][1])
        den = a_re * a_re + a_im * a_im
        quo = ((num[0] * a_re + num[1] * a_im) / den, (num[1] * a_re - num[0] * a_im) / den)
        b_bar = cmul((quo[0][..., None], quo[1][..., None]), (p["s5_b_re"][d], p["s5_b_im"][d]))
        c_c = (p["s5_c_re"][d], p["s5_c_im"][d])
        qb = cmul((pw[0][:t, :, :, None], pw[1][:t, :, :, None]), (b_bar[0][None], b_bar[1][None]))
        hp = lax.Precision.HIGHEST
        kd = (jnp.einsum("gcp,jgpe->gejc", c_c[0], qb[0], precision=hp)
              - jnp.einsum("gcp,jgpe->gejc", c_c[1], qb[1], precision=hp))
        xi = jnp.arange(t * S5_CH)
        t_col = xi[None, None, :] // S5_CH
        t_src = tt[:, None, None] + (-1 if d else 1) * (xi[None, :, None] // S5_CH)
        shift = ((xi[None, :, None] % S5_CH == xi[None, None, :] % S5_CH) & (t_col == t_src)).astype(F32)
        kmats.append(jnp.einsum("gex,sxy->gsey", kd.reshape(S5_G, S5_CH, t * S5_CH), shift, precision=hp)
                     .reshape(S5_G, t * S5_CH, t * S5_CH))
        fsel = (lambda x: x) if d else (lambda x: x[::-1])
        fmats += [jnp.transpose(fsel(x), (1, 0, 3, 2)).reshape(S5_G, t * S5_CH, S5_P) for x in qb]
        esel = (lambda x: x[1:][::-1]) if d else (lambda x: x[1:])
        ec = cmul((c_c[0][None], c_c[1][None]), (esel(pw[0])[:, :, None, :], esel(pw[1])[:, :, None, :]))
        ec = [jnp.transpose(x, (1, 3, 0, 2)).reshape(S5_G, S5_P, t * S5_CH) for x in ec]
        emats += [ec[0], -ec[1]]
        lam_re += [pw[0][t], pw[0][t]]
        lam_im += [-pw[1][t], pw[1][t]]
    kmat = kmats[0] + kmats[1]
    fmat = jnp.concatenate(fmats, axis=2)
    emat = jnp.concatenate(emats, axis=1)
    return (kmat.astype(BF16), fmat.astype(BF16), emat.astype(BF16),
            jnp.concatenate(lam_re, axis=1), jnp.concatenate(lam_im, axis=1))


def _s5_local_kernel(ua_ref, ub_ref, f_ref, ug_ref, z_ref):
    rb = z_ref.shape[2]
    step_rows = lambda s: jnp.concatenate([ua_ref[0, pl.ds(s, rb, stride=S5_T), :],
                                           ub_ref[0, pl.ds(s, rb, stride=S5_T), :]], axis=1)
    ug = _block_transpose16([step_rows(s) for s in range(S5_T)])
    ub = [_bf(x) for x in ug]
    zs = [_dg(ub[g], f_ref[g], "nn") for g in range(S5_G)]
    for g in range(S5_G):
        ug_ref[0, g] = ub[g]
        z_ref[0, g] = zs[g]


def _block_transpose16(arrs):
    lane_blk = _iota((1, 256), 1) // 16
    arrs = list(arrs)
    for k in range(4):
        b = 1 << k
        bit = (lane_blk & b) != 0
        new = list(arrs)
        for i in range(16):
            if not i & b:
                lo, hi = arrs[i], arrs[i | b]
                new[i] = jnp.where(bit, pltpu.roll(hi, 16 * b, 1), lo)
                new[i | b] = jnp.where(bit, hi, pltpu.roll(lo, 256 - 16 * b, 1))
        arrs = new
    return arrs


def _s5_rows(nc):
    return 64 if nc % 64 == 0 else nc


def _s5_local(f_s5, fmat):
    b, L, _ = f_s5.shape
    nc = L // S5_T
    rb = _s5_rows(nc)
    grp = pl.BlockSpec((1, S5_G, rb, 256), lambda i, j: (i, 0, j, 0))
    return pl.pallas_call(
        _s5_local_kernel,
        grid=(b, nc // rb),
        in_specs=[pl.BlockSpec((1, rb * S5_T, 128), lambda i, j: (i, j, 0)),
                  pl.BlockSpec((1, rb * S5_T, 128), lambda i, j: (i, j, 1)), _full(fmat.shape)],
        out_specs=[grp, grp],
        out_shape=[jax.ShapeDtypeStruct((b, S5_G, nc, 256), BF16), jax.ShapeDtypeStruct((b, S5_G, nc, 256), F32)],
        compiler_params=_params(("parallel", "parallel")),
        name="s5_local",
    )(f_s5, f_s5, fmat)


def _s5_scan_kernel(nblk, cb, zf_ref, zr_ref, lre_ref, lim_ref, init_ref, xf_ref, xr_ref, fin_ref, x_scr):
    ci = pl.program_id(0)

    @pl.when(ci == 0)
    def _():
        x_scr[...] = init_ref[...]

    lane = _iota((1, 256), 1)
    is_fwd = lane < 128
    even_q = (lane // S5_P) % 2 == 0
    lre, lim = lre_ref[...], lim_ref[...]
    x = _load(x_scr)
    for i in range(cb):
        for s in range(len(x.xs)):
            xf_ref[s, :, i, :] = x.xs[s]
            xr_ref[s, :, cb - 1 - i, :] = x.xs[s]
        z = _Lk([jnp.where(is_fwd, zf_ref[s, :, i, :], zr_ref[s, :, cb - 1 - i, :]) for s in range(len(x.xs))])
        swapped = _where(even_q, _roll(x, 192, 1), _roll(x, 64, 1))
        x = lre * x + lim * swapped + z
    _store(x_scr, (), x)

    @pl.when(ci == nblk - 1)
    def _():
        _store(fin_ref, (), x)


def _s5_scan(z, lam_re, lam_im, init):
    b, g, nc, w = z.shape
    cb = 8
    nblk = nc // cb
    fwd = pl.BlockSpec((b, g, cb, w), lambda j: (0, 0, j, 0))
    rev = pl.BlockSpec((b, g, cb, w), lambda j: (0, 0, nblk - 1 - j, 0))
    ent = jax.ShapeDtypeStruct((b, g, nc, w), F32)
    return pl.pallas_call(
        functools.partial(_s5_scan_kernel, nblk, cb),
        grid=(nblk,),
        in_specs=[fwd, rev, _full((g, w)), _full((g, w)), _full((b, g, w))],
        out_specs=[fwd, rev, _full((b, g, w))],
        out_shape=[ent, ent, jax.ShapeDtypeStruct((b, g, w), F32)],
        scratch_shapes=[pltpu.VMEM((b, g, w), F32)],
        compiler_params=_params(("arbitrary",)),
        name="s5_scan",
    )(z, z, lam_re, lam_im, init)


def _s5_out_kernel(ug_ref, xf_ref, xr_ref, k_ref, e_ref, ya_ref, yb_ref):
    rb = ug_ref.shape[2]
    lane = _iota((1, 256), 1)
    groups = range(S5_G)
    x_ent = [_split2(jnp.where(lane < 128, xf_ref[0, g], xr_ref[0, g])) for g in groups]
    ys = [_dg(ug_ref[0, g], k_ref[g], "nn") + (_dg(x_ent[g][0], e_ref[g], "nn") + _dg(x_ent[g][1], e_ref[g], "nn"))
          for g in groups]
    yt = _block_transpose16(ys)
    for s in range(S5_T):
        ya_ref[0, pl.ds(s, rb, stride=S5_T), :] = yt[s][:, 0:128]
        yb_ref[0, pl.ds(s, rb, stride=S5_T), :] = yt[s][:, 128:256]


def _s5_out(ug, xf, xr, kmat, emat):
    b, g, nc, w = ug.shape
    rb = _s5_rows(nc)
    grp = pl.BlockSpec((1, g, rb, w), lambda i, j: (i, 0, j, 0))
    half = pl.BlockSpec((1, rb * S5_T, 128), lambda i, j: (i, j, 0))
    half_shape = jax.ShapeDtypeStruct((b, nc * S5_T, 128), F32)
    return pl.pallas_call(
        _s5_out_kernel,
        grid=(b, nc // rb),
        in_specs=[grp, grp, grp, _full(kmat.shape), _full(emat.shape)],
        out_specs=[half, half],
        out_shape=[half_shape, half_shape],
        compiler_params=_params(("parallel", "parallel")),
        name="s5_out",
    )(ug, xf, xr, kmat, emat)


def _s5_out_stages(members, ys5a_ref, ys5b_ref, fs5_ref, s5d_ref, gluw_ref, glub_ref):
    _load = _member_io(members)[0]
    u = _load(fs5_ref, slice(None), slice(0, 256))
    y = _cat([_load(ys5a_ref), _load(ys5b_ref)], 1) + s5d_ref[...] * u
    y = 0.5 * y * (1.0 + _tanh(math.sqrt(2.0 / math.pi) * (y + 0.044715 * (y * y * y))))
    yield
    y = y * _sigmoid(dot1(y, gluw_ref[...]) + glub_ref[...])
    yield
    return y * _silu(_load(fs5_ref, slice(None), slice(256, 512)))


def _layer_out_kernel(h_ref, mod_ref, w_ref, g_ref, *refs):
    rw_refs, s5_refs, ssd_refs, gla_refs, o_ref = refs[0:8], refs[8:14], refs[14:22], refs[22:28], refs[28]
    nb, rows, d = h_ref.shape
    pairs = [range(s, min(s + 4, nb)) for s in range(0, nb, 4)]
    gens = []
    for g in pairs:
        gens += [_rw_out_stages(g, *rw_refs), _s5_out_stages(g, *s5_refs), _ssd_out_stages(g, *ssd_refs),
                 _gla_out_stages(g, *gla_refs)]
    vals = _run_staggered(gens, 8)
    tiles = []
    for k in range(len(pairs)):
        ycat = _cat(vals[4 * k:4 * k + 4], 1)
        tiles += ycat.xs
    o = dot1(jnp.concatenate(tiles, axis=0), w_ref[...])
    o = o * lax.rsqrt(jnp.mean(o * o, axis=-1, keepdims=True) + EPS) * g_ref[...]
    for s in range(nb):
        gate = mod_ref[min(s, mod_ref.shape[0] - 1)][:, 2 * d:3 * d]
        o_ref[s] = h_ref[s] + gate * o[s * rows:(s + 1) * rows]


def _layer_out(h, mod3, shared_mod_row, rw, s5, ssd, gla, p):
    b, L, d = h.shape
    nc = L // CH
    nb = _nb(b, NB)
    row = lambda w, col=0: pl.BlockSpec((nb, CH, w), lambda i, c: (i, c, col))
    ent = lambda r: pl.BlockSpec((nb, 1, r, W_BR), lambda i, c: (i, c, 0, 0))
    vec = lambda w: _full((1, w))
    if shared_mod_row is None:
        mod_spec = pl.BlockSpec((nb, 1, 3 * d), lambda i, c: (i, 0, 0))
    else:
        mod_spec = pl.BlockSpec((1, 1, 3 * d), lambda i, c: (shared_mod_row, 0, 0))
    tail = W_SSD - 768
    y0, bonus, rt, rw_sf, rw_sr, f_rw = rw
    in_specs = [row(d), mod_spec, _full((d, d)), vec(d),
                row(W_BR), row(W_BR), pl.BlockSpec((nb, 1, 2, HEAD, W_BR), lambda i, c: (i, c, 0, 0, 0)),
                ent(HEAD), ent(HEAD), row(W_BR, 4), vec(W_BR), vec(W_BR),
                row(128), row(128), row(W_S5), vec(W_BR), _full((W_BR, W_BR)), vec(W_BR),
                row(768), row(tail, 768 // tail), vec(8), vec(8), ent(HEAD), ent(HEAD), vec(W_BR), vec(W_BR),
                row(W_GLA), ent(GLA_DK), ent(GLA_DK), _full((2, GLA_RANK, 128)), _full((2, 128)), vec(W_BR)]
    args = [h, mod3, p["w_out"].astype(BF16), p["norm_post"].reshape(1, d),
            y0, bonus, rt, rw_sf, rw_sr, f_rw, p["rw_ln_w"].reshape(1, W_BR), p["rw_ln_b"].reshape(1, W_BR),
            s5[0], s5[1], s5[2], p["s5_d"].reshape(1, W_BR), p["s5_glu_w"].astype(BF16),
            p["s5_glu_b"].reshape(1, W_BR),
            ssd[0], ssd[1], p["ssd_dt_bias"].reshape(1, 8), p["ssd_a_neg"].reshape(1, 8), ssd[2], ssd[3],
            jnp.repeat(p["ssd_d"], HEAD).reshape(1, W_BR), p["ssd_norm"].reshape(1, W_BR),
            gla[0], gla[1], gla[2], p["gla_g2"], p["gla_gb"], p["gla_norm"].reshape(1, W_BR)]
    return pl.pallas_call(
        _layer_out_kernel,
        grid=(b // nb, nc),
        in_specs=in_specs,
        out_specs=row(d),
        out_shape=jax.ShapeDtypeStruct((b, L, d), F32),
        compiler_params=_params(("parallel", "parallel")),
        name="layer_out",
    )(*args)


def _permute_w_in(w_in):
    rw = w_in[:, 0:1280]
    s5 = w_in[:, 1280:1792]
    o = 1792
    x, bm, cm = w_in[:, o:o + 256], w_in[:, o + 256:o + 384], w_in[:, o + 384:o + 512]
    dt, z = w_in[:, o + 512:o + 520], w_in[:, o + 520:o + 776]
    rep = lambda m: jnp.concatenate([m[:, 0:64], m[:, 0:64], m[:, 64:128], m[:, 64:128]], axis=1)
    pad = lambda n: jnp.zeros((w_in.shape[0], n), w_in.dtype)
    ssd = jnp.concatenate([x, rep(bm), rep(cm), z, dt, pad(120)], axis=1)
    o = 1792 + 776
    q, k, v = w_in[:, o:o + 128], w_in[:, o + 128:o + 256], w_in[:, o + 256:o + 512]
    gl, gate = w_in[:, o + 512:o + 544], w_in[:, o + 544:o + 800]
    gla = jnp.concatenate([q, k, v, gate, gl, pad(96)], axis=1)
    return jnp.concatenate([rw, s5, ssd, gla], axis=1).astype(BF16)


def _layer_params(l, a):
    p = {k: v[l] for k, v in a.items()}
    rep = lambda m: jnp.concatenate([m[..., 0:64], m[..., 0:64], m[..., 64:128], m[..., 64:128]], axis=-1)
    cw, cb = p["ssd_conv_w"], p["ssd_conv_b"]
    p["conv_w_rep"] = jnp.concatenate([cw[:, 0:256], rep(cw[:, 256:384]), rep(cw[:, 384:512])], axis=1)
    p["conv_b_rep"] = jnp.concatenate([cb[0:256], rep(cb[256:384]), rep(cb[384:512])], axis=0)
    p["ssd_a_neg"] = -jnp.exp(p["ssd_a_log"])
    p["w_in_p"] = _permute_w_in(p["w_in"])
    p["s5_ops"] = _s5_weights(p)
    return p


def _mixer_layer(h, mod3, shared_mod_row, p, is_grid, init, need_out):
    b, L, d = h.shape
    tm = min(512, L)
    mod_row = (lambda i: i) if shared_mod_row is None else (lambda i: shared_mod_row)
    f_rw, f_s5, f_ssd, f_gla = _inproj(h, mod3, mod_row, p["norm_pre"], p["w_in_p"], tm)
    if init is None:
        init = (jnp.zeros((b, 2, HEAD, W_BR), F32), jnp.zeros((b, S5_G, 256), F32),
                jnp.zeros((b, 2, HEAD, W_BR), F32), jnp.zeros((b, 2, GLA_DK, W_BR), F32))
    (rt, y0, bonus, m_rw, c_rw), (ssd_c, ssd_d, ssd_xbc), (gla_c, gla_d) = _chunk_local(f_rw, f_ssd, f_gla, p, is_grid)
    rw_sf, rw_sr, rw_fin = _scan_call(_rw_scan_kernel, "rwkv_scan", m_rw, c_rw, init[0])
    kmat, fmat, emat, lam_re, lam_im = p["s5_ops"]
    ug, z5 = _s5_local(f_s5, fmat)
    s5_xf, s5_xr, s5_fin = _s5_scan(z5, lam_re, lam_im, init[1])
    ssd_sf, ssd_sr, ssd_fin = _scan_call(_ew_scan_kernel, "ssd_scan", ssd_c, ssd_d, init[2])
    gla_sf, gla_sr, gla_fin = _scan_call(_ew_scan_kernel, "gla_scan", gla_c, gla_d, init[3])
    states = (rw_fin, s5_fin, ssd_fin, gla_fin)
    if not need_out:
        return None, states
    y5a, y5b = _s5_out(ug, s5_xf, s5_xr, kmat, emat)
    h_new = _layer_out(h, mod3, shared_mod_row,
                       (y0, bonus, rt, rw_sf, rw_sr, f_rw), (y5a, y5b, f_s5),
                       (ssd_xbc, f_ssd, ssd_sf, ssd_sr), (f_gla, gla_sf, gla_sr), p)
    return h_new, states


def kernel(x, c, ctx, c_ctx, ada_w, ada_b, norm_pre, norm_post, w_in, w_out, rw_mu, rw_w0, rw_w2, rw_a0, rw_a2, rw_kk, rw_ka, rw_rk, rw_ln_w, rw_ln_b, s5_a_re, s5_a_im, s5_log_dt, s5_b_re, s5_b_im, s5_c_re, s5_c_im, s5_d, s5_glu_w, s5_glu_b, ssd_conv_w, ssd_conv_b, ssd_dt_bias, ssd_a_log, ssd_d, ssd_norm, gla_g2, gla_gb, gla_norm):
    stacked = dict(norm_pre=norm_pre, norm_post=norm_post, w_in=w_in, w_out=w_out, rw_mu=rw_mu, rw_w0=rw_w0,
                   rw_w2=rw_w2, rw_a0=rw_a0, rw_a2=rw_a2, rw_kk=rw_kk, rw_ka=rw_ka, rw_rk=rw_rk, rw_ln_w=rw_ln_w,
                   rw_ln_b=rw_ln_b, s5_a_re=s5_a_re, s5_a_im=s5_a_im, s5_log_dt=s5_log_dt, s5_b_re=s5_b_re,
                   s5_b_im=s5_b_im, s5_c_re=s5_c_re, s5_c_im=s5_c_im, s5_d=s5_d, s5_glu_w=s5_glu_w,
                   s5_glu_b=s5_glu_b, ssd_conv_w=ssd_conv_w, ssd_conv_b=ssd_conv_b, ssd_dt_bias=ssd_dt_bias,
                   ssd_a_log=ssd_a_log, ssd_d=ssd_d, ssd_norm=ssd_norm, gla_g2=gla_g2, gla_gb=gla_gb,
                   gla_norm=gla_norm)
    depth = ada_w.shape[0]
    b, d = c.shape
    rows = -(-(b + 1) // 8) * 8
    cvec = jnp.concatenate([c, c_ctx[None, :], jnp.zeros((rows - b - 1, d), F32)], axis=0)
    mod = _modulation(cvec, ada_w, ada_b)
    h, hc = x, ctx
    for l in range(depth):
        p = _layer_params(l, stacked)
        mod3 = mod[l].reshape(rows, 1, 3 * d)
        last = l == depth - 1
        hc_next, ctx_states = _mixer_layer(hc, mod3, b, p, False, None, not last)
        h, _ = _mixer_layer(h, mod3, None, p, True, ctx_states, True)
        hc = hc_next
    return h
```
